```python
import math
import jax, jax.numpy as jnp
from jax import lax
import numpy as np

D_MODEL = 2048
BATCH = 4
SEQ = 8192
DEPTH = 4
DEC_BATCH = 2
DEC_SEQ = 8192
PAST_LEN = 128

N_MEM = 256
N_BRANCH = 4
BRANCH_W = D_MODEL // 4
HY_W = BRANCH_W
HY_ORDER = 2
HY_EMB = 33
HY_BANDS = (HY_EMB - 1) // 2
HY_FFN = 64
HY_MIN_DECAY = math.log(1e-2) / 1.5
HY_MAX_DECAY = math.log(1e-2) / 0.3
HG_HEADS = 4
HG_DK = BRANCH_W // HG_HEADS
HG_DV = BRANCH_W // HG_HEADS
HG_CHUNK = 64
DA_HEADS = 4
DA_HD = BRANCH_W // (2 * DA_HEADS)
DA_VD = 2 * DA_HD
ROPE_DIM = DA_HD // 4
ROPE_THETA = 500000.0
Q_BLOCK = 128
MEM_HEADS = 4
MEM_HD = BRANCH_W // MEM_HEADS
N_EXPERTS = 16
EC_FACTOR = 2
EXPERT_FF = 2048
DN_ALPHA = (2 * DEPTH) ** 0.25
DN_BETA = (8 * DEPTH) ** -0.25
LN_EPS = 1e-5
HY_COLS = 3 * HY_W
HG_COLS = 5 * BRANCH_W
DA_COLS = 3 * BRANCH_W
ME_COLS = BRANCH_W
IN_W = HY_COLS + HG_COLS + DA_COLS + ME_COLS

kernel_name = 'hybrid_bidir_encoder_gated_branches_ec_moe'


def layer_norm(x, g, b):
    xf = x.astype(jnp.float32)
    mu = jnp.mean(xf, axis=-1, keepdims=True)
    var = jnp.mean(jnp.square(xf - mu), axis=-1, keepdims=True)
    return ((xf - mu) * lax.rsqrt(var + LN_EPS) * g + b).astype(x.dtype)


def rms_norm(x, g, eps=1e-6):
    xf = x.astype(jnp.float32)
    return (xf * lax.rsqrt(jnp.mean(xf * xf, axis=-1, keepdims=True) + eps) * g).astype(x.dtype)


def short_conv(u, w, b):
    up = jnp.pad(u, ((0, 0), (1, 1), (0, 0)))
    return up[:, :-2] * w[0] + up[:, 1:-1] * w[1] + up[:, 2:] * w[2] + b


def hyena_filters(L, w1, b1, w2, b2, w3, b3, freq):
    f32 = jnp.float32
    t = jnp.linspace(0.0, 1.0, L, dtype=f32)[:, None]
    w = 2.0 * math.pi * jnp.arange(L, dtype=f32)[:, None] / L
    fr = jnp.linspace(1e-4, HY_BANDS - 1, HY_BANDS, dtype=f32)[None, :]
    feats = jnp.concatenate([t, jnp.cos(fr * w), -jnp.sin(fr * w)], axis=-1)
    fq = freq.astype(f32)
    h = jnp.sin(fq[0] * (feats @ w1.astype(f32) + b1.astype(f32)))
    h = jnp.sin(fq[1] * (h @ w2.astype(f32) + b2.astype(f32)))
    h = h @ w3.astype(f32) + b3.astype(f32)
    deltas = jnp.abs(jnp.linspace(HY_MIN_DECAY, HY_MAX_DECAY, HY_W, dtype=f32))
    decay = jnp.exp(-t * deltas)
    h = h.reshape(L, HY_ORDER, 2, HY_W) * decay[:, None, None, :]
    h_fwd, h_bwd = h[:, :, 0], h[:, :, 1]
    k = jnp.concatenate([h_fwd, jnp.zeros_like(h_fwd[:1]), h_bwd[1:][::-1]], axis=0)
    return k / jnp.sum(jnp.abs(k), axis=0, keepdims=True)


def long_conv(u, k_hat, skip):
    L = u.shape[1]
    uf = u.astype(jnp.float32)
    y = jnp.fft.irfft(jnp.fft.rfft(uf, n=2 * L, axis=1) * k_hat, n=2 * L, axis=1)[:, :L]
    return (y + uf * skip).astype(u.dtype)


def hyena_mixer(z, short_w, short_b, w1, b1, w2, b2, w3, b3, freq, skip):
    L = z.shape[1]
    u = short_conv(z, short_w, short_b)
    v, x1, x2 = jnp.split(u, 3, axis=-1)
    k_hat = jnp.fft.rfft(hyena_filters(L, w1, b1, w2, b2, w3, b3, freq), axis=0)
    v = x1 * long_conv(v, k_hat[:, 0], skip[0])
    v = x2 * long_conv(v, k_hat[:, 1], skip[1])
    return v


def gla_chunk_scan(q, k, v, log_f):
    B, L, H, DK = q.shape
    DV = v.shape[-1]
    n = L // HG_CHUNK

    def to_chunks(a):
        return a.reshape(B, n, HG_CHUNK, H, a.shape[-1]).transpose(1, 0, 3, 2, 4)

    qc, kc, vc, gc = to_chunks(q), to_chunks(k), to_chunks(v), to_chunks(log_f)
    mask = jnp.tril(jnp.ones((HG_CHUNK, HG_CHUNK), dtype=bool))[None, None, :, :, None]

    def step(S, inp):
        qb, kb, vb, gb = inp
        b = jnp.cumsum(gb, axis=2)
        rel = jnp.where(mask, b[:, :, :, None, :] - b[:, :, None, :, :], -jnp.inf)
        A = jnp.einsum('bhtk,bhsk,bhtsk->bhts', qb, kb, jnp.exp(rel))
        o = jnp.einsum('bhts,bhsv->bhtv', A, vb) + jnp.einsum('bhtk,bhkv->bhtv', qb * jnp.exp(b), S)
        b_last = b[:, :, -1:, :]
        S = jnp.exp(b_last[:, :, 0, :, None]) * S + jnp.einsum('bhsk,bhsv->bhkv', kb * jnp.exp(b_last - b), vb)
        return S, o

    S0 = jnp.zeros((B, H, DK, DV), jnp.float32)
    _, o = lax.scan(step, S0, (qc, kc, vc, gc))
    return o.transpose(1, 0, 3, 2, 4).reshape(B, L, H, DV)


def hgrn2_mixer(z, lb_fwd, lb_bwd, norm_g):
    B, L, _ = z.shape
    zq, zi, zf, zb, zg = jnp.split(z.astype(jnp.float32), 5, axis=-1)
    q = jax.nn.silu(zq).reshape(B, L, HG_HEADS, HG_DK)
    i = zi.reshape(B, L, HG_HEADS, HG_DV)

    def one_direction(zdir, lb, rev):
        log_f = jnp.logaddexp(jnp.log(lb), jnp.log1p(-lb) + jax.nn.log_sigmoid(zdir))
        k = (1.0 - lb) * jax.nn.sigmoid(-zdir)
        qd = q
        kd = k.reshape(B, L, HG_HEADS, HG_DK)
        vd = i
        gd = log_f.reshape(B, L, HG_HEADS, HG_DK)
        if rev:
            qd, kd, vd, gd = jnp.flip(qd, 1), jnp.flip(kd, 1), jnp.flip(vd, 1), jnp.flip(gd, 1)
        o = gla_chunk_scan(qd, kd, vd, gd)
        return jnp.flip(o, 1) if rev else o

    o = one_direction(zf, lb_fwd, False) + one_direction(zb, lb_bwd, True)
    o = rms_norm(o, norm_g) * jax.nn.silu(zg).reshape(B, L, HG_HEADS, HG_DV)
    return o.reshape(B, L, BRANCH_W).astype(z.dtype)


def rope_tables(L):
    inv = ROPE_THETA ** (-jnp.arange(0, ROPE_DIM, 2, dtype=jnp.float32) / ROPE_DIM)
    ang = jnp.arange(L, dtype=jnp.float32)[:, None] * inv[None, :]
    return jnp.cos(ang), jnp.sin(ang)


def partial_rope(x, cos, sin):
    c = cos[None, :, None, None, :].astype(x.dtype)
    s = sin[None, :, None, None, :].astype(x.dtype)
    half = ROPE_DIM // 2
    x1, x2, xp = x[..., :half], x[..., half:ROPE_DIM], x[..., ROPE_DIM:]
    return jnp.concatenate([x1 * c - x2 * s, x2 * c + x1 * s, xp], axis=-1)


def diff_attention(z, cos, sin, lam_params, norm_g, lam_init):
    B, L, _ = z.shape
    zq, zk, zv = jnp.split(z, 3, axis=-1)
    q = partial_rope(zq.reshape(B, L, DA_HEADS, 2, DA_HD), cos, sin)
    k = partial_rope(zk.reshape(B, L, DA_HEADS, 2, DA_HD), cos, sin)
    v = zv.reshape(B, L, DA_HEADS, DA_VD).transpose(0, 2, 1, 3)
    lp = lam_params.astype(jnp.float32)
    lam = jnp.exp(jnp.sum(lp[0] * lp[1])) - jnp.exp(jnp.sum(lp[2] * lp[3])) + lam_init
    kt = k.transpose(0, 2, 3, 1, 4)
    nb = L // Q_BLOCK
    qb = q.reshape(B, nb, Q_BLOCK, DA_HEADS, 2, DA_HD).transpose(1, 0, 3, 4, 2, 5)
    scale = DA_HD ** -0.5

    def block(qblk):
        s = jnp.einsum('bhcqd,bhckd->bhcqk', qblk, kt).astype(jnp.float32) * scale
        p = jax.nn.softmax(s, axis=-1)
        a = p[:, :, 0] - lam * p[:, :, 1]
        return jnp.einsum('bhqk,bhkv->bhqv', a.astype(v.dtype), v)

    o = lax.map(block, qb)
    o = o.transpose(1, 0, 3, 2, 4).reshape(B, L, DA_HEADS, DA_VD)
    o = rms_norm(o, norm_g) * (1.0 - lam_init)
    return o.reshape(B, L, BRANCH_W)


def memory_attention(z, mem, w_kv):
    B, L, _ = z.shape
    M = mem.shape[1]
    q = z.reshape(B, L, MEM_HEADS, MEM_HD)
    k, v = jnp.split(mem @ w_kv, 2, axis=-1)
    k = k.reshape(B, M, MEM_HEADS, MEM_HD)
    v = v.reshape(B, M, MEM_HEADS, MEM_HD)
    s = jnp.einsum('blhd,bmhd->bhlm', q, k).astype(jnp.float32) * MEM_HD ** -0.5
    p = jax.nn.softmax(s, axis=-1).astype(v.dtype)
    return jnp.einsum('bhlm,bmhd->blhd', p, v).reshape(B, L, BRANCH_W)


def ec_moe(x, w_router, w1, w3, w2):
    B, L, D = x.shape
    T = B * L
    xf = x.reshape(T, D)
    aff = jax.nn.softmax((xf @ w_router).astype(jnp.float32), axis=-1)
    cap = EC_FACTOR * T // N_EXPERTS
    gate, idx = lax.top_k(aff.T, cap)
    xe = xf[idx]
    h = jax.nn.silu(jnp.einsum('ecd,edf->ecf', xe, w1)) * jnp.einsum('ecd,edf->ecf', xe, w3)
    ye = jnp.einsum('ecf,efd->ecd', h, w2) * gate[..., None].astype(x.dtype)
    y = jnp.zeros_like(xf).at[idx.reshape(-1)].add(ye.reshape(-1, D))
    return y.reshape(B, L, D)


def trunk(x, mem, ln_in_g, ln_in_b, w_in, hy_short_w, hy_short_b, hy_ffn_w1, hy_ffn_b1,
          hy_ffn_w2, hy_ffn_b2, hy_ffn_w3, hy_ffn_b3, hy_freq, hy_skip, hg_lb_raw, hg_norm_g,
          da_lambda, da_norm_g, w_mem_kv, w_gate, b_gate, w_br, w_out, ln1_g, ln1_b,
          w_router, w_e1, w_e3, w_e2, ln2_g, ln2_b):
    L = x.shape[1]
    cos, sin = rope_tables(L)
    lb_all = jnp.cumsum(jax.nn.softmax(hg_lb_raw.astype(jnp.float32), axis=1), axis=1)
    lb_all = lb_all - lb_all[:, :1]
    x = layer_norm(x, ln_in_g, ln_in_b)
    for l in range(DEPTH):
        z = x @ w_in[l]
        o0 = HY_COLS
        o1 = o0 + HG_COLS
        o2 = o1 + DA_COLS
        o_hy = hyena_mixer(z[..., :o0], hy_short_w[l], hy_short_b[l], hy_ffn_w1[l], hy_ffn_b1[l],
                           hy_ffn_w2[l], hy_ffn_b2[l], hy_ffn_w3[l], hy_ffn_b3[l], hy_freq[l], hy_skip[l])
        o_hg = hgrn2_mixer(z[..., o0:o1], lb_all[0, l], lb_all[1, l], hg_norm_g[l])
        lam_init = 0.8 - 0.6 * math.exp(-0.3 * l)
        o_da = diff_attention(z[..., o1:o2], cos, sin, da_lambda[l], da_norm_g[l], lam_init)
        o_me = memory_attention(z[..., o2:], mem, w_mem_kv[l])
        branches = (o_hy, o_hg, o_da, o_me)
        merged = jax.nn.sigmoid(x @ w_gate[l, 0] + b_gate[l, 0]) * (branches[0] @ w_br[l, 0])
        for bi in range(1, N_BRANCH):
            merged = merged + jax.nn.sigmoid(x @ w_gate[l, bi] + b_gate[l, bi]) * (branches[bi] @ w_br[l, bi])
        x = layer_norm(DN_ALPHA * x + merged @ w_out[l], ln1_g[l], ln1_b[l])
        x = layer_norm(DN_ALPHA * x + ec_moe(x, w_router[l], w_e1[l], w_e3[l], w_e2[l]), ln2_g[l], ln2_b[l])
    return x


def setup_inputs(seed: int = 0) -> dict:
    key = jax.random.key(seed)
    ks = jax.random.split(key, 40)
    f32 = jnp.float32

    def nrm(k, shape, scale):
        return jax.random.normal(k, shape, f32) * scale

    D, E, F = D_MODEL, N_EXPERTS, EXPERT_FF
    return {
        'x_prompt': nrm(ks[0], (BATCH, SEQ, D), 1.0),
        'x_sample': nrm(ks[1], (DEC_BATCH, DEC_SEQ, D), 1.0),
        'mem_prompt': nrm(ks[2], (BATCH, N_MEM, D), 1.0),
        'mem_sample': nrm(ks[3], (DEC_BATCH, N_MEM, D), 1.0),
        'ln_in_g': 1.0 + nrm(ks[4], (D,), 0.02),
        'ln_in_b': nrm(ks[5], (D,), 0.02),
        'w_in': nrm(ks[6], (DEPTH, D, IN_W), D ** -0.5),
        'hy_short_w': nrm(ks[7], (DEPTH, 3, HY_COLS), 3 ** -0.5),
        'hy_short_b': nrm(ks[8], (DEPTH, HY_COLS), 0.02),
        'hy_ffn_w1': nrm(ks[9], (DEPTH, HY_EMB, HY_FFN), HY_EMB ** -0.5),
        'hy_ffn_b1': nrm(ks[10], (DEPTH, HY_FFN), 0.02),
        'hy_ffn_w2': nrm(ks[11], (DEPTH, HY_FFN, HY_FFN), HY_FFN ** -0.5),
        'hy_ffn_b2': nrm(ks[12], (DEPTH, HY_FFN), 0.02),
        'hy_ffn_w3': nrm(ks[13], (DEPTH, HY_FFN, HY_ORDER * 2 * HY_W), HY_FFN ** -0.5),
        'hy_ffn_b3': nrm(ks[14], (DEPTH, HY_ORDER * 2 * HY_W), 0.02),
        'hy_freq': 1.0 + nrm(ks[15], (DEPTH, 2, HY_FFN), 0.02),
        'hy_skip': nrm(ks[16], (DEPTH, HY_ORDER, HY_W), 0.5),
        'hg_lb_raw': nrm(ks[17], (2, DEPTH, BRANCH_W), 0.1),
        'hg_norm_g': 1.0 + nrm(ks[18], (DEPTH, HG_DV), 0.02),
        'da_lambda': nrm(ks[19], (DEPTH, 4, DA_HD), 0.1),
        'da_norm_g': 1.0 + nrm(ks[20], (DEPTH, DA_VD), 0.02),
        'w_mem_kv': nrm(ks[21], (DEPTH, D, 2 * BRANCH_W), D ** -0.5),
        'w_gate': nrm(ks[22], (DEPTH, N_BRANCH, D, D), D ** -0.5),
        'b_gate': nrm(ks[23], (DEPTH, N_BRANCH, D), 0.02),
        'w_br': nrm(ks[24], (DEPTH, N_BRANCH, BRANCH_W, D), BRANCH_W ** -0.5),
        'w_out': nrm(ks[25], (DEPTH, D, D), DN_BETA * D ** -0.5),
        'ln1_g': 1.0 + nrm(ks[26], (DEPTH, D), 0.02),
        'ln1_b': nrm(ks[27], (DEPTH, D), 0.02),
        'w_router': nrm(ks[28], (DEPTH, D, E), D ** -0.5),
        'w_e1': nrm(ks[29], (DEPTH, E, D, F), D ** -0.5),
        'w_e3': nrm(ks[30], (DEPTH, E, D, F), D ** -0.5),
        'w_e2': nrm(ks[31], (DEPTH, E, F, D), DN_BETA * F ** -0.5),
        'ln2_g': 1.0 + nrm(ks[32], (DEPTH, D), 0.02),
        'ln2_b': nrm(ks[33], (DEPTH, D), 0.02),
    }


def reference(x_prompt, x_sample, mem_prompt, mem_sample, ln_in_g, ln_in_b, w_in, hy_short_w,
              hy_short_b, hy_ffn_w1, hy_ffn_b1, hy_ffn_w2, hy_ffn_b2, hy_ffn_w3, hy_ffn_b3, hy_freq,
              hy_skip, hg_lb_raw, hg_norm_g, da_lambda, da_norm_g, w_mem_kv, w_gate, b_gate, w_br,
              w_out, ln1_g, ln1_b, w_router, w_e1, w_e3, w_e2, ln2_g, ln2_b):
    weights = (ln_in_g, ln_in_b, w_in, hy_short_w, hy_short_b, hy_ffn_w1, hy_ffn_b1, hy_ffn_w2,
               hy_ffn_b2, hy_ffn_w3, hy_ffn_b3, hy_freq, hy_skip, hg_lb_raw, hg_norm_g, da_lambda,
               da_norm_g, w_mem_kv, w_gate, b_gate, w_br, w_out, ln1_g, ln1_b, w_router, w_e1,
               w_e3, w_e2, ln2_g, ln2_b)
    y_prompt = trunk(x_prompt, mem_prompt, *weights)
    y_sample = trunk(x_sample, mem_sample, *weights)
    return (y_prompt, y_sample)
```

```python
import functools
import math

import jax
import jax.numpy as jnp
from jax import lax
from jax.experimental import pallas as pl
from jax.experimental.pallas import tpu as pltpu

F32 = jnp.float32
BF16 = jnp.bfloat16

D_MODEL = 2048
DEPTH = 4
N_BRANCH = 4
BRANCH_W = D_MODEL // 4
HY_W = BRANCH_W
HY_ORDER = 2
HY_EMB = 33
HY_BANDS = (HY_EMB - 1) // 2
HY_MIN_DECAY = math.log(1e-2) / 1.5
HY_MAX_DECAY = math.log(1e-2) / 0.3
HG_HEADS = 4
HG_DK = BRANCH_W // HG_HEADS
HG_DV = BRANCH_W // HG_HEADS
HG_CHUNK = 64
DA_HEADS = 4
DA_HD = BRANCH_W // (2 * DA_HEADS)
DA_VD = 2 * DA_HD
ROPE_DIM = DA_HD // 4
ROPE_THETA = 500000.0
MEM_HEADS = 4
MEM_HD = BRANCH_W // MEM_HEADS
N_EXPERTS = 16
EC_FACTOR = 2
EXPERT_FF = 2048
DN_ALPHA = (2 * DEPTH) ** 0.25
LN_EPS = 1e-5
HY_COLS = 3 * HY_W
HG_COLS = 5 * BRANCH_W
DA_COLS = 3 * BRANCH_W
ME_COLS = BRANCH_W
IN_W = HY_COLS + HG_COLS + DA_COLS + ME_COLS
DA_OFF = HY_COLS + HG_COLS
ME_OFF = DA_OFF + DA_COLS

V7X_VMEM_LIMIT_BYTES = 56 * 1024 * 1024
LANE = 128


def _cparams(sem):
    return pltpu.CompilerParams(dimension_semantics=sem, vmem_limit_bytes=V7X_VMEM_LIMIT_BYTES)


def _mm_kernel(a_ref, w_ref, o_ref):
    o_ref[...] = jnp.dot(a_ref[...].astype(BF16), w_ref[...], preferred_element_type=F32)


def _matmul(a, w, tm, tn):
    m, k = a.shape
    n = w.shape[1]
    return pl.pallas_call(
        _mm_kernel,
        out_shape=jax.ShapeDtypeStruct((m, n), F32),
        grid=(m // tm, n // tn),
        in_specs=[pl.BlockSpec((tm, k), lambda i, j: (i, 0)), pl.BlockSpec((k, tn), lambda i, j: (0, j))],
        out_specs=pl.BlockSpec((tm, tn), lambda i, j: (i, j)),
        compiler_params=_cparams(("parallel", "arbitrary")),
        name="dense_matmul",
    )(a, w)


def _ln_rows(y, g, b):
    mu = jnp.mean(y, axis=-1, keepdims=True)
    d = y - mu
    var = jnp.mean(d * d, axis=-1, keepdims=True)
    return d * lax.rsqrt(var + LN_EPS) * g + b


def _ln_kernel(alpha, x_ref, r_ref, g_ref, b_ref, o_ref):
    y = x_ref[...]
    if r_ref is not None:
        y = alpha * y + r_ref[...]
    o_ref[...] = _ln_rows(y, g_ref[...], b_ref[...])


def _layer_norm(x, g, b, resid=None, alpha=1.0, tm=512):
    t, d = x.shape
    row = pl.BlockSpec((tm, d), lambda i: (i, 0))
    vec = pl.BlockSpec((1, d), lambda i: (0, 0))
    if resid is None:
        kern = lambda x_ref, g_ref, b_ref, o_ref: _ln_kernel(alpha, x_ref, None, g_ref, b_ref, o_ref)
        args, specs = (x, g.reshape(1, d), b.reshape(1, d)), [row, vec, vec]
    else:
        kern = functools.partial(_ln_kernel, alpha)
        args, specs = (x, resid, g.reshape(1, d), b.reshape(1, d)), [row, row, vec, vec]
    return pl.pallas_call(
        kern,
        out_shape=jax.ShapeDtypeStruct((t, d), F32),
        grid=(t // tm,),
        in_specs=specs,
        out_specs=row,
        compiler_params=_cparams(("parallel",)),
        name="layer_norm",
    )(*args)


def _da_kernel(lam_init, tk, lp_ref, g_ref, q_ref, k_ref, v_ref, o_ref):
    tq = q_ref.shape[0]
    nk = k_ref.shape[0] // tk
    q = q_ref[...]
    qs = (q[:, :DA_HD], q[:, DA_HD:])

    def body(i, carry):
        off = pl.multiple_of(i * tk, tk)
        k = k_ref[pl.ds(off, tk), :]
        v = v_ref[pl.ds(off, tk), :]
        out = []
        for c in range(2):
            m_prev, l_prev, acc_prev = carry[c]
            s = lax.dot_general(qs[c], k[:, c * DA_HD:(c + 1) * DA_HD], (((1,), (1,)), ((), ())),
                                preferred_element_type=F32)
            m_new = jnp.maximum(m_prev, jnp.max(s, axis=-1, keepdims=True))
            a = jnp.exp(m_prev - m_new)
            p = jnp.exp(s - m_new)
            l_new = a * l_prev + jnp.sum(p, axis=-1, keepdims=True)
            acc_new = a * acc_prev + jnp.dot(p.astype(BF16), v, preferred_element_type=F32)
            out.append((m_new, l_new, acc_new))
        return tuple(out)

    init = tuple((jnp.full((tq, 1), -jnp.inf, F32), jnp.zeros((tq, 1), F32), jnp.zeros((tq, DA_VD), F32))
                 for _ in range(2))
    (_, l0, a0), (_, l1, a1) = lax.fori_loop(0, nk, body, init)
    lp = lp_ref[...]
    lam = (jnp.exp(jnp.sum(lp[0:1] * lp[1:2], axis=-1, keepdims=True))
           - jnp.exp(jnp.sum(lp[2:3] * lp[3:4], axis=-1, keepdims=True)) + lam_init)
    o = a0 / l0 - lam * (a1 / l1)
    o = o * lax.rsqrt(jnp.mean(o * o, axis=-1, keepdims=True) + 1e-6) * g_ref[...]
    o_ref[...] = o * (1.0 - lam_init)


def _diff_attention(qr, kr, vr, lam_params, norm_g, lam_init, tq=512, tk=1024):
    b, h, l, _ = qr.shape
    nq = l // tq
    return pl.pallas_call(
        functools.partial(_da_kernel, lam_init, tk),
        out_shape=jax.ShapeDtypeStruct((b * l, BRANCH_W), F32),
        grid=(b, h, nq),
        in_specs=[
            pl.BlockSpec((4, DA_HD), lambda bi, hi, qi: (0, 0)),
            pl.BlockSpec((1, DA_VD), lambda bi, hi, qi: (0, 0)),
            pl.BlockSpec((None, None, tq, DA_VD), lambda bi, hi, qi: (bi, hi, qi, 0)),
            pl.BlockSpec((None, None, l, DA_VD), lambda bi, hi, qi: (bi, hi, 0, 0)),
            pl.BlockSpec((None, None, l, DA_VD), lambda bi, hi, qi: (bi, hi, 0, 0)),
        ],
        out_specs=pl.BlockSpec((tq, DA_VD), lambda bi, hi, qi: (bi * nq + qi, hi)),
        compiler_params=_cparams(("parallel", "parallel", "arbitrary")),
        name="diff_attention",
    )(lam_params, norm_g.reshape(1, DA_VD), qr, kr, vr)


def _mem_attn_kernel(q_ref, kv_ref, o_ref):
    q = q_ref[...]
    kv = kv_ref[...]
    scale = MEM_HD ** -0.5
    for h in range(MEM_HEADS):
        qh = q[:, h * MEM_HD:(h + 1) * MEM_HD].astype(BF16)
        kh = kv[:, h * MEM_HD:(h + 1) * MEM_HD].astype(BF16)
        vh = kv[:, BRANCH_W + h * MEM_HD:BRANCH_W + (h + 1) * MEM_HD].astype(BF16)
        s = lax.dot_general(qh, kh, (((1,), (1,)), ((), ())), preferred_element_type=F32) * scale
        e = jnp.exp(s - jnp.max(s, axis=-1, keepdims=True))
        p = e / jnp.sum(e, axis=-1, keepdims=True)
        o_ref[:, h * MEM_HD:(h + 1) * MEM_HD] = jnp.dot(p.astype(BF16), vh, preferred_element_type=F32)


def _memory_attention(z, kv, b, l, tm=512):
    nm = l // tm
    m = kv.shape[1]
    return pl.pallas_call(
        _mem_attn_kernel,
        out_shape=jax.ShapeDtypeStruct((b * l, BRANCH_W), F32),
        grid=(b, nm),
        in_specs=[
            pl.BlockSpec((tm, ME_COLS), lambda bi, i: (bi * nm + i, ME_OFF // ME_COLS)),
            pl.BlockSpec((None, m, 2 * BRANCH_W), lambda bi, i: (bi, 0, 0)),
        ],
        out_specs=pl.BlockSpec((tm, BRANCH_W), lambda bi, i: (bi * nm + i, 0)),
        compiler_params=_cparams(("parallel", "arbitrary")),
        name="memory_attention",
    )(z, kv)


def _merge_kernel(x_ref, o0_ref, o1_ref, o2_ref, o3_ref, wg_ref, bg_ref, wbr_ref, wout_ref, g_ref, b_ref,
                  out_ref, acc_ref, xb_ref, ob_ref):
    j = pl.program_id(1)

    @pl.when(j == 0)
    def _():
        xb_ref[...] = x_ref[...].astype(BF16)
        for bi, o_ref in enumerate((o0_ref, o1_ref, o2_ref, o3_ref)):
            ob_ref[bi] = o_ref[...].astype(BF16)
        acc_ref[...] = jnp.zeros_like(acc_ref)

    xb = xb_ref[...]
    merged = None
    for bi in range(N_BRANCH):
        gate = jax.nn.sigmoid(jnp.dot(xb, wg_ref[bi], preferred_element_type=F32) + bg_ref[bi])
        term = gate * jnp.dot(ob_ref[bi], wbr_ref[bi], preferred_element_type=F32)
        merged = term if merged is None else merged + term
    acc_ref[...] += jnp.dot(merged.astype(BF16), wout_ref[...], preferred_element_type=F32)

    @pl.when(j == pl.num_programs(1) - 1)
    def _():
        out_ref[...] = _ln_rows(DN_ALPHA * x_ref[...] + acc_ref[...], g_ref[...], b_ref[...])


def _merge(x, branches, w_gate, b_gate, w_br, w_out, ln_g, ln_b, tm=512, tn=256):
    t, d = x.shape
    row = pl.BlockSpec((tm, d), lambda i, j: (i, 0))
    brow = pl.BlockSpec((tm, BRANCH_W), lambda i, j: (i, 0))
    vec = pl.BlockSpec((1, d), lambda i, j: (0, 0))
    return pl.pallas_call(
        _merge_kernel,
        out_shape=jax.ShapeDtypeStruct((t, d), F32),
        grid=(t // tm, d // tn),
        in_specs=[
            row, brow, brow, brow, brow,
            pl.BlockSpec((N_BRANCH, d, tn), lambda i, j: (0, 0, j)),
            pl.BlockSpec((N_BRANCH, 1, tn), lambda i, j: (0, 0, j)),
            pl.BlockSpec((N_BRANCH, BRANCH_W, tn), lambda i, j: (0, 0, j)),
            pl.BlockSpec((tn, d), lambda i, j: (j, 0)),
            vec, vec,
        ],
        out_specs=row,
        scratch_shapes=[pltpu.VMEM((tm, d), F32), pltpu.VMEM((tm, d), BF16),
                        pltpu.VMEM((N_BRANCH, tm, BRANCH_W), BF16)],
        compiler_params=_cparams(("parallel", "arbitrary")),
        name="gated_merge",
    )(x, *branches, w_gate, b_gate.reshape(N_BRANCH, 1, d), w_br, w_out, ln_g.reshape(1, d), ln_b.reshape(1, d))


def _expert_kernel(x_ref, gate_ref, w1_ref, w3_ref, w2_ref, o_ref):
    f = pl.program_id(2)
    x = x_ref[...]
    h1 = jnp.dot(x, w1_ref[...], preferred_element_type=F32)
    h3 = jnp.dot(x, w3_ref[...], preferred_element_type=F32)
    h = (jax.nn.silu(h1) * h3).astype(BF16)
    part = jnp.dot(h, w2_ref[...], preferred_element_type=F32)

    @pl.when(f == 0)
    def _():
        o_ref[...] = part

    @pl.when(f != 0)
    def _():
        o_ref[...] += part

    @pl.when(f == pl.num_programs(2) - 1)
    def _():
        o_ref[...] = o_ref[...] * gate_ref[...]


def _expert_ffn(xe, gate, w1, w3, w2, tm=1024, tf=512):
    e, c, d = xe.shape
    ff = w1.shape[2]
    tm = min(tm, c)
    return pl.pallas_call(
        _expert_kernel,
        out_shape=jax.ShapeDtypeStruct((e, c, d), F32),
        grid=(e, c // tm, ff // tf),
        in_specs=[
            pl.BlockSpec((None, tm, d), lambda ei, i, f: (ei, i, 0)),
            pl.BlockSpec((None, tm, 1), lambda ei, i, f: (ei, i, 0)),
            pl.BlockSpec((None, d, tf), lambda ei, i, f: (ei, 0, f)),
            pl.BlockSpec((None, d, tf), lambda ei, i, f: (ei, 0, f)),
            pl.BlockSpec((None, tf, d), lambda ei, i, f: (ei, f, 0)),
        ],
        out_specs=pl.BlockSpec((None, tm, d), lambda ei, i, f: (ei, i, 0)),
        compiler_params=_cparams(("parallel", "parallel", "arbitrary")),
        name="expert_ffn",
    )(xe, gate, w1, w3, w2)


def _short_conv(u, w, b):
    up = jnp.pad(u, ((0, 0), (1, 1), (0, 0)))
    return up[:, :-2] * w[0] + up[:, 1:-1] * w[1] + up[:, 2:] * w[2] + b


def _hyena_filters(L, w1, b1, w2, b2, w3, b3, freq):
    t = jnp.linspace(0.0, 1.0, L, dtype=F32)[:, None]
    w = 2.0 * math.pi * jnp.arange(L, dtype=F32)[:, None] / L
    fr = jnp.linspace(1e-4, HY_BANDS - 1, HY_BANDS, dtype=F32)[None, :]
    feats = jnp.concatenate([t, jnp.cos(fr * w), -jnp.sin(fr * w)], axis=-1)
    h = jnp.sin(freq[0] * (feats @ w1 + b1))
    h = jnp.sin(freq[1] * (h @ w2 + b2))
    h = h @ w3 + b3
    deltas = jnp.abs(jnp.linspace(HY_MIN_DECAY, HY_MAX_DECAY, HY_W, dtype=F32))
    decay = jnp.exp(-t * deltas)
    h = h.reshape(L, HY_ORDER, 2, HY_W) * decay[:, None, None, :]
    h_fwd, h_bwd = h[:, :, 0], h[:, :, 1]
    k = jnp.concatenate([h_fwd, jnp.zeros_like(h_fwd[:1]), h_bwd[1:][::-1]], axis=0)
    return k / jnp.sum(jnp.abs(k), axis=0, keepdims=True)


def _long_conv(u, k_hat, skip):
    L = u.shape[1]
    y = jnp.fft.irfft(jnp.fft.rfft(u, n=2 * L, axis=1) * k_hat, n=2 * L, axis=1)[:, :L]
    return y + u * skip


def _hyena_mixer(z, short_w, short_b, w1, b1, w2, b2, w3, b3, freq, skip):
    L = z.shape[1]
    u = _short_conv(z, short_w, short_b)
    v, x1, x2 = jnp.split(u, 3, axis=-1)
    k_hat = jnp.fft.rfft(_hyena_filters(L, w1, b1, w2, b2, w3, b3, freq), axis=0)
    v = x1 * _long_conv(v, k_hat[:, 0], skip[0])
    v = x2 * _long_conv(v, k_hat[:, 1], skip[1])
    return v


def _gla_chunk_scan(q, k, v, log_f):
    B, L, H, DK = q.shape
    DV = v.shape[-1]
    n = L // HG_CHUNK

    def to_chunks(a):
        return a.reshape(B, n, HG_CHUNK, H, a.shape[-1]).transpose(1, 0, 3, 2, 4)

    qc, kc, vc, gc = to_chunks(q), to_chunks(k), to_chunks(v), to_chunks(log_f)
    mask = jnp.tril(jnp.ones((HG_CHUNK, HG_CHUNK), dtype=bool))[None, None, :, :, None]

    def step(S, inp):
        qb, kb, vb, gb = inp
        b = jnp.cumsum(gb, axis=2)
        rel = jnp.where(mask, b[:, :, :, None, :] - b[:, :, None, :, :], -jnp.inf)
        A = jnp.einsum('bhtk,bhsk,bhtsk->bhts', qb, kb, jnp.exp(rel))
        o = jnp.einsum('bhts,bhsv->bhtv', A, vb) + jnp.einsum('bhtk,bhkv->bhtv', qb * jnp.exp(b), S)
        b_last = b[:, :, -1:, :]
        S = jnp.exp(b_last[:, :, 0, :, None]) * S + jnp.einsum('bhsk,bhsv->bhkv', kb * jnp.exp(b_last - b), vb)
        return S, o

    S0 = jnp.zeros((B, H, DK, DV), F32)
    _, o = lax.scan(step, S0, (qc, kc, vc, gc))
    return o.transpose(1, 0, 3, 2, 4).reshape(B, L, H, DV)


def _rms_norm(x, g, eps=1e-6):
    return x * lax.rsqrt(jnp.mean(x * x, axis=-1, keepdims=True) + eps) * g


def _hgrn2_mixer(z, lb_fwd, lb_bwd, norm_g):
    B, L, _ = z.shape
    zq, zi, zf, zb, zg = jnp.split(z, 5, axis=-1)
    q = jax.nn.silu(zq).reshape(B, L, HG_HEADS, HG_DK)
    i = zi.reshape(B, L, HG_HEADS, HG_DV)

    def one_direction(zdir, lb, rev):
        log_f = jnp.logaddexp(jnp.log(lb), jnp.log1p(-lb) + jax.nn.log_sigmoid(zdir))
        k = (1.0 - lb) * jax.nn.sigmoid(-zdir)
        qd, kd, vd = q, k.reshape(B, L, HG_HEADS, HG_DK), i
        gd = log_f.reshape(B, L, HG_HEADS, HG_DK)
        if rev:
            qd, kd, vd, gd = jnp.flip(qd, 1), jnp.flip(kd, 1), jnp.flip(vd, 1), jnp.flip(gd, 1)
        o = _gla_chunk_scan(qd, kd, vd, gd)
        return jnp.flip(o, 1) if rev else o

    o = one_direction(zf, lb_fwd, False) + one_direction(zb, lb_bwd, True)
    o = _rms_norm(o, norm_g) * jax.nn.silu(zg).reshape(B, L, HG_HEADS, HG_DV)
    return o.reshape(B, L, BRANCH_W)


def _rope_tables(L):
    inv = ROPE_THETA ** (-jnp.arange(0, ROPE_DIM, 2, dtype=F32) / ROPE_DIM)
    ang = jnp.arange(L, dtype=F32)[:, None] * inv[None, :]
    return jnp.cos(ang), jnp.sin(ang)


def _partial_rope(x, cos, sin):
    c = cos[None, :, None, None, :]
    s = sin[None, :, None, None, :]
    half = ROPE_DIM // 2
    x1, x2, xp = x[..., :half], x[..., half:ROPE_DIM], x[..., ROPE_DIM:]
    return jnp.concatenate([x1 * c - x2 * s, x2 * c + x1 * s, xp], axis=-1)


def _trunk(x, mem, p):
    B, L, D = x.shape
    T = B * L
    cos, sin = _rope_tables(L)
    lb_all = jnp.cumsum(jax.nn.softmax(p['hg_lb_raw'], axis=1), axis=1)
    lb_all = lb_all - lb_all[:, :1]
    xf = _layer_norm(x.reshape(T, D), p['ln_in_g'], p['ln_in_b'])
    memf = mem.reshape(B * mem.shape[1], D)
    cap = EC_FACTOR * T // N_EXPERTS
    for l in range(DEPTH):
        z = _matmul(xf, p['w_in'][l], 1024, 1024)
        z3 = z.reshape(B, L, IN_W)
        o_hy = _hyena_mixer(z3[..., :HY_COLS], p['hy_short_w'][l], p['hy_short_b'][l], p['hy_ffn_w1'][l],
                            p['hy_ffn_b1'][l], p['hy_ffn_w2'][l], p['hy_ffn_b2'][l], p['hy_ffn_w3'][l],
                            p['hy_ffn_b3'][l], p['hy_freq'][l], p['hy_skip'][l]).reshape(T, BRANCH_W)
        o_hg = _hgrn2_mixer(z3[..., HY_COLS:DA_OFF], lb_all[0, l], lb_all[1, l],
                            p['hg_norm_g'][l]).reshape(T, BRANCH_W)
        zq, zk, zv = jnp.split(z3[..., DA_OFF:ME_OFF], 3, axis=-1)
        q = _partial_rope(zq.reshape(B, L, DA_HEADS, 2, DA_HD), cos, sin) * (DA_HD ** -0.5)
        k = _partial_rope(zk.reshape(B, L, DA_HEADS, 2, DA_HD), cos, sin)
        qr = q.reshape(B, L, DA_HEADS, DA_VD).transpose(0, 2, 1, 3).astype(BF16)
        kr = k.reshape(B, L, DA_HEADS, DA_VD).transpose(0, 2, 1, 3).astype(BF16)
        vr = zv.reshape(B, L, DA_HEADS, DA_VD).transpose(0, 2, 1, 3).astype(BF16)
        lam_init = 0.8 - 0.6 * math.exp(-0.3 * l)
        o_da = _diff_attention(qr, kr, vr, p['da_lambda'][l], p['da_norm_g'][l], lam_init)
        kv = _matmul(memf, p['w_mem_kv'][l], min(512, memf.shape[0]), 512).reshape(B, -1, 2 * BRANCH_W)
        o_me = _memory_attention(z, kv, B, L)
        xf = _merge(xf, (o_hy, o_hg, o_da, o_me), p['w_gate'][l], p['b_gate'][l], p['w_br'][l], p['w_out'][l],
                    p['ln1_g'][l], p['ln1_b'][l])
        logits = _matmul(xf, p['w_router_pad'][l], 1024, LANE)[:, :N_EXPERTS]
        aff = jax.nn.softmax(logits, axis=-1)
        gate, idx = lax.top_k(aff.T, cap)
        xe = xf.astype(BF16)[idx]
        ye = _expert_ffn(xe, gate[..., None], p['w_e1'][l], p['w_e3'][l], p['w_e2'][l])
        y = jnp.zeros_like(xf).at[idx.reshape(-1)].add(ye.reshape(-1, D))
        xf = _layer_norm(xf, p['ln2_g'][l], p['ln2_b'][l], resid=y, alpha=DN_ALPHA)
    return xf.reshape(B, L, D)


def kernel(x_prompt, x_sample, mem_prompt, mem_sample, ln_in_g, ln_in_b, w_in, hy_short_w, hy_short_b, hy_ffn_w1, hy_ffn_b1, hy_ffn_w2, hy_ffn_b2, hy_ffn_w3, hy_ffn_b3, hy_freq, hy_skip, hg_lb_raw, hg_norm_g, da_lambda, da_norm_g, w_mem_kv, w_gate, b_gate, w_br, w_out, ln1_g, ln1_b, w_router, w_e1, w_e3, w_e2, ln2_g, ln2_b):
    p = dict(ln_in_g=ln_in_g, ln_in_b=ln_in_b, hy_short_w=hy_short_w, hy_short_b=hy_short_b,
             hy_ffn_w1=hy_ffn_w1, hy_ffn_b1=hy_ffn_b1, hy_ffn_w2=hy_ffn_w2, hy_ffn_b2=hy_ffn_b2,
             hy_ffn_w3=hy_ffn_w3, hy_ffn_b3=hy_ffn_b3, hy_freq=hy_freq, hy_skip=hy_skip, hg_lb_raw=hg_lb_raw,
             hg_norm_g=hg_norm_g, da_lambda=da_lambda, da_norm_g=da_norm_g, b_gate=b_gate,
             ln1_g=ln1_g, ln1_b=ln1_b, ln2_g=ln2_g, ln2_b=ln2_b)
    for name, w in (('w_in', w_in), ('w_mem_kv', w_mem_kv), ('w_gate', w_gate), ('w_br', w_br), ('w_out', w_out),
                    ('w_e1', w_e1), ('w_e3', w_e3), ('w_e2', w_e2)):
        p[name] = w.astype(BF16)
    p['w_router_pad'] = jnp.pad(w_router, ((0, 0), (0, 0), (0, LANE - N_EXPERTS))).astype(BF16)
    y_prompt = _trunk(x_prompt, mem_prompt, p)
    y_sample = _trunk(x_sample, mem_sample, p)
    return (y_prompt, y_sample)
```

```python
import functools
import math

import jax
import jax.numpy as jnp
from jax import lax
from jax.experimental import pallas as pl
from jax.experimental.pallas import tpu as pltpu

F32 = jnp.float32
BF16 = jnp.bfloat16

D_MODEL = 2048
DEPTH = 4
N_BRANCH = 4
BRANCH_W = D_MODEL // 4
HY_W = BRANCH_W
HY_ORDER = 2
HY_EMB = 33
HY_BANDS = (HY_EMB - 1) // 2
HY_MIN_DECAY = math.log(1e-2) / 1.5
HY_MAX_DECAY = math.log(1e-2) / 0.3
HG_HEADS = 4
HG_DK = BRANCH_W // HG_HEADS
HG_DV = BRANCH_W // HG_HEADS
HG_CHUNK = 64
DA_HEADS = 4
DA_HD = BRANCH_W // (2 * DA_HEADS)
DA_VD = 2 * DA_HD
ROPE_DIM = DA_HD // 4
ROPE_THETA = 500000.0
MEM_HEADS = 4
MEM_HD = BRANCH_W // MEM_HEADS
N_EXPERTS = 16
EC_FACTOR = 2
EXPERT_FF = 2048
DN_ALPHA = (2 * DEPTH) ** 0.25
LN_EPS = 1e-5
HY_COLS = 3 * HY_W
HG_COLS = 5 * BRANCH_W
DA_COLS = 3 * BRANCH_W
ME_COLS = BRANCH_W
IN_W = HY_COLS + HG_COLS + DA_COLS + ME_COLS
DA_OFF = HY_COLS + HG_COLS
ME_OFF = DA_OFF + DA_COLS

V7X_VMEM_LIMIT_BYTES = 56 * 1024 * 1024
LANE = 128


def _cparams(sem):
    return pltpu.CompilerParams(dimension_semantics=sem, vmem_limit_bytes=V7X_VMEM_LIMIT_BYTES)


def _mm_kernel(a_ref, w_ref, o_ref):
    o_ref[...] = jnp.dot(a_ref[...].astype(BF16), w_ref[...], preferred_element_type=F32)


def _matmul(a, w, tm, tn):
    m, k = a.shape
    n = w.shape[1]
    return pl.pallas_call(
        _mm_kernel,
        out_shape=jax.ShapeDtypeStruct((m, n), F32),
        grid=(m // tm, n // tn),
        in_specs=[pl.BlockSpec((tm, k), lambda i, j: (i, 0)), pl.BlockSpec((k, tn), lambda i, j: (0, j))],
        out_specs=pl.BlockSpec((tm, tn), lambda i, j: (i, j)),
        compiler_params=_cparams(("parallel", "arbitrary")),
        name="dense_matmul",
    )(a, w)


def _ln_rows(y, g, b):
    mu = jnp.mean(y, axis=-1, keepdims=True)
    d = y - mu
    var = jnp.mean(d * d, axis=-1, keepdims=True)
    return d * lax.rsqrt(var + LN_EPS) * g + b


def _ln_kernel(alpha, x_ref, r_ref, g_ref, b_ref, o_ref):
    y = x_ref[...]
    if r_ref is not None:
        y = alpha * y + r_ref[...]
    o_ref[...] = _ln_rows(y, g_ref[...], b_ref[...])


def _layer_norm(x, g, b, resid=None, alpha=1.0, tm=512):
    t, d = x.shape
    row = pl.BlockSpec((tm, d), lambda i: (i, 0))
    vec = pl.BlockSpec((1, d), lambda i: (0, 0))
    if resid is None:
        kern = lambda x_ref, g_ref, b_ref, o_ref: _ln_kernel(alpha, x_ref, None, g_ref, b_ref, o_ref)
        args, specs = (x, g.reshape(1, d), b.reshape(1, d)), [row, vec, vec]
    else:
        kern = functools.partial(_ln_kernel, alpha)
        args, specs = (x, resid, g.reshape(1, d), b.reshape(1, d)), [row, row, vec, vec]
    return pl.pallas_call(
        kern,
        out_shape=jax.ShapeDtypeStruct((t, d), F32),
        grid=(t // tm,),
        in_specs=specs,
        out_specs=row,
        compiler_params=_cparams(("parallel",)),
        name="layer_norm",
    )(*args)


def _da_kernel(lam_init, tk, lp_ref, g_ref, q_ref, k_ref, v_ref, o_ref):
    tq = q_ref.shape[0]
    nk = k_ref.shape[0] // tk
    q = q_ref[...]
    qs = (q[:, :DA_HD], q[:, DA_HD:])

    def body(i, carry):
        off = pl.multiple_of(i * tk, tk)
        k = k_ref[pl.ds(off, tk), :]
        v = v_ref[pl.ds(off, tk), :]
        out = []
        for c in range(2):
            m_prev, l_prev, acc_prev = carry[c]
            s = lax.dot_general(qs[c], k[:, c * DA_HD:(c + 1) * DA_HD], (((1,), (1,)), ((), ())),
                                preferred_element_type=F32)
            m_new = jnp.maximum(m_prev, jnp.max(s, axis=-1, keepdims=True))
            a = jnp.exp(m_prev - m_new)
            p = jnp.exp(s - m_new)
            l_new = a * l_prev + jnp.sum(p, axis=-1, keepdims=True)
            acc_new = a * acc_prev + jnp.dot(p.astype(BF16), v, preferred_element_type=F32)
            out.append((m_new, l_new, acc_new))
        return tuple(out)

    init = tuple((jnp.full((tq, 1), -jnp.inf, F32), jnp.zeros((tq, 1), F32), jnp.zeros((tq, DA_VD), F32))
                 for _ in range(2))
    (_, l0, a0), (_, l1, a1) = lax.fori_loop(0, nk, body, init)
    lp = lp_ref[...]
    lam = (jnp.exp(jnp.sum(lp[0:1] * lp[1:2], axis=-1, keepdims=True))
           - jnp.exp(jnp.sum(lp[2:3] * lp[3:4], axis=-1, keepdims=True)) + lam_init)
    o = a0 / l0 - lam * (a1 / l1)
    o = o * lax.rsqrt(jnp.mean(o * o, axis=-1, keepdims=True) + 1e-6) * g_ref[...]
    o_ref[...] = o * (1.0 - lam_init)


def _diff_attention(qr, kr, vr, lam_params, norm_g, lam_init, tq=512, tk=1024):
    b, h, l, _ = qr.shape
    nq = l // tq
    return pl.pallas_call(
        functools.partial(_da_kernel, lam_init, tk),
        out_shape=jax.ShapeDtypeStruct((b * l, BRANCH_W), F32),
        grid=(b, h, nq),
        in_specs=[
            pl.BlockSpec((4, DA_HD), lambda bi, hi, qi: (0, 0)),
            pl.BlockSpec((1, DA_VD), lambda bi, hi, qi: (0, 0)),
            pl.BlockSpec((None, None, tq, DA_VD), lambda bi, hi, qi: (bi, hi, qi, 0)),
            pl.BlockSpec((None, None, l, DA_VD), lambda bi, hi, qi: (bi, hi, 0, 0)),
            pl.BlockSpec((None, None, l, DA_VD), lambda bi, hi, qi: (bi, hi, 0, 0)),
        ],
        out_specs=pl.BlockSpec((tq, DA_VD), lambda bi, hi, qi: (bi * nq + qi, hi)),
        compiler_params=_cparams(("parallel", "parallel", "arbitrary")),
        name="diff_attention",
    )(lam_params, norm_g.reshape(1, DA_VD), qr, kr, vr)


def _mem_attn_kernel(q_ref, kv_ref, o_ref):
    q = q_ref[...]
    kv = kv_ref[...]
    scale = MEM_HD ** -0.5
    for h in range(MEM_HEADS):
        qh = q[:, h * MEM_HD:(h + 1) * MEM_HD].astype(BF16)
        kh = kv[:, h * MEM_HD:(h + 1) * MEM_HD].astype(BF16)
        vh = kv[:, BRANCH_W + h * MEM_HD:BRANCH_W + (h + 1) * MEM_HD].astype(BF16)
        s = lax.dot_general(qh, kh, (((1,), (1,)), ((), ())), preferred_element_type=F32) * scale
        e = jnp.exp(s - jnp.max(s, axis=-1, keepdims=True))
        p = e / jnp.sum(e, axis=-1, keepdims=True)
        o_ref[:, h * MEM_HD:(h + 1) * MEM_HD] = jnp.dot(p.astype(BF16), vh, preferred_element_type=F32)


def _memory_attention(z, kv, b, l, tm=512):
    nm = l // tm
    m = kv.shape[1]
    return pl.pallas_call(
        _mem_attn_kernel,
        out_shape=jax.ShapeDtypeStruct((b * l, BRANCH_W), F32),
        grid=(b, nm),
        in_specs=[
            pl.BlockSpec((tm, ME_COLS), lambda bi, i: (bi * nm + i, ME_OFF // ME_COLS)),
            pl.BlockSpec((None, m, 2 * BRANCH_W), lambda bi, i: (bi, 0, 0)),
        ],
        out_specs=pl.BlockSpec((tm, BRANCH_W), lambda bi, i: (bi * nm + i, 0)),
        compiler_params=_cparams(("parallel", "arbitrary")),
        name="memory_attention",
    )(z, kv)


DFT_NA = 128
DFT_NB = 128
HY_LBLK = 4096
HY_KB = 8
HIGHEST = lax.Precision.HIGHEST


def _dft_tables(l):
    assert 2 * l == DFT_NA * DFT_NB
    n = DFT_NA * DFT_NB
    i = jnp.arange(DFT_NA, dtype=jnp.int32)
    ang1 = (2.0 * math.pi / DFT_NA) * ((i[:, None] * i[None, :]) % DFT_NA).astype(F32)
    c1, s1 = jnp.cos(ang1), jnp.sin(ang1)
    half = DFT_NA // 2
    kk = i[:, None, None] + DFT_NA * i[None, :, None]
    ang = (2.0 * math.pi / n) * ((kk * i[None, None, :]) % n).astype(F32)
    gr, gi = jnp.cos(ang), -jnp.sin(ang)
    t = dict(
        f=jnp.concatenate([c1[:, :half], -s1[:, :half]], axis=0),
        gr=gr, gi=gi,
        gg=jnp.concatenate([gr, gi], axis=1),
        hh=jnp.concatenate([gr.transpose(0, 2, 1), gi.transpose(0, 2, 1)], axis=1),
        er=c1.T[:half] / n, ei=s1.T[:half] / n,
    )
    for name in ('f', 'gg', 'hh', 'er', 'ei'):
        t[name + '_bf'] = t[name].astype(BF16)
    return t


def _dft1_kernel(prec, x_ref, f_ref, ar_ref, ai_ref):
    x = x_ref[...]
    f = f_ref[...]
    if prec is None:
        a = jnp.dot(f, x.astype(BF16), preferred_element_type=F32)
    else:
        a = jnp.dot(f, x, precision=prec, preferred_element_type=F32)
    ar_ref[...] = a[:DFT_NA].astype(ar_ref.dtype)
    ai_ref[...] = a[DFT_NA:].astype(ai_ref.dtype)


def _dft_stage1(x, f, out_dtype, prec):
    g, r, w = x.shape
    spec_o = pl.BlockSpec((None, DFT_NA, HY_LBLK), lambda gi, j: (gi, 0, j))
    return pl.pallas_call(
        functools.partial(_dft1_kernel, prec),
        out_shape=[jax.ShapeDtypeStruct((g, DFT_NA, w), out_dtype)] * 2,
        grid=(g, w // HY_LBLK),
        in_specs=[pl.BlockSpec((None, r, HY_LBLK), lambda gi, j: (gi, 0, j)),
                  pl.BlockSpec((2 * DFT_NA, r), lambda gi, j: (0, 0))],
        out_specs=[spec_o, spec_o],
        compiler_params=_cparams(("parallel", "parallel")),
        name="hyena_dft_stage1",
    )(x, f)


def _filter_mlp_kernel(l, feats_ref, w1_ref, b1_ref, w2_ref, b2_ref, w3_ref, b3_ref, fq_ref, dl_ref, h_ref, nrm_ref):
    i = pl.program_id(0)
    tm = feats_ref.shape[0]
    dot = functools.partial(jnp.dot, precision=HIGHEST, preferred_element_type=F32)
    h = jnp.sin(fq_ref[0:1] * (dot(feats_ref[...], w1_ref[...]) + b1_ref[...]))
    h = jnp.sin(fq_ref[1:2] * (dot(h, w2_ref[...]) + b2_ref[...]))
    h = dot(h, w3_ref[...]) + b3_ref[...]
    row = i * tm + lax.broadcasted_iota(jnp.int32, (tm, 1), 0)
    t = row.astype(F32) * (1.0 / (l - 1))
    decay = jnp.exp(-t * dl_ref[...])
    h = h * jnp.concatenate([decay] * (2 * HY_ORDER), axis=1)
    col = lax.broadcasted_iota(jnp.int32, (1, 2 * HY_ORDER * HY_W), 1)
    bwd = jnp.bitwise_and(lax.shift_right_logical(col, HY_W.bit_length() - 1), 1) == 1
    h = jnp.where(jnp.logical_and(row == 0, bwd), 0.0, h)
    h_ref[...] = h

    @pl.when(i == 0)
    def _():
        nrm_ref[...] = jnp.zeros_like(nrm_ref)

    nrm_ref[...] += jnp.sum(jnp.abs(h), axis=0, keepdims=True)


def _filter_mlp(feats, w1, b1, w2, b2, w3, b3, freq, deltas, tm=1024):
    l = feats.shape[0]
    wcols = 2 * HY_ORDER * HY_W
    full = lambda a: pl.BlockSpec(a.shape, lambda i: (0,) * a.ndim)
    args = (feats, w1, b1.reshape(1, -1), w2, b2.reshape(1, -1), w3, b3.reshape(1, -1), freq, deltas.reshape(1, -1))
    return pl.pallas_call(
        functools.partial(_filter_mlp_kernel, l),
        out_shape=[jax.ShapeDtypeStruct((l, wcols), F32), jax.ShapeDtypeStruct((1, wcols), F32)],
        grid=(l // tm,),
        in_specs=[pl.BlockSpec((tm, feats.shape[1]), lambda i: (i, 0))] + [full(a) for a in args[1:]],
        out_specs=[pl.BlockSpec((tm, wcols), lambda i: (i, 0)), pl.BlockSpec((1, wcols), lambda i: (0, 0))],
        compiler_params=_cparams(("arbitrary",)),
        name="hyena_filter_mlp",
    )(*args)


def _filter_spec_kernel(afr_ref, afi_ref, abr_ref, abi_ref, gr_ref, gi_ref, nf_ref, nb_ref, kr_ref, ki_ref):
    dot = functools.partial(jnp.dot, precision=HIGHEST, preferred_element_type=F32)
    inv = 1.0 / (nf_ref[...] + nb_ref[...])
    for j in range(gr_ref.shape[0]):
        gr, gi = gr_ref[j], gi_ref[j]
        sr = afr_ref[j] + abr_ref[j]
        si = afi_ref[j] + abi_ref[j]
        dr = afr_ref[j] - abr_ref[j]
        di = afi_ref[j] - abi_ref[j]
        kr_ref[j] = (dot(gr, sr) - dot(gi, si)) * inv
        ki_ref[j] = (dot(gr, di) + dot(gi, dr)) * inv


def _filter_spectrum(ar, ai, gr, gi, nrm, kb=4, cw=256):
    ncw = HY_W // cw

    def a_spec(d):
        return pl.BlockSpec((kb, DFT_NB, cw), lambda o, ki, ci: (ki, 0, (2 * o + d) * ncw + ci))

    def n_spec(d):
        return pl.BlockSpec((1, cw), lambda o, ki, ci: (0, (2 * o + d) * ncw + ci))

    g_spec = pl.BlockSpec((kb, DFT_NB, DFT_NB), lambda o, ki, ci: (ki, 0, 0))
    o_spec = pl.BlockSpec((None, kb, DFT_NB, cw), lambda o, ki, ci: (o, ki, 0, ci))
    return pl.pallas_call(
        _filter_spec_kernel,
        out_shape=[jax.ShapeDtypeStruct((HY_ORDER, DFT_NA, DFT_NB, HY_W), F32)] * 2,
        grid=(HY_ORDER, DFT_NA // kb, ncw),
        in_specs=[a_spec(0), a_spec(0), a_spec(1), a_spec(1), g_spec, g_spec, n_spec(0), n_spec(1)],
        out_specs=[o_spec, o_spec],
        compiler_params=_cparams(("parallel", "parallel", "parallel")),
        name="hyena_filter_spectrum",
    )(ar, ai, ar, ai, gr, gi, nrm, nrm)


def _hy_mid_kernel(ar_ref, ai_ref, gg_ref, hh_ref, kr_ref, ki_ref, br_ref, bi_ref):
    for j in range(HY_KB):
        gg = gg_ref[j]
        p = jnp.dot(gg, ar_ref[j], preferred_element_type=F32)
        q = jnp.dot(gg, ai_ref[j], preferred_element_type=F32)
        xr = p[:DFT_NB] - q[DFT_NB:]
        xi = q[:DFT_NB] + p[DFT_NB:]
        kr, ki = kr_ref[j], ki_ref[j]
        yr = (xr * kr - xi * ki).astype(BF16)
        yi = (xr * ki + xi * kr).astype(BF16)
        hh = hh_ref[j]
        p = jnp.dot(hh, yr, preferred_element_type=F32)
        q = jnp.dot(hh, yi, preferred_element_type=F32)
        br_ref[j] = (p[:DFT_NB] + q[DFT_NB:]).astype(BF16)
        bi_ref[j] = (q[:DFT_NB] - p[DFT_NB:]).astype(BF16)


def _hy_mid(ar, ai, gg, hh, kr, ki):
    b = ar.shape[0]
    a_spec = pl.BlockSpec((None, HY_KB, DFT_NB, HY_W), lambda ki_, bi: (bi, ki_, 0, 0))
    t_spec = pl.BlockSpec((HY_KB, 2 * DFT_NB, DFT_NB), lambda ki_, bi: (ki_, 0, 0))
    k_spec = pl.BlockSpec((HY_KB, DFT_NB, HY_W), lambda ki_, bi: (ki_, 0, 0))
    return pl.pallas_call(
        _hy_mid_kernel,
        out_shape=[jax.ShapeDtypeStruct(ar.shape, BF16)] * 2,
        grid=(DFT_NA // HY_KB, b),
        in_specs=[a_spec, a_spec, t_spec, t_spec, k_spec, k_spec],
        out_specs=[a_spec, a_spec],
        compiler_params=_cparams(("parallel", "arbitrary")),
        name="hyena_dft_mid",
    )(ar, ai, gg, hh, kr, ki)


def _hy_out_kernel(br_ref, bi_ref, er_ref, ei_ref, u_ref, gate_ref, skip_ref, o_ref):
    y = (jnp.dot(er_ref[...], br_ref[...], preferred_element_type=F32)
         - jnp.dot(ei_ref[...], bi_ref[...], preferred_element_type=F32))
    o_ref[...] = gate_ref[...] * (y + u_ref[...] * skip_ref[...])


def _hy_out(br, bi, er, ei, u, gate, skip_row):
    b, r, w = u.shape
    row = pl.BlockSpec((None, r, HY_LBLK), lambda bi_, j: (bi_, 0, j))
    bspec = pl.BlockSpec((None, DFT_NA, HY_LBLK), lambda bi_, j: (bi_, 0, j))
    espec = pl.BlockSpec((r, DFT_NA), lambda bi_, j: (0, 0))
    return pl.pallas_call(
        _hy_out_kernel,
        out_shape=jax.ShapeDtypeStruct(u.shape, F32),
        grid=(b, w // HY_LBLK),
        in_specs=[bspec, bspec, espec, espec, row, row, pl.BlockSpec((1, HY_LBLK), lambda bi_, j: (0, 0))],
        out_specs=row,
        compiler_params=_cparams(("parallel", "parallel")),
        name="hyena_dft_out",
    )(br, bi, er, ei, u, gate, skip_row)


def _short_conv_kernel(z_ref, w_ref, b_ref, o_ref):
    z = z_ref[...]
    l = z.shape[0]
    row = lax.broadcasted_iota(jnp.int32, (l, 1), 0)
    prev = jnp.where(row == 0, 0.0, pltpu.roll(z, 1, 0))
    nxt = jnp.where(row == l - 1, 0.0, pltpu.roll(z, l - 1, 0))
    w = w_ref[...]
    o_ref[...] = prev * w[0:1] + z * w[1:2] + nxt * w[2:3] + b_ref[...]


def _short_conv(z, w, bias, b, l):
    nc = HY_W // LANE
    return pl.pallas_call(
        _short_conv_kernel,
        out_shape=jax.ShapeDtypeStruct((3, b * l, HY_W), F32),
        grid=(b, 3, nc),
        in_specs=[pl.BlockSpec((l, LANE), lambda bi, p, ci: (bi, p * nc + ci)),
                  pl.BlockSpec((3, LANE), lambda bi, p, ci: (0, p * nc + ci)),
                  pl.BlockSpec((1, LANE), lambda bi, p, ci: (0, p * nc + ci))],
        out_specs=pl.BlockSpec((None, l, LANE), lambda bi, p, ci: (p, bi, ci)),
        compiler_params=_cparams(("parallel", "parallel", "parallel")),
        name="hyena_short_conv",
    )(z, w, bias.reshape(1, -1))


def _hyena_filter_spectrum(l, w1, b1, w2, b2, w3, b3, freq, tabs):
    t = jnp.linspace(0.0, 1.0, l, dtype=F32)[:, None]
    w = 2.0 * math.pi * jnp.arange(l, dtype=F32)[:, None] / l
    fr = jnp.linspace(1e-4, HY_BANDS - 1, HY_BANDS, dtype=F32)[None, :]
    feats = jnp.concatenate([t, jnp.cos(fr * w), -jnp.sin(fr * w)], axis=-1)
    deltas = jnp.abs(jnp.linspace(HY_MIN_DECAY, HY_MAX_DECAY, HY_W, dtype=F32))
    h, nrm = _filter_mlp(feats, w1, b1, w2, b2, w3, b3, freq, deltas)
    wcols = h.shape[1]
    ar, ai = _dft_stage1(h.reshape(1, l // DFT_NB, DFT_NB * wcols), tabs['f'], F32, HIGHEST)
    return _filter_spectrum(ar.reshape(DFT_NA, DFT_NB, wcols), ai.reshape(DFT_NA, DFT_NB, wcols),
                            tabs['gr'], tabs['gi'], nrm)


def _hyena(z, short_w, short_b, skip, kr, ki, tabs, b, l):
    u = _short_conv(z, short_w, short_b, b, l)
    rows = l // DFT_NB
    view = lambda a: a.reshape(b, rows, DFT_NB * HY_W)
    v, gates = view(u[0]), (view(u[1]), view(u[2]))
    for o in range(HY_ORDER):
        ar, ai = _dft_stage1(v, tabs['f_bf'], BF16, None)
        shp4 = (b, DFT_NA, DFT_NB, HY_W)
        br, bi = _hy_mid(ar.reshape(shp4), ai.reshape(shp4), tabs['gg_bf'], tabs['hh_bf'], kr[o], ki[o])
        shp3 = (b, DFT_NA, DFT_NB * HY_W)
        skip_row = jnp.tile(skip[o], HY_LBLK // HY_W).reshape(1, HY_LBLK)
        v = _hy_out(br.reshape(shp3), bi.reshape(shp3), tabs['er_bf'], tabs['ei_bf'], v, gates[o], skip_row)
    return v.reshape(b * l, HY_W)


HG_REC_CHUNK = 64
HG_SUB = 16
HG_TILE = 512


def _gate_terms(z, lb):
    a = jnp.log(lb)
    b = jnp.log1p(-lb) + (jnp.minimum(z, 0.0) - jnp.log1p(jnp.exp(-jnp.abs(z))))
    log_f = jnp.maximum(a, b) + jnp.log1p(jnp.exp(-jnp.abs(a - b)))
    return log_f, (1.0 - lb) * jax.nn.sigmoid(-z)


def _gla_chunk(q, k, v, g, st, fwd):
    ch = q.shape[0]
    ri = lax.broadcasted_iota(jnp.int32, (ch, ch), 0)
    ci = lax.broadcasted_iota(jnp.int32, (ch, ch), 1)
    tri = jnp.where((ci <= ri) if fwd else (ci >= ri), 1.0, 0.0).astype(F32)
    b = jnp.dot(tri, g, precision=lax.Precision.HIGHEST, preferred_element_type=F32)
    nsb = ch // HG_SUB
    parts = [None] * nsb

    def add(i, val):
        parts[i] = val if parts[i] is None else parts[i] + val

    vb = v.astype(BF16)
    li = lax.broadcasted_iota(jnp.int32, (HG_SUB, HG_SUB), 1)
    si = lax.broadcasted_iota(jnp.int32, (HG_SUB, HG_SUB), 0)
    keep = (li <= si) if fwd else (li >= si)
    for i in range(nsb):
        sl = slice(i * HG_SUB, (i + 1) * HG_SUB)
        qb, kb, bb = q[sl], k[sl], b[sl]
        a = jnp.zeros((HG_SUB, HG_SUB), F32)
        for s in range(HG_SUB):
            e = jnp.exp(jnp.minimum(bb - bb[s:s + 1], 0.0))
            col = jnp.sum(qb * e * kb[s:s + 1], axis=-1, keepdims=True)
            a = jnp.where(li == s, col, a)
        a = jnp.where(keep, a, 0.0)
        add(i, jnp.dot(a.astype(BF16), vb[sl], preferred_element_type=F32))
    h = HG_SUB
    while h < ch:
        for j in range(ch // (2 * h)):
            lo = slice(2 * h * j, 2 * h * j + h)
            hi = slice(2 * h * j + h, 2 * h * j + 2 * h)
            if fwd:
                rows, cols, ref = hi, lo, b[2 * h * j + h - 1:2 * h * j + h]
            else:
                rows, cols, ref = lo, hi, b[2 * h * j + h:2 * h * j + h + 1]
            qt = (q[rows] * jnp.exp(b[rows] - ref)).astype(BF16)
            kt = (k[cols] * jnp.exp(ref - b[cols])).astype(BF16)
            a = lax.dot_general(qt, kt, (((1,), (1,)), ((), ())), preferred_element_type=F32)
            ov = jnp.dot(a.astype(BF16), vb[cols], preferred_element_type=F32)
            for ii in range(h // HG_SUB):
                add(rows.start // HG_SUB + ii, ov[ii * HG_SUB:(ii + 1) * HG_SUB])
        h *= 2
    o = jnp.concatenate(parts, axis=0)
    qh = (q * jnp.exp(b)).astype(BF16)
    o = o + lax.dot_general(qh, st.astype(BF16), (((1,), (1,)), ((), ())), preferred_element_type=F32)
    btot = b[ch - 1:ch] if fwd else b[0:1]
    kh = (k * jnp.exp(btot - b)).astype(BF16)
    st_new = st * jnp.exp(btot) + lax.dot_general(vb, kh, (((0,), (0,)), ((), ())), preferred_element_type=F32)
    return o, st_new


def _hgrn2_kernel(lbf_ref, lbb_ref, ng_ref, zqf_ref, zif_ref, zff_ref, zgf_ref, zqb_ref, zib_ref, zbb_ref, zgb_ref,
                  o_ref, st_ref):
    c = pl.program_id(2)
    n = pl.num_programs(2)
    nch = HG_TILE // HG_REC_CHUNK

    @pl.when(c == 0)
    def _():
        st_ref[...] = jnp.zeros_like(st_ref)

    def body(second_pass, i, carry):
        off_f = pl.multiple_of(i * HG_REC_CHUNK, HG_REC_CHUNK)
        off_b = pl.multiple_of((nch - 1 - i) * HG_REC_CHUNK, HG_REC_CHUNK)
        for fwd, off, zq_ref, zi_ref, zd_ref, zg_ref, lb_ref, tile in (
                (True, off_f, zqf_ref, zif_ref, zff_ref, zgf_ref, lbf_ref, c),
                (False, off_b, zqb_ref, zib_ref, zbb_ref, zgb_ref, lbb_ref, n - 1 - c)):
            sl = pl.ds(off, HG_REC_CHUNK)
            g, k = _gate_terms(zd_ref[sl, :], lb_ref[...])
            d = 0 if fwd else 1
            o, st_new = _gla_chunk(jax.nn.silu(zq_ref[sl, :]), k, zi_ref[sl, :], g, st_ref[d], fwd)
            st_ref[d] = st_new
            rows = pl.ds(pl.multiple_of(tile * HG_TILE + off, HG_REC_CHUNK), HG_REC_CHUNK)
            if second_pass:
                tot = o_ref[rows, :] + o
                tot = tot * lax.rsqrt(jnp.mean(tot * tot, axis=-1, keepdims=True) + 1e-6) * ng_ref[...]
                o_ref[rows, :] = tot * jax.nn.silu(zg_ref[sl, :])
            else:
                o_ref[rows, :] = o
        return carry

    @pl.when(c < n // 2)
    def _():
        lax.fori_loop(0, nch, functools.partial(body, False), 0)

    @pl.when(c >= n // 2)
    def _():
        lax.fori_loop(0, nch, functools.partial(body, True), 0)


def _hgrn2(z, lb_fwd, lb_bwd, norm_g, b, l):
    n = l // HG_TILE
    assert n % 2 == 0
    base = HY_COLS // LANE
    nh = BRANCH_W // LANE

    def zspec(part, rev):
        if rev:
            return pl.BlockSpec((HG_TILE, LANE), lambda bi, hi, ci: (bi * n + n - 1 - ci, base + part * nh + hi))
        return pl.BlockSpec((HG_TILE, LANE), lambda bi, hi, ci: (bi * n + ci, base + part * nh + hi))

    lbspec = pl.BlockSpec((1, LANE), lambda bi, hi, ci: (0, hi))
    return pl.pallas_call(
        _hgrn2_kernel,
        out_shape=jax.ShapeDtypeStruct((b * l, BRANCH_W), F32),
        grid=(b, HG_HEADS, n),
        in_specs=[lbspec, lbspec, pl.BlockSpec((1, LANE), lambda bi, hi, ci: (0, 0)),
                  zspec(0, False), zspec(1, False), zspec(2, False), zspec(4, False),
                  zspec(0, True), zspec(1, True), zspec(3, True), zspec(4, True)],
        out_specs=pl.BlockSpec((l, LANE), lambda bi, hi, ci: (bi, hi)),
        scratch_shapes=[pltpu.VMEM((2, HG_DV, HG_DK), F32)],
        compiler_params=_cparams(("parallel", "parallel", "arbitrary")),
        name="hgrn2_scan",
    )(lb_fwd.reshape(1, BRANCH_W), lb_bwd.reshape(1, BRANCH_W), norm_g.reshape(1, HG_DV),
      z, z, z, z, z, z, z, z)


def _merge_kernel(x_ref, o0_ref, o1_ref, o2_ref, o3_ref, wg_ref, bg_ref, wbr_ref, wout_ref, g_ref, b_ref,
                  out_ref, acc_ref, xb_ref, ob_ref):
    j = pl.program_id(1)

    @pl.when(j == 0)
    def _():
        xb_ref[...] = x_ref[...].astype(BF16)
        for bi, o_ref in enumerate((o0_ref, o1_ref, o2_ref, o3_ref)):
            ob_ref[bi] = o_ref[...].astype(BF16)
        acc_ref[...] = jnp.zeros_like(acc_ref)

    xb = xb_ref[...]
    merged = None
    for bi in range(N_BRANCH):
        gate = jax.nn.sigmoid(jnp.dot(xb, wg_ref[bi], preferred_element_type=F32) + bg_ref[bi])
        term = gate * jnp.dot(ob_ref[bi], wbr_ref[bi], preferred_element_type=F32)
        merged = term if merged is None else merged + term
    acc_ref[...] += jnp.dot(merged.astype(BF16), wout_ref[...], preferred_element_type=F32)

    @pl.when(j == pl.num_programs(1) - 1)
    def _():
        out_ref[...] = _ln_rows(DN_ALPHA * x_ref[...] + acc_ref[...], g_ref[...], b_ref[...])


def _merge(x, branches, w_gate, b_gate, w_br, w_out, ln_g, ln_b, tm=512, tn=256):
    t, d = x.shape
    row = pl.BlockSpec((tm, d), lambda i, j: (i, 0))
    brow = pl.BlockSpec((tm, BRANCH_W), lambda i, j: (i, 0))
    vec = pl.BlockSpec((1, d), lambda i, j: (0, 0))
    return pl.pallas_call(
        _merge_kernel,
        out_shape=jax.ShapeDtypeStruct((t, d), F32),
        grid=(t // tm, d // tn),
        in_specs=[
            row, brow, brow, brow, brow,
            pl.BlockSpec((N_BRANCH, d, tn), lambda i, j: (0, 0, j)),
            pl.BlockSpec((N_BRANCH, 1, tn), lambda i, j: (0, 0, j)),
            pl.BlockSpec((N_BRANCH, BRANCH_W, tn), lambda i, j: (0, 0, j)),
            pl.BlockSpec((tn, d), lambda i, j: (j, 0)),
            vec, vec,
        ],
        out_specs=row,
        scratch_shapes=[pltpu.VMEM((tm, d), F32), pltpu.VMEM((tm, d), BF16),
                        pltpu.VMEM((N_BRANCH, tm, BRANCH_W), BF16)],
        compiler_params=_cparams(("parallel", "arbitrary")),
        name="gated_merge",
    )(x, *branches, w_gate, b_gate.reshape(N_BRANCH, 1, d), w_br, w_out, ln_g.reshape(1, d), ln_b.reshape(1, d))


def _expert_kernel(x_ref, gate_ref, w1_ref, w3_ref, w2_ref, o_ref):
    f = pl.program_id(2)
    x = x_ref[...]
    h1 = jnp.dot(x, w1_ref[...], preferred_element_type=F32)
    h3 = jnp.dot(x, w3_ref[...], preferred_element_type=F32)
    h = (jax.nn.silu(h1) * h3).astype(BF16)
    part = jnp.dot(h, w2_ref[...], preferred_element_type=F32)

    @pl.when(f == 0)
    def _():
        o_ref[...] = part

    @pl.when(f != 0)
    def _():
        o_ref[...] += part

    @pl.when(f == pl.num_programs(2) - 1)
    def _():
        o_ref[...] = o_ref[...] * gate_ref[...]


def _expert_ffn(xe, gate, w1, w3, w2, tm=1024, tf=512):
    e, c, d = xe.shape
    ff = w1.shape[2]
    tm = min(tm, c)
    return pl.pallas_call(
        _expert_kernel,
        out_shape=jax.ShapeDtypeStruct((e, c, d), F32),
        grid=(e, c // tm, ff // tf),
        in_specs=[
            pl.BlockSpec((None, tm, d), lambda ei, i, f: (ei, i, 0)),
            pl.BlockSpec((None, tm, 1), lambda ei, i, f: (ei, i, 0)),
            pl.BlockSpec((None, d, tf), lambda ei, i, f: (ei, 0, f)),
            pl.BlockSpec((None, d, tf), lambda ei, i, f: (ei, 0, f)),
            pl.BlockSpec((None, tf, d), lambda ei, i, f: (ei, f, 0)),
        ],
        out_specs=pl.BlockSpec((None, tm, d), lambda ei, i, f: (ei, i, 0)),
        compiler_params=_cparams(("parallel", "parallel", "arbitrary")),
        name="expert_ffn",
    )(xe, gate, w1, w3, w2)


def _rope_tables(L):
    inv = ROPE_THETA ** (-jnp.arange(0, ROPE_DIM, 2, dtype=F32) / ROPE_DIM)
    ang = jnp.arange(L, dtype=F32)[:, None] * inv[None, :]
    return jnp.cos(ang), jnp.sin(ang)


def _partial_rope(x, cos, sin):
    c = cos[None, :, None, None, :]
    s = sin[None, :, None, None, :]
    half = ROPE_DIM // 2
    x1, x2, xp = x[..., :half], x[..., half:ROPE_DIM], x[..., ROPE_DIM:]
    return jnp.concatenate([x1 * c - x2 * s, x2 * c + x1 * s, xp], axis=-1)


def _trunk(x, mem, p):
    B, L, D = x.shape
    T = B * L
    cos, sin = _rope_tables(L)
    lb_all = jnp.cumsum(jax.nn.softmax(p['hg_lb_raw'], axis=1), axis=1)
    lb_all = lb_all - lb_all[:, :1]
    xf = _layer_norm(x.reshape(T, D), p['ln_in_g'], p['ln_in_b'])
    memf = mem.reshape(B * mem.shape[1], D)
    cap = EC_FACTOR * T // N_EXPERTS
    for l in range(DEPTH):
        z = _matmul(xf, p['w_in'][l], 1024, 1024)
        z3 = z.reshape(B, L, IN_W)
        kr, ki = p['hy_spectrum'][L][l]
        o_hy = _hyena(z, p['hy_short_w'][l], p['hy_short_b'][l], p['hy_skip'][l], kr, ki, p['dft_tables'][L], B, L)
        o_hg = _hgrn2(z, lb_all[0, l], lb_all[1, l], p['hg_norm_g'][l], B, L)
        zq, zk, zv = jnp.split(z3[..., DA_OFF:ME_OFF], 3, axis=-1)
        q = _partial_rope(zq.reshape(B, L, DA_HEADS, 2, DA_HD), cos, sin) * (DA_HD ** -0.5)
        k = _partial_rope(zk.reshape(B, L, DA_HEADS, 2, DA_HD), cos, sin)
        qr = q.reshape(B, L, DA_HEADS, DA_VD).transpose(0, 2, 1, 3).astype(BF16)
        kr = k.reshape(B, L, DA_HEADS, DA_VD).transpose(0, 2, 1, 3).astype(BF16)
        vr = zv.reshape(B, L, DA_HEADS, DA_VD).transpose(0, 2, 1, 3).astype(BF16)
        lam_init = 0.8 - 0.6 * math.exp(-0.3 * l)
        o_da = _diff_attention(qr, kr, vr, p['da_lambda'][l], p['da_norm_g'][l], lam_init)
        kv = _matmul(memf, p['w_mem_kv'][l], min(512, memf.shape[0]), 512).reshape(B, -1, 2 * BRANCH_W)
        o_me = _memory_attention(z, kv, B, L)
        xf = _merge(xf, (o_hy, o_hg, o_da, o_me), p['w_gate'][l], p['b_gate'][l], p['w_br'][l], p['w_out'][l],
                    p['ln1_g'][l], p['ln1_b'][l])
        logits = _matmul(xf, p['w_router_pad'][l], 1024, LANE)[:, :N_EXPERTS]
        aff = jax.nn.softmax(logits, axis=-1)
        gate, idx = lax.top_k(aff.T, cap)
        xe = xf.astype(BF16)[idx]
        ye = _expert_ffn(xe, gate[..., None], p['w_e1'][l], p['w_e3'][l], p['w_e2'][l])
        y = jnp.zeros_like(xf).at[idx.reshape(-1)].add(ye.reshape(-1, D))
        xf = _layer_norm(xf, p['ln2_g'][l], p['ln2_b'][l], resid=y, alpha=DN_ALPHA)
    return xf.reshape(B, L, D)


def kernel(x_prompt, x_sample, mem_prompt, mem_sample, ln_in_g, ln_in_b, w_in, hy_short_w, hy_short_b, hy_ffn_w1, hy_ffn_b1, hy_ffn_w2, hy_ffn_b2, hy_ffn_w3, hy_ffn_b3, hy_freq, hy_skip, hg_lb_raw, hg_norm_g, da_lambda, da_norm_g, w_mem_kv, w_gate, b_gate, w_br, w_out, ln1_g, ln1_b, w_router, w_e1, w_e3, w_e2, ln2_g, ln2_b):
    p = dict(ln_in_g=ln_in_g, ln_in_b=ln_in_b, hy_short_w=hy_short_w, hy_short_b=hy_short_b,
             hy_ffn_w1=hy_ffn_w1, hy_ffn_b1=hy_ffn_b1, hy_ffn_w2=hy_ffn_w2, hy_ffn_b2=hy_ffn_b2,
             hy_ffn_w3=hy_ffn_w3, hy_ffn_b3=hy_ffn_b3, hy_freq=hy_freq, hy_skip=hy_skip, hg_lb_raw=hg_lb_raw,
             hg_norm_g=hg_norm_g, da_lambda=da_lambda, da_norm_g=da_norm_g, b_gate=b_gate,
             ln1_g=ln1_g, ln1_b=ln1_b, ln2_g=ln2_g, ln2_b=ln2_b)
    for name, w in (('w_in', w_in), ('w_mem_kv', w_mem_kv), ('w_gate', w_gate), ('w_br', w_br), ('w_out', w_out),
                    ('w_e1', w_e1), ('w_e3', w_e3), ('w_e2', w_e2)):
        p[name] = w.astype(BF16)
    p['w_router_pad'] = jnp.pad(w_router, ((0, 0), (0, 0), (0, LANE - N_EXPERTS))).astype(BF16)
    p['dft_tables'], p['hy_spectrum'] = {}, {}
    for seq_len in sorted({x_prompt.shape[1], x_sample.shape[1]}):
        tabs = _dft_tables(seq_len)
        p['dft_tables'][seq_len] = tabs
        p['hy_spectrum'][seq_len] = [
            _hyena_filter_spectrum(seq_len, hy_ffn_w1[l], hy_ffn_b1[l], hy_ffn_w2[l], hy_ffn_b2[l], hy_ffn_w3[l],
                                   hy_ffn_b3[l], hy_freq[l], tabs) for l in range(DEPTH)]
    y_prompt = _trunk(x_prompt, mem_prompt, p)
    y_sample = _trunk(x_sample, mem_sample, p)
    return (y_prompt, y_sample)
```

```python
import functools
import math

import jax
import jax.numpy as jnp
from jax import lax
from jax.experimental import pallas as pl
from jax.experimental.pallas import tpu as pltpu

F32 = jnp.float32
BF16 = jnp.bfloat16

D_MODEL = 2048
DEPTH = 4
N_BRANCH = 4
BRANCH_W = D_MODEL // 4
HY_W = BRANCH_W
HY_ORDER = 2
HY_EMB = 33
HY_BANDS = (HY_EMB - 1) // 2
HY_MIN_DECAY = math.log(1e-2) / 1.5
HY_MAX_DECAY = math.log(1e-2) / 0.3
HG_HEADS = 4
HG_DK = BRANCH_W // HG_HEADS
HG_DV = BRANCH_W // HG_HEADS
HG_CHUNK = 64
DA_HEADS = 4
DA_HD = BRANCH_W // (2 * DA_HEADS)
DA_VD = 2 * DA_HD
ROPE_DIM = DA_HD // 4
ROPE_THETA = 500000.0
MEM_HEADS = 4
MEM_HD = BRANCH_W // MEM_HEADS
N_EXPERTS = 16
EC_FACTOR = 2
EXPERT_FF = 2048
DN_ALPHA = (2 * DEPTH) ** 0.25
LN_EPS = 1e-5
HY_COLS = 3 * HY_W
HG_COLS = 5 * BRANCH_W
DA_COLS = 3 * BRANCH_W
ME_COLS = BRANCH_W
IN_W = HY_COLS + HG_COLS + DA_COLS + ME_COLS
DA_OFF = HY_COLS + HG_COLS
ME_OFF = DA_OFF + DA_COLS

V7X_VMEM_LIMIT_BYTES = 56 * 1024 * 1024
LANE = 128


def _cparams(sem):
    return pltpu.CompilerParams(dimension_semantics=sem, vmem_limit_bytes=V7X_VMEM_LIMIT_BYTES)


def _mm_kernel(a_ref, w_ref, o_ref):
    o_ref[...] = jnp.dot(a_ref[...].astype(BF16), w_ref[...], preferred_element_type=F32)


def _matmul(a, w, tm, tn):
    m, k = a.shape
    n = w.shape[1]
    return pl.pallas_call(
        _mm_kernel,
        out_shape=jax.ShapeDtypeStruct((m, n), F32),
        grid=(m // tm, n // tn),
        in_specs=[pl.BlockSpec((tm, k), lambda i, j: (i, 0)), pl.BlockSpec((k, tn), lambda i, j: (0, j))],
        out_specs=pl.BlockSpec((tm, tn), lambda i, j: (i, j)),
        compiler_params=_cparams(("parallel", "arbitrary")),
        name="dense_matmul",
    )(a, w)


def _ln_rows(y, g, b):
    mu = jnp.mean(y, axis=-1, keepdims=True)
    d = y - mu
    var = jnp.mean(d * d, axis=-1, keepdims=True)
    return d * lax.rsqrt(var + LN_EPS) * g + b


def _ln_kernel(alpha, x_ref, r_ref, g_ref, b_ref, o_ref):
    y = x_ref[...]
    if r_ref is not None:
        y = alpha * y + r_ref[...]
    o_ref[...] = _ln_rows(y, g_ref[...], b_ref[...])


def _layer_norm(x, g, b, resid=None, alpha=1.0, tm=512):
    t, d = x.shape
    row = pl.BlockSpec((tm, d), lambda i: (i, 0))
    vec = pl.BlockSpec((1, d), lambda i: (0, 0))
    if resid is None:
        kern = lambda x_ref, g_ref, b_ref, o_ref: _ln_kernel(alpha, x_ref, None, g_ref, b_ref, o_ref)
        args, specs = (x, g.reshape(1, d), b.reshape(1, d)), [row, vec, vec]
    else:
        kern = functools.partial(_ln_kernel, alpha)
        args, specs = (x, resid, g.reshape(1, d), b.reshape(1, d)), [row, row, vec, vec]
    return pl.pallas_call(
        kern,
        out_shape=jax.ShapeDtypeStruct((t, d), F32),
        grid=(t // tm,),
        in_specs=specs,
        out_specs=row,
        compiler_params=_cparams(("parallel",)),
        name="layer_norm",
    )(*args)


def _da_kernel(lam_init, tk, lp_ref, g_ref, q_ref, k_ref, v_ref, o_ref):
    tq = q_ref.shape[0]
    nk = k_ref.shape[0] // tk
    q = q_ref[...]
    qs = (q[:, :DA_HD], q[:, DA_HD:])

    def body(i, carry):
        off = pl.multiple_of(i * tk, tk)
        k = k_ref[pl.ds(off, tk), :]
        v = v_ref[pl.ds(off, tk), :]
        out = []
        for c in range(2):
            m_prev, l_prev, acc_prev = carry[c]
            s = lax.dot_general(qs[c], k[:, c * DA_HD:(c + 1) * DA_HD], (((1,), (1,)), ((), ())),
                                preferred_element_type=F32)
            m_new = jnp.maximum(m_prev, jnp.max(s, axis=-1, keepdims=True))
            a = jnp.exp(m_prev - m_new)
            p = jnp.exp(s - m_new)
            l_new = a * l_prev + jnp.sum(p, axis=-1, keepdims=True)
            acc_new = a * acc_prev + jnp.dot(p.astype(BF16), v, preferred_element_type=F32)
            out.append((m_new, l_new, acc_new))
        return tuple(out)

    init = tuple((jnp.full((tq, 1), -jnp.inf, F32), jnp.zeros((tq, 1), F32), jnp.zeros((tq, DA_VD), F32))
                 for _ in range(2))
    (_, l0, a0), (_, l1, a1) = lax.fori_loop(0, nk, body, init)
    lp = lp_ref[...]
    lam = (jnp.exp(jnp.sum(lp[0:1] * lp[1:2], axis=-1, keepdims=True))
           - jnp.exp(jnp.sum(lp[2:3] * lp[3:4], axis=-1, keepdims=True)) + lam_init)
    o = a0 / l0 - lam * (a1 / l1)
    o = o * lax.rsqrt(jnp.mean(o * o, axis=-1, keepdims=True) + 1e-6) * g_ref[...]
    o_ref[...] = o * (1.0 - lam_init)


def _diff_attention(qr, kr, vr, lam_params, norm_g, lam_init, tq=512, tk=1024):
    b, h, l, _ = qr.shape
    nq = l // tq
    return pl.pallas_call(
        functools.partial(_da_kernel, lam_init, tk),
        out_shape=jax.ShapeDtypeStruct((b * l, BRANCH_W), F32),
        grid=(b, h, nq),
        in_specs=[
            pl.BlockSpec((4, DA_HD), lambda bi, hi, qi: (0, 0)),
            pl.BlockSpec((1, DA_VD), lambda bi, hi, qi: (0, 0)),
            pl.BlockSpec((None, None, tq, DA_VD), lambda bi, hi, qi: (bi, hi, qi, 0)),
            pl.BlockSpec((None, None, l, DA_VD), lambda bi, hi, qi: (bi, hi, 0, 0)),
            pl.BlockSpec((None, None, l, DA_VD), lambda bi, hi, qi: (bi, hi, 0, 0)),
        ],
        out_specs=pl.BlockSpec((tq, DA_VD), lambda bi, hi, qi: (bi * nq + qi, hi)),
        compiler_params=_cparams(("parallel", "parallel", "arbitrary")),
        name="diff_attention",
    )(lam_params, norm_g.reshape(1, DA_VD), qr, kr, vr)


def _mem_attn_kernel(q_ref, kv_ref, o_ref):
    q = q_ref[...]
    kv = kv_ref[...]
    scale = MEM_HD ** -0.5
    for h in range(MEM_HEADS):
        qh = q[:, h * MEM_HD:(h + 1) * MEM_HD].astype(BF16)
        kh = kv[:, h * MEM_HD:(h + 1) * MEM_HD].astype(BF16)
        vh = kv[:, BRANCH_W + h * MEM_HD:BRANCH_W + (h + 1) * MEM_HD].astype(BF16)
        s = lax.dot_general(qh, kh, (((1,), (1,)), ((), ())), preferred_element_type=F32) * scale
        e = jnp.exp(s - jnp.max(s, axis=-1, keepdims=True))
        p = e / jnp.sum(e, axis=-1, keepdims=True)
        o_ref[:, h * MEM_HD:(h + 1) * MEM_HD] = jnp.dot(p.astype(BF16), vh, preferred_element_type=F32)


def _memory_attention(z, kv, b, l, tm=512):
    nm = l // tm
    m = kv.shape[1]
    return pl.pallas_call(
        _mem_attn_kernel,
        out_shape=jax.ShapeDtypeStruct((b * l, BRANCH_W), F32),
        grid=(b, nm),
        in_specs=[
            pl.BlockSpec((tm, ME_COLS), lambda bi, i: (bi * nm + i, ME_OFF // ME_COLS)),
            pl.BlockSpec((None, m, 2 * BRANCH_W), lambda bi, i: (bi, 0, 0)),
        ],
        out_specs=pl.BlockSpec((tm, BRANCH_W), lambda bi, i: (bi * nm + i, 0)),
        compiler_params=_cparams(("parallel", "arbitrary")),
        name="memory_attention",
    )(z, kv)


DFT_NA = 128
DFT_NB = 128
HY_LBLK = 4096
HY_KB = 8
HIGHEST = lax.Precision.HIGHEST


def _dft_tables(l):
    assert 2 * l == DFT_NA * DFT_NB
    n = DFT_NA * DFT_NB
    i = jnp.arange(DFT_NA, dtype=jnp.int32)
    ang1 = (2.0 * math.pi / DFT_NA) * ((i[:, None] * i[None, :]) % DFT_NA).astype(F32)
    c1, s1 = jnp.cos(ang1), jnp.sin(ang1)
    half = DFT_NA // 2
    kk = i[:, None, None] + DFT_NA * i[None, :, None]
    ang = (2.0 * math.pi / n) * ((kk * i[None, None, :]) % n).astype(F32)
    gr, gi = jnp.cos(ang), -jnp.sin(ang)
    t = dict(
        f=jnp.concatenate([c1[:, :half], -s1[:, :half]], axis=0),
        gr=gr, gi=gi,
        gg=jnp.concatenate([gr, gi], axis=1),
        hh=jnp.concatenate([gr.transpose(0, 2, 1), gi.transpose(0, 2, 1)], axis=1),
        er=c1.T[:half] / n, ei=s1.T[:half] / n,
    )
    for name in ('f', 'gg', 'hh', 'er', 'ei'):
        t[name + '_bf'] = t[name].astype(BF16)
    return t


def _dft1_kernel(prec, x_ref, f_ref, ar_ref, ai_ref):
    x = x_ref[...]
    f = f_ref[...]
    if prec is None:
        a = jnp.dot(f, x.astype(BF16), preferred_element_type=F32)
    else:
        a = jnp.dot(f, x, precision=prec, preferred_element_type=F32)
    ar_ref[...] = a[:DFT_NA].astype(ar_ref.dtype)
    ai_ref[...] = a[DFT_NA:].astype(ai_ref.dtype)


def _dft_stage1(x, f, out_dtype, prec):
    g, r, w = x.shape
    spec_o = pl.BlockSpec((None, DFT_NA, HY_LBLK), lambda gi, j: (gi, 0, j))
    return pl.pallas_call(
        functools.partial(_dft1_kernel, prec),
        out_shape=[jax.ShapeDtypeStruct((g, DFT_NA, w), out_dtype)] * 2,
        grid=(g, w // HY_LBLK),
        in_specs=[pl.BlockSpec((None, r, HY_LBLK), lambda gi, j: (gi, 0, j)),
                  pl.BlockSpec((2 * DFT_NA, r), lambda gi, j: (0, 0))],
        out_specs=[spec_o, spec_o],
        compiler_params=_cparams(("parallel", "parallel")),
        name="hyena_dft_stage1",
    )(x, f)


def _filter_mlp_kernel(l, feats_ref, w1_ref, b1_ref, w2_ref, b2_ref, w3_ref, b3_ref, fq_ref, dl_ref, h_ref, nrm_ref):
    i = pl.program_id(0)
    tm = feats_ref.shape[0]
    dot = functools.partial(jnp.dot, precision=HIGHEST, preferred_element_type=F32)
    h = jnp.sin(fq_ref[0:1] * (dot(feats_ref[...], w1_ref[...]) + b1_ref[...]))
    h = jnp.sin(fq_ref[1:2] * (dot(h, w2_ref[...]) + b2_ref[...]))
    h = dot(h, w3_ref[...]) + b3_ref[...]
    row = i * tm + lax.broadcasted_iota(jnp.int32, (tm, 1), 0)
    t = row.astype(F32) * (1.0 / (l - 1))
    decay = jnp.exp(-t * dl_ref[...])
    h = h * jnp.concatenate([decay] * (2 * HY_ORDER), axis=1)
    col = lax.broadcasted_iota(jnp.int32, (1, 2 * HY_ORDER * HY_W), 1)
    bwd = jnp.bitwise_and(lax.shift_right_logical(col, HY_W.bit_length() - 1), 1) == 1
    h = jnp.where(jnp.logical_and(row == 0, bwd), 0.0, h)
    h_ref[...] = h

    @pl.when(i == 0)
    def _():
        nrm_ref[...] = jnp.zeros_like(nrm_ref)

    nrm_ref[...] += jnp.sum(jnp.abs(h), axis=0, keepdims=True)


def _filter_mlp(feats, w1, b1, w2, b2, w3, b3, freq, deltas, tm=1024):
    l = feats.shape[0]
    wcols = 2 * HY_ORDER * HY_W
    full = lambda a: pl.BlockSpec(a.shape, lambda i: (0,) * a.ndim)
    args = (feats, w1, b1.reshape(1, -1), w2, b2.reshape(1, -1), w3, b3.reshape(1, -1), freq, deltas.reshape(1, -1))
    return pl.pallas_call(
        functools.partial(_filter_mlp_kernel, l),
        out_shape=[jax.ShapeDtypeStruct((l, wcols), F32), jax.ShapeDtypeStruct((1, wcols), F32)],
        grid=(l // tm,),
        in_specs=[pl.BlockSpec((tm, feats.shape[1]), lambda i: (i, 0))] + [full(a) for a in args[1:]],
        out_specs=[pl.BlockSpec((tm, wcols), lambda i: (i, 0)), pl.BlockSpec((1, wcols), lambda i: (0, 0))],
        compiler_params=_cparams(("arbitrary",)),
        name="hyena_filter_mlp",
    )(*args)


def _filter_spec_kernel(afr_ref, afi_ref, abr_ref, abi_ref, gr_ref, gi_ref, nf_ref, nb_ref, kr_ref, ki_ref):
    dot = functools.partial(jnp.dot, precision=HIGHEST, preferred_element_type=F32)
    inv = 1.0 / (nf_ref[...] + nb_ref[...])
    for j in range(gr_ref.shape[0]):
        gr, gi = gr_ref[j], gi_ref[j]
        sr = afr_ref[j] + abr_ref[j]
        si = afi_ref[j] + abi_ref[j]
        dr = afr_ref[j] - abr_ref[j]
        di = afi_ref[j] - abi_ref[j]
        kr_ref[j] = (dot(gr, sr) - dot(gi, si)) * inv
        ki_ref[j] = (dot(gr, di) + dot(gi, dr)) * inv


def _filter_spectrum(ar, ai, gr, gi, nrm, kb=4, cw=256):
    ncw = HY_W // cw

    def a_spec(d):
        return pl.BlockSpec((kb, DFT_NB, cw), lambda o, ki, ci: (ki, 0, (2 * o + d) * ncw + ci))

    def n_spec(d):
        return pl.BlockSpec((1, cw), lambda o, ki, ci: (0, (2 * o + d) * ncw + ci))

    g_spec = pl.BlockSpec((kb, DFT_NB, DFT_NB), lambda o, ki, ci: (ki, 0, 0))
    o_spec = pl.BlockSpec((None, kb, DFT_NB, cw), lambda o, ki, ci: (o, ki, 0, ci))
    return pl.pallas_call(
        _filter_spec_kernel,
        out_shape=[jax.ShapeDtypeStruct((HY_ORDER, DFT_NA, DFT_NB, HY_W), F32)] * 2,
        grid=(HY_ORDER, DFT_NA // kb, ncw),
        in_specs=[a_spec(0), a_spec(0), a_spec(1), a_spec(1), g_spec, g_spec, n_spec(0), n_spec(1)],
        out_specs=[o_spec, o_spec],
        compiler_params=_cparams(("parallel", "parallel", "parallel")),
        name="hyena_filter_spectrum",
    )(ar, ai, ar, ai, gr, gi, nrm, nrm)


def _hy_mid_kernel(ar_ref, ai_ref, gg_ref, hh_ref, kr_ref, ki_ref, br_ref, bi_ref):
    for j in range(HY_KB):
        gg = gg_ref[j]
        p = jnp.dot(gg, ar_ref[j], preferred_element_type=F32)
        q = jnp.dot(gg, ai_ref[j], preferred_element_type=F32)
        xr = p[:DFT_NB] - q[DFT_NB:]
        xi = q[:DFT_NB] + p[DFT_NB:]
        kr, ki = kr_ref[j], ki_ref[j]
        yr = (xr * kr - xi * ki).astype(BF16)
        yi = (xr * ki + xi * kr).astype(BF16)
        hh = hh_ref[j]
        p = jnp.dot(hh, yr, preferred_element_type=F32)
        q = jnp.dot(hh, yi, preferred_element_type=F32)
        br_ref[j] = (p[:DFT_NB] + q[DFT_NB:]).astype(BF16)
        bi_ref[j] = (q[:DFT_NB] - p[DFT_NB:]).astype(BF16)


def _hy_mid(ar, ai, gg, hh, kr, ki):
    b = ar.shape[0]
    a_spec = pl.BlockSpec((None, HY_KB, DFT_NB, HY_W), lambda ki_, bi: (bi, ki_, 0, 0))
    t_spec = pl.BlockSpec((HY_KB, 2 * DFT_NB, DFT_NB), lambda ki_, bi: (ki_, 0, 0))
    k_spec = pl.BlockSpec((HY_KB, DFT_NB, HY_W), lambda ki_, bi: (ki_, 0, 0))
    return pl.pallas_call(
        _hy_mid_kernel,
        out_shape=[jax.ShapeDtypeStruct(ar.shape, BF16)] * 2,
        grid=(DFT_NA // HY_KB, b),
        in_specs=[a_spec, a_spec, t_spec, t_spec, k_spec, k_spec],
        out_specs=[a_spec, a_spec],
        compiler_params=_cparams(("parallel", "arbitrary")),
        name="hyena_dft_mid",
    )(ar, ai, gg, hh, kr, ki)


def _hy_out_kernel(br_ref, bi_ref, er_ref, ei_ref, u_ref, gate_ref, skip_ref, o_ref):
    y = (jnp.dot(er_ref[...], br_ref[...], preferred_element_type=F32)
         - jnp.dot(ei_ref[...], bi_ref[...], preferred_element_type=F32))
    o_ref[...] = gate_ref[...] * (y + u_ref[...] * skip_ref[...])


def _hy_out(br, bi, er, ei, u, gate, skip_row):
    b, r, w = u.shape
    row = pl.BlockSpec((None, r, HY_LBLK), lambda bi_, j: (bi_, 0, j))
    bspec = pl.BlockSpec((None, DFT_NA, HY_LBLK), lambda bi_, j: (bi_, 0, j))
    espec = pl.BlockSpec((r, DFT_NA), lambda bi_, j: (0, 0))
    return pl.pallas_call(
        _hy_out_kernel,
        out_shape=jax.ShapeDtypeStruct(u.shape, F32),
        grid=(b, w // HY_LBLK),
        in_specs=[bspec, bspec, espec, espec, row, row, pl.BlockSpec((1, HY_LBLK), lambda bi_, j: (0, 0))],
        out_specs=row,
        compiler_params=_cparams(("parallel", "parallel")),
        name="hyena_dft_out",
    )(br, bi, er, ei, u, gate, skip_row)


def _short_conv_kernel(z_ref, w_ref, b_ref, o_ref):
    z = z_ref[...]
    l = z.shape[0]
    row = lax.broadcasted_iota(jnp.int32, (l, 1), 0)
    prev = jnp.where(row == 0, 0.0, pltpu.roll(z, 1, 0))
    nxt = jnp.where(row == l - 1, 0.0, pltpu.roll(z, l - 1, 0))
    w = w_ref[...]
    o_ref[...] = prev * w[0:1] + z * w[1:2] + nxt * w[2:3] + b_ref[...]


def _short_conv(z, w, bias, b, l):
    nc = HY_W // LANE
    return pl.pallas_call(
        _short_conv_kernel,
        out_shape=jax.ShapeDtypeStruct((3, b * l, HY_W), F32),
        grid=(b, 3, nc),
        in_specs=[pl.BlockSpec((l, LANE), lambda bi, p, ci: (bi, p * nc + ci)),
                  pl.BlockSpec((3, LANE), lambda bi, p, ci: (0, p * nc + ci)),
                  pl.BlockSpec((1, LANE), lambda bi, p, ci: (0, p * nc + ci))],
        out_specs=pl.BlockSpec((None, l, LANE), lambda bi, p, ci: (p, bi, ci)),
        compiler_params=_cparams(("parallel", "parallel", "parallel")),
        name="hyena_short_conv",
    )(z, w, bias.reshape(1, -1))


def _hyena_filter_spectrum(l, w1, b1, w2, b2, w3, b3, freq, tabs):
    t = jnp.linspace(0.0, 1.0, l, dtype=F32)[:, None]
    w = 2.0 * math.pi * jnp.arange(l, dtype=F32)[:, None] / l
    fr = jnp.linspace(1e-4, HY_BANDS - 1, HY_BANDS, dtype=F32)[None, :]
    feats = jnp.concatenate([t, jnp.cos(fr * w), -jnp.sin(fr * w)], axis=-1)
    deltas = jnp.abs(jnp.linspace(HY_MIN_DECAY, HY_MAX_DECAY, HY_W, dtype=F32))
    h, nrm = _filter_mlp(feats, w1, b1, w2, b2, w3, b3, freq, deltas)
    wcols = h.shape[1]
    ar, ai = _dft_stage1(h.reshape(1, l // DFT_NB, DFT_NB * wcols), tabs['f'], F32, HIGHEST)
    return _filter_spectrum(ar.reshape(DFT_NA, DFT_NB, wcols), ai.reshape(DFT_NA, DFT_NB, wcols),
                            tabs['gr'], tabs['gi'], nrm)


def _hyena(z, short_w, short_b, skip, kr, ki, tabs, b, l):
    u = _short_conv(z, short_w, short_b, b, l)
    rows = l // DFT_NB
    view = lambda a: a.reshape(b, rows, DFT_NB * HY_W)
    v, gates = view(u[0]), (view(u[1]), view(u[2]))
    for o in range(HY_ORDER):
        ar, ai = _dft_stage1(v, tabs['f_bf'], BF16, None)
        shp4 = (b, DFT_NA, DFT_NB, HY_W)
        br, bi = _hy_mid(ar.reshape(shp4), ai.reshape(shp4), tabs['gg_bf'], tabs['hh_bf'], kr[o], ki[o])
        shp3 = (b, DFT_NA, DFT_NB * HY_W)
        skip_row = jnp.tile(skip[o], HY_LBLK // HY_W).reshape(1, HY_LBLK)
        v = _hy_out(br.reshape(shp3), bi.reshape(shp3), tabs['er_bf'], tabs['ei_bf'], v, gates[o], skip_row)
    return v.reshape(b * l, HY_W)


HG_REC_CHUNK = 64
HG_SUB = 16
HG_TILE = 512
HG_HEADS_PER_STEP = 2


def _gate_terms(z, lb):
    a = jnp.log(lb)
    b = jnp.log1p(-lb) + (jnp.minimum(z, 0.0) - jnp.log1p(jnp.exp(-jnp.abs(z))))
    log_f = jnp.maximum(a, b) + jnp.log1p(jnp.exp(-jnp.abs(a - b)))
    return log_f, (1.0 - lb) * jax.nn.sigmoid(-z)


def _gla_chunk(q, k, v, g, st, fwd):
    ch = q.shape[0]
    ri = lax.broadcasted_iota(jnp.int32, (ch, ch), 0)
    ci = lax.broadcasted_iota(jnp.int32, (ch, ch), 1)
    tri = jnp.where((ci <= ri) if fwd else (ci >= ri), 1.0, 0.0).astype(F32)
    b = jnp.dot(tri, g, precision=lax.Precision.HIGHEST, preferred_element_type=F32)
    nsb = ch // HG_SUB
    parts = [None] * nsb

    def add(i, val):
        parts[i] = val if parts[i] is None else parts[i] + val

    vb = v.astype(BF16)
    li = lax.broadcasted_iota(jnp.int32, (ch, HG_SUB), 1)
    si = jnp.bitwise_and(lax.broadcasted_iota(jnp.int32, (ch, HG_SUB), 0), HG_SUB - 1)
    keep = (li <= si) if fwd else (li >= si)
    a = jnp.zeros((ch, HG_SUB), F32)
    for s in range(HG_SUB):
        src = [i * HG_SUB + s for i in range(nsb)]
        bs = jnp.concatenate([jnp.broadcast_to(b[r:r + 1], (HG_SUB, LANE)) for r in src], axis=0)
        ks = jnp.concatenate([jnp.broadcast_to(k[r:r + 1], (HG_SUB, LANE)) for r in src], axis=0)
        e = jnp.exp(jnp.minimum(b - bs, 0.0))
        a = jnp.where(li == s, jnp.sum(q * e * ks, axis=-1, keepdims=True), a)
    a = jnp.where(keep, a, 0.0).astype(BF16)
    for i in range(nsb):
        sl = slice(i * HG_SUB, (i + 1) * HG_SUB)
        add(i, jnp.dot(a[sl], vb[sl], preferred_element_type=F32))
    h = HG_SUB
    while h < ch:
        for j in range(ch // (2 * h)):
            lo = slice(2 * h * j, 2 * h * j + h)
            hi = slice(2 * h * j + h, 2 * h * j + 2 * h)
            if fwd:
                rows, cols, ref = hi, lo, b[2 * h * j + h - 1:2 * h * j + h]
            else:
                rows, cols, ref = lo, hi, b[2 * h * j + h:2 * h * j + h + 1]
            qt = (q[rows] * jnp.exp(b[rows] - ref)).astype(BF16)
            kt = (k[cols] * jnp.exp(ref - b[cols])).astype(BF16)
            a = lax.dot_general(qt, kt, (((1,), (1,)), ((), ())), preferred_element_type=F32)
            ov = jnp.dot(a.astype(BF16), vb[cols], preferred_element_type=F32)
            for ii in range(h // HG_SUB):
                add(rows.start // HG_SUB + ii, ov[ii * HG_SUB:(ii + 1) * HG_SUB])
        h *= 2
    o = jnp.concatenate(parts, axis=0)
    qh = (q * jnp.exp(b)).astype(BF16)
    o = o + lax.dot_general(qh, st.astype(BF16), (((1,), (1,)), ((), ())), preferred_element_type=F32)
    btot = b[ch - 1:ch] if fwd else b[0:1]
    kh = (k * jnp.exp(btot - b)).astype(BF16)
    st_new = st * jnp.exp(btot) + lax.dot_general(vb, kh, (((0,), (0,)), ((), ())), preferred_element_type=F32)
    return o, st_new


def _hgrn2_kernel(lbf_ref, lbb_ref, ng_ref, zqf_ref, zif_ref, zff_ref, zgf_ref, zqb_ref, zib_ref, zbb_ref, zgb_ref,
                  o_ref, st_ref):
    c = pl.program_id(2)
    n = pl.num_programs(2)
    nch = HG_TILE // HG_REC_CHUNK

    @pl.when(c == 0)
    def _():
        st_ref[...] = jnp.zeros_like(st_ref)

    def body(second_pass, i, carry):
        off_f = pl.multiple_of(i * HG_REC_CHUNK, HG_REC_CHUNK)
        off_b = pl.multiple_of((nch - 1 - i) * HG_REC_CHUNK, HG_REC_CHUNK)
        for fwd, off, zq_ref, zi_ref, zd_ref, zg_ref, lb_ref, tile in (
                (True, off_f, zqf_ref, zif_ref, zff_ref, zgf_ref, lbf_ref, c),
                (False, off_b, zqb_ref, zib_ref, zbb_ref, zgb_ref, lbb_ref, n - 1 - c)):
            sl = pl.ds(off, HG_REC_CHUNK)
            rows = pl.ds(pl.multiple_of(tile * HG_TILE + off, HG_REC_CHUNK), HG_REC_CHUNK)
            d = 0 if fwd else 1
            for hh in range(HG_HEADS_PER_STEP):
                ln = slice(hh * LANE, (hh + 1) * LANE)
                g, k = _gate_terms(zd_ref[sl, ln], lb_ref[:, ln])
                o, st_new = _gla_chunk(jax.nn.silu(zq_ref[sl, ln]), k, zi_ref[sl, ln], g, st_ref[d, hh], fwd)
                st_ref[d, hh] = st_new
                if second_pass:
                    tot = o_ref[rows, ln] + o
                    tot = tot * lax.rsqrt(jnp.mean(tot * tot, axis=-1, keepdims=True) + 1e-6) * ng_ref[...]
                    o_ref[rows, ln] = tot * jax.nn.silu(zg_ref[sl, ln])
                else:
                    o_ref[rows, ln] = o
        return carry

    @pl.when(c < n // 2)
    def _():
        lax.fori_loop(0, nch, functools.partial(body, False), 0)

    @pl.when(c >= n // 2)
    def _():
        lax.fori_loop(0, nch, functools.partial(body, True), 0)


def _hgrn2(z, lb_fwd, lb_bwd, norm_g, b, l):
    n = l // HG_TILE
    assert n % 2 == 0
    w = HG_HEADS_PER_STEP * LANE
    base = HY_COLS // w
    nh = BRANCH_W // w

    def zspec(part, rev):
        if rev:
            return pl.BlockSpec((HG_TILE, w), lambda bi, hi, ci: (bi * n + n - 1 - ci, base + part * nh + hi))
        return pl.BlockSpec((HG_TILE, w), lambda bi, hi, ci: (bi * n + ci, base + part * nh + hi))

    lbspec = pl.BlockSpec((1, w), lambda bi, hi, ci: (0, hi))
    return pl.pallas_call(
        _hgrn2_kernel,
        out_shape=jax.ShapeDtypeStruct((b * l, BRANCH_W), F32),
        grid=(b, nh, n),
        in_specs=[lbspec, lbspec, pl.BlockSpec((1, LANE), lambda bi, hi, ci: (0, 0)),
                  zspec(0, False), zspec(1, False), zspec(2, False), zspec(4, False),
                  zspec(0, True), zspec(1, True), zspec(3, True), zspec(4, True)],
        out_specs=pl.BlockSpec((l, w), lambda bi, hi, ci: (bi, hi)),
        scratch_shapes=[pltpu.VMEM((2, HG_HEADS_PER_STEP, HG_DV, HG_DK), F32)],
        compiler_params=_cparams(("parallel", "parallel", "arbitrary")),
        name="hgrn2_scan",
    )(lb_fwd.reshape(1, BRANCH_W), lb_bwd.reshape(1, BRANCH_W), norm_g.reshape(1, HG_DV),
      z, z, z, z, z, z, z, z)


def _merge_kernel(x_ref, o0_ref, o1_ref, o2_ref, o3_ref, wg_ref, bg_ref, wbr_ref, wout_ref, g_ref, b_ref,
                  out_ref, acc_ref, xb_ref, ob_ref):
    j = pl.program_id(1)

    @pl.when(j == 0)
    def _():
        xb_ref[...] = x_ref[...].astype(BF16)
        for bi, o_ref in enumerate((o0_ref, o1_ref, o2_ref, o3_ref)):
            ob_ref[bi] = o_ref[...].astype(BF16)
        acc_ref[...] = jnp.zeros_like(acc_ref)

    xb = xb_ref[...]
    merged = None
    for bi in range(N_BRANCH):
        gate = jax.nn.sigmoid(jnp.dot(xb, wg_ref[bi], preferred_element_type=F32) + bg_ref[bi])
        term = gate * jnp.dot(ob_ref[bi], wbr_ref[bi], preferred_element_type=F32)
        merged = term if merged is None else merged + term
    acc_ref[...] += jnp.dot(merged.astype(BF16), wout_ref[...], preferred_element_type=F32)

    @pl.when(j == pl.num_programs(1) - 1)
    def _():
        out_ref[...] = _ln_rows(DN_ALPHA * x_ref[...] + acc_ref[...], g_ref[...], b_ref[...])


def _merge(x, branches, w_gate, b_gate, w_br, w_out, ln_g, ln_b, tm=512, tn=256):
    t, d = x.shape
    row = pl.BlockSpec((tm, d), lambda i, j: (i, 0))
    brow = pl.BlockSpec((tm, BRANCH_W), lambda i, j: (i, 0))
    vec = pl.BlockSpec((1, d), lambda i, j: (0, 0))
    return pl.pallas_call(
        _merge_kernel,
        out_shape=jax.ShapeDtypeStruct((t, d), F32),
        grid=(t // tm, d // tn),
        in_specs=[
            row, brow, brow, brow, brow,
            pl.BlockSpec((N_BRANCH, d, tn), lambda i, j: (0, 0, j)),
            pl.BlockSpec((N_BRANCH, 1, tn), lambda i, j: (0, 0, j)),
            pl.BlockSpec((N_BRANCH, BRANCH_W, tn), lambda i, j: (0, 0, j)),
            pl.BlockSpec((tn, d), lambda i, j: (j, 0)),
            vec, vec,
        ],
        out_specs=row,
        scratch_shapes=[pltpu.VMEM((tm, d), F32), pltpu.VMEM((tm, d), BF16),
                        pltpu.VMEM((N_BRANCH, tm, BRANCH_W), BF16)],
        compiler_params=_cparams(("parallel", "arbitrary")),
        name="gated_merge",
    )(x, *branches, w_gate, b_gate.reshape(N_BRANCH, 1, d), w_br, w_out, ln_g.reshape(1, d), ln_b.reshape(1, d))


def _expert_kernel(x_ref, gate_ref, w1_ref, w3_ref, w2_ref, o_ref):
    f = pl.program_id(2)
    x = x_ref[...]
    h1 = jnp.dot(x, w1_ref[...], preferred_element_type=F32)
    h3 = jnp.dot(x, w3_ref[...], preferred_element_type=F32)
    h = (jax.nn.silu(h1) * h3).astype(BF16)
    part = jnp.dot(h, w2_ref[...], preferred_element_type=F32)

    @pl.when(f == 0)
    def _():
        o_ref[...] = part

    @pl.when(f != 0)
    def _():
        o_ref[...] += part

    @pl.when(f == pl.num_programs(2) - 1)
    def _():
        o_ref[...] = o_ref[...] * gate_ref[...]


def _expert_ffn(xe, gate, w1, w3, w2, tm=1024, tf=512):
    e, c, d = xe.shape
    ff = w1.shape[2]
    tm = min(tm, c)
    return pl.pallas_call(
        _expert_kernel,
        out_shape=jax.ShapeDtypeStruct((e, c, d), F32),
        grid=(e, c // tm, ff // tf),
        in_specs=[
            pl.BlockSpec((None, tm, d), lambda ei, i, f: (ei, i, 0)),
            pl.BlockSpec((None, tm, 1), lambda ei, i, f: (ei, i, 0)),
            pl.BlockSpec((None, d, tf), lambda ei, i, f: (ei, 0, f)),
            pl.BlockSpec((None, d, tf), lambda ei, i, f: (ei, 0, f)),
            pl.BlockSpec((None, tf, d), lambda ei, i, f: (ei, f, 0)),
        ],
        out_specs=pl.BlockSpec((None, tm, d), lambda ei, i, f: (ei, i, 0)),
        compiler_params=_cparams(("parallel", "parallel", "arbitrary")),
        name="expert_ffn",
    )(xe, gate, w1, w3, w2)


RT_TM = 1024


def _router_kernel(x_ref, w_ref, o_ref):
    logits = lax.dot_general(w_ref[...], x_ref[...].astype(BF16), (((1,), (1,)), ((), ())),
                             preferred_element_type=F32)
    e = jnp.exp(logits - jnp.max(logits, axis=0, keepdims=True))
    aff = e / jnp.sum(e, axis=0, keepdims=True)
    for c in range(x_ref.shape[0] // LANE):
        o_ref[:, c, :] = aff[:, c * LANE:(c + 1) * LANE]


def _router(x, w_t):
    t, d = x.shape
    return pl.pallas_call(
        _router_kernel,
        out_shape=jax.ShapeDtypeStruct((N_EXPERTS, t // LANE, LANE), F32),
        grid=(t // RT_TM,),
        in_specs=[pl.BlockSpec((RT_TM, d), lambda i: (i, 0)), pl.BlockSpec((N_EXPERTS, d), lambda i: (0, 0))],
        out_specs=pl.BlockSpec((N_EXPERTS, RT_TM // LANE, LANE), lambda i: (0, i, 0)),
        compiler_params=_cparams(("parallel",)),
        name="moe_router",
    )(x, w_t)


def _prefix_count(mask):
    nch = mask.shape[0]
    m = jnp.where(mask, 1.0, 0.0)
    li = lax.broadcasted_iota(jnp.int32, (LANE, LANE), 0)
    lj = lax.broadcasted_iota(jnp.int32, (LANE, LANE), 1)
    incl = jnp.dot(m.astype(BF16), jnp.where(li <= lj, 1.0, 0.0).astype(BF16), preferred_element_type=F32)
    tot = jnp.broadcast_to(incl[:, LANE - 1:LANE], (nch, LANE))
    ri = lax.broadcasted_iota(jnp.int32, (nch, nch), 0)
    ci = lax.broadcasted_iota(jnp.int32, (nch, nch), 1)
    start = jnp.dot(jnp.where(ci < ri, 1.0, 0.0).astype(BF16), tot.astype(BF16), preferred_element_type=F32)
    return incl - m + start, start


def _route_select_kernel(cap, a_ref, pos_ref, start_ref):
    bits = pltpu.bitcast(a_ref[...], jnp.int32)

    def step(i, tau):
        cand = jnp.bitwise_or(tau, lax.shift_left(jnp.int32(1), 30 - i))
        cnt = jnp.sum(jnp.where(bits >= cand, 1.0, 0.0))
        return jnp.where(cnt >= cap, cand, tau)

    tau = lax.fori_loop(0, 31, step, jnp.int32(0))
    gt = bits > tau
    eq = bits == tau
    need = cap - jnp.sum(jnp.where(gt, 1.0, 0.0))
    eq_rank, _ = _prefix_count(eq)
    sel = jnp.logical_or(gt, jnp.logical_and(eq, eq_rank < need))
    pos, start = _prefix_count(sel)
    pos_ref[...] = jnp.where(sel, pos, -1.0)
    start_ref[...] = start


def _route_select(aff3, cap):
    e, nch, _ = aff3.shape
    spec = pl.BlockSpec((None, nch, LANE), lambda ei: (ei, 0, 0))
    return pl.pallas_call(
        functools.partial(_route_select_kernel, float(cap)),
        out_shape=[jax.ShapeDtypeStruct(aff3.shape, F32)] * 2,
        grid=(e,),
        in_specs=[spec],
        out_specs=[spec, spec],
        compiler_params=_cparams(("parallel",)),
        name="moe_route_select",
    )(aff3)


def _route_compact_kernel(cap, start_sm, pos_ref, a_ref, idx_ref, gate_ref, il_ref, gl_ref):
    ei = pl.program_id(0)
    nch = pos_ref.shape[0]
    win = 2 * LANE
    il_ref[...] = jnp.zeros_like(il_ref)
    gl_ref[...] = jnp.zeros_like(gl_ref)
    s_iota = lax.broadcasted_iota(jnp.int32, (win, LANE), 0).astype(F32)
    lane = lax.broadcasted_iota(jnp.int32, (1, LANE), 1).astype(F32)

    def body(j, carry):
        first = start_sm[ei, j]
        base = pl.multiple_of(lax.shift_left(lax.shift_right_logical(first, 7), 7), LANE)
        hit = (pos_ref[pl.ds(j, 1), :] - base.astype(F32)) == s_iota
        tok = lane + (j * LANE).astype(F32)
        il_ref[pl.ds(base, win), :] += jnp.sum(jnp.where(hit, tok, 0.0), axis=1, keepdims=True)
        gl_ref[pl.ds(base, win), :] += jnp.sum(jnp.where(hit, a_ref[pl.ds(j, 1), :], 0.0), axis=1, keepdims=True)
        return carry

    lax.fori_loop(0, nch, body, 0)
    diag = (lax.broadcasted_iota(jnp.int32, (LANE, LANE), 0) == lax.broadcasted_iota(jnp.int32, (LANE, LANE), 1))
    for r in range(idx_ref.shape[0]):
        rows = slice(r * LANE, (r + 1) * LANE)
        idx_ref[r:r + 1, :] = jnp.sum(jnp.where(diag, il_ref[rows, :], 0.0), axis=0, keepdims=True).astype(jnp.int32)
        gate_ref[r:r + 1, :] = jnp.sum(jnp.where(diag, gl_ref[rows, :], 0.0), axis=0, keepdims=True)


def _route_compact(start, pos3, aff3, cap):
    e, nch, _ = pos3.shape
    spec = pl.BlockSpec((None, nch, LANE), lambda ei, s: (ei, 0, 0))
    ospec = pl.BlockSpec((None, cap // LANE, LANE), lambda ei, s: (ei, 0, 0))
    return pl.pallas_call(
        functools.partial(_route_compact_kernel, cap),
        out_shape=[jax.ShapeDtypeStruct((e, cap // LANE, LANE), jnp.int32),
                   jax.ShapeDtypeStruct((e, cap // LANE, LANE), F32)],
        grid_spec=pltpu.PrefetchScalarGridSpec(
            num_scalar_prefetch=1, grid=(e,), in_specs=[spec, spec], out_specs=[ospec, ospec],
            scratch_shapes=[pltpu.VMEM((cap + 2 * LANE, LANE), F32), pltpu.VMEM((cap + 2 * LANE, LANE), F32)]),
        compiler_params=_cparams(("parallel",)),
        name="moe_route_compact",
    )(start, pos3, aff3)


def _expert_choice(x, w_t, cap):
    aff3 = _router(x, w_t)
    pos3, start3 = _route_select(aff3, cap)
    idx3, gate3 = _route_compact(start3[:, :, 0].astype(jnp.int32), pos3, aff3, cap)
    return idx3.reshape(N_EXPERTS, cap), gate3.reshape(N_EXPERTS, cap)


def _attn_prep_kernel(zq_ref, zk_ref, zv_ref, cos_ref, sin_ref, q_ref, k_ref, v_ref):
    lane = lax.broadcasted_iota(jnp.int32, (1, LANE), 1)
    first = jnp.bitwise_and(lane, DA_HD - 1) < ROPE_DIM // 2
    cos, sin = cos_ref[...], sin_ref[...]

    def rope(x):
        partner = jnp.where(first, pltpu.roll(x, LANE - ROPE_DIM // 2, 1), pltpu.roll(x, ROPE_DIM // 2, 1))
        return x * cos + partner * sin

    q_ref[...] = (rope(zq_ref[...]) * (DA_HD ** -0.5)).astype(BF16)
    k_ref[...] = rope(zk_ref[...]).astype(BF16)
    v_ref[...] = zv_ref[...].astype(BF16)


def _rope_lane_tables(l):
    inv = ROPE_THETA ** (-jnp.arange(0, ROPE_DIM, 2, dtype=F32) / ROPE_DIM)
    ang = jnp.arange(l, dtype=F32)[:, None] * inv[None, :]
    c, s = jnp.cos(ang), jnp.sin(ang)
    rest = DA_HD - ROPE_DIM
    cos_h = jnp.concatenate([c, c, jnp.ones((l, rest), F32)], axis=1)
    sin_h = jnp.concatenate([-s, s, jnp.zeros((l, rest), F32)], axis=1)
    return jnp.concatenate([cos_h, cos_h], axis=1), jnp.concatenate([sin_h, sin_h], axis=1)


def _attn_prep(z, cos, sin, b, l, tm=1024):
    n = l // tm
    base = DA_OFF // LANE

    def zspec(part):
        return pl.BlockSpec((tm, LANE), lambda bi, hi, i: (bi * n + i, base + part * DA_HEADS + hi))

    tspec = pl.BlockSpec((tm, LANE), lambda bi, hi, i: (i, 0))
    ospec = pl.BlockSpec((None, None, tm, LANE), lambda bi, hi, i: (bi, hi, i, 0))
    return pl.pallas_call(
        _attn_prep_kernel,
        out_shape=[jax.ShapeDtypeStruct((b, DA_HEADS, l, DA_VD), BF16)] * 3,
        grid=(b, DA_HEADS, n),
        in_specs=[zspec(0), zspec(1), zspec(2), tspec, tspec],
        out_specs=[ospec, ospec, ospec],
        compiler_params=_cparams(("parallel", "parallel", "parallel")),
        name="attention_prep",
    )(z, z, z, cos, sin)


def _trunk(x, mem, p):
    B, L, D = x.shape
    T = B * L
    cos, sin = _rope_lane_tables(L)
    lb_all = jnp.cumsum(jax.nn.softmax(p['hg_lb_raw'], axis=1), axis=1)
    lb_all = lb_all - lb_all[:, :1]
    xf = _layer_norm(x.reshape(T, D), p['ln_in_g'], p['ln_in_b'])
    memf = mem.reshape(B * mem.shape[1], D)
    cap = EC_FACTOR * T // N_EXPERTS
    for l in range(DEPTH):
        z = _matmul(xf, p['w_in'][l], 1024, 1024)
        hy_kr, hy_ki = p['hy_spectrum'][L][l]
        o_hy = _hyena(z, p['hy_short_w'][l], p['hy_short_b'][l], p['hy_skip'][l], hy_kr, hy_ki,
                      p['dft_tables'][L], B, L)
        o_hg = _hgrn2(z, lb_all[0, l], lb_all[1, l], p['hg_norm_g'][l], B, L)
        qr, kr, vr = _attn_prep(z, cos, sin, B, L)
        lam_init = 0.8 - 0.6 * math.exp(-0.3 * l)
        o_da = _diff_attention(qr, kr, vr, p['da_lambda'][l], p['da_norm_g'][l], lam_init)
        kv = _matmul(memf, p['w_mem_kv'][l], min(512, memf.shape[0]), 512).reshape(B, -1, 2 * BRANCH_W)
        o_me = _memory_attention(z, kv, B, L)
        xf = _merge(xf, (o_hy, o_hg, o_da, o_me), p['w_gate'][l], p['b_gate'][l], p['w_br'][l], p['w_out'][l],
                    p['ln1_g'][l], p['ln1_b'][l])
        idx, gate = _expert_choice(xf, p['w_router_t'][l], cap)
        xe = xf.astype(BF16)[idx]
        ye = _expert_ffn(xe, gate[..., None], p['w_e1'][l], p['w_e3'][l], p['w_e2'][l])
        y = jnp.zeros_like(xf).at[idx.reshape(-1)].add(ye.reshape(-1, D))
        xf = _layer_norm(xf, p['ln2_g'][l], p['ln2_b'][l], resid=y, alpha=DN_ALPHA)
    return xf.reshape(B, L, D)


def kernel(x_prompt, x_sample, mem_prompt, mem_sample, ln_in_g, ln_in_b, w_in, hy_short_w, hy_short_b, hy_ffn_w1, hy_ffn_b1, hy_ffn_w2, hy_ffn_b2, hy_ffn_w3, hy_ffn_b3, hy_freq, hy_skip, hg_lb_raw, hg_norm_g, da_lambda, da_norm_g, w_mem_kv, w_gate, b_gate, w_br, w_out, ln1_g, ln1_b, w_router, w_e1, w_e3, w_e2, ln2_g, ln2_b):
    p = dict(ln_in_g=ln_in_g, ln_in_b=ln_in_b, hy_short_w=hy_short_w, hy_short_b=hy_short_b,
             hy_ffn_w1=hy_ffn_w1, hy_ffn_b1=hy_ffn_b1, hy_ffn_w2=hy_ffn_w2, hy_ffn_b2=hy_ffn_b2,
             hy_ffn_w3=hy_ffn_w3, hy_ffn_b3=hy_ffn_b3, hy_freq=hy_freq, hy_skip=hy_skip, hg_lb_raw=hg_lb_raw,
             hg_norm_g=hg_norm_g, da_lambda=da_lambda, da_norm_g=da_norm_g, b_gate=b_gate,
             ln1_g=ln1_g, ln1_b=ln1_b, ln2_g=ln2_g, ln2_b=ln2_b)
    for name, w in (('w_in', w_in), ('w_mem_kv', w_mem_kv), ('w_gate', w_gate), ('w_br', w_br), ('w_out', w_out),
                    ('w_e1', w_e1), ('w_e3', w_e3), ('w_e2', w_e2)):
        p[name] = w.astype(BF16)
    p['w_router_t'] = jnp.swapaxes(w_router, 1, 2).astype(BF16)
    p['dft_tables'], p['hy_spectrum'] = {}, {}
    for seq_len in sorted({x_prompt.shape[1], x_sample.shape[1]}):
        tabs = _dft_tables(seq_len)
        p['dft_tables'][seq_len] = tabs
        p['hy_spectrum'][seq_len] = [
            _hyena_filter_spectrum(seq_len, hy_ffn_w1[l], hy_ffn_b1[l], hy_ffn_w2[l], hy_ffn_b2[l], hy_ffn_w3[l],
                                   hy_ffn_b3[l], hy_freq[l], tabs) for l in range(DEPTH)]
    y_prompt = _trunk(x_prompt, mem_prompt, p)
    y_sample = _trunk(x_sample, mem_sample, p)
    return (y_prompt, y_sample)
```

```python
import functools
import math

import jax
import jax.numpy as jnp
from jax import lax
from jax.experimental import pallas as pl
from jax.experimental.pallas import tpu as pltpu

F32 = jnp.float32
BF16 = jnp.bfloat16

D_MODEL = 2048
DEPTH = 4
N_BRANCH = 4
BRANCH_W = D_MODEL // 4
HY_W = BRANCH_W
HY_ORDER = 2
HY_EMB = 33
HY_BANDS = (HY_EMB - 1) // 2
HY_MIN_DECAY = math.log(1e-2) / 1.5
HY_MAX_DECAY = math.log(1e-2) / 0.3
HG_HEADS = 4
HG_DK = BRANCH_W // HG_HEADS
HG_DV = BRANCH_W // HG_HEADS
HG_CHUNK = 64
DA_HEADS = 4
DA_HD = BRANCH_W // (2 * DA_HEADS)
DA_VD = 2 * DA_HD
ROPE_DIM = DA_HD // 4
ROPE_THETA = 500000.0
MEM_HEADS = 4
MEM_HD = BRANCH_W // MEM_HEADS
N_EXPERTS = 16
EC_FACTOR = 2
EXPERT_FF = 2048
DN_ALPHA = (2 * DEPTH) ** 0.25
LN_EPS = 1e-5
HY_COLS = 3 * HY_W
HG_COLS = 5 * BRANCH_W
DA_COLS = 3 * BRANCH_W
ME_COLS = BRANCH_W
IN_W = HY_COLS + HG_COLS + DA_COLS + ME_COLS
DA_OFF = HY_COLS + HG_COLS
ME_OFF = DA_OFF + DA_COLS

V7X_VMEM_LIMIT_BYTES = 56 * 1024 * 1024
LANE = 128


def _cparams(sem):
    return pltpu.CompilerParams(dimension_semantics=sem, vmem_limit_bytes=V7X_VMEM_LIMIT_BYTES)


def _mm_kernel(a_ref, w_ref, o_ref):
    o_ref[...] = jnp.dot(a_ref[...].astype(BF16), w_ref[...], preferred_element_type=F32)


def _matmul(a, w, tm, tn):
    m, k = a.shape
    n = w.shape[1]
    return pl.pallas_call(
        _mm_kernel,
        out_shape=jax.ShapeDtypeStruct((m, n), F32),
        grid=(m // tm, n // tn),
        in_specs=[pl.BlockSpec((tm, k), lambda i, j: (i, 0)), pl.BlockSpec((k, tn), lambda i, j: (0, j))],
        out_specs=pl.BlockSpec((tm, tn), lambda i, j: (i, j)),
        compiler_params=_cparams(("parallel", "arbitrary")),
        name="dense_matmul",
    )(a, w)


def _ln_rows(y, g, b):
    mu = jnp.mean(y, axis=-1, keepdims=True)
    d = y - mu
    var = jnp.mean(d * d, axis=-1, keepdims=True)
    return d * lax.rsqrt(var + LN_EPS) * g + b


def _ln_kernel(alpha, x_ref, r_ref, g_ref, b_ref, o_ref):
    y = x_ref[...]
    if r_ref is not None:
        y = alpha * y + r_ref[...]
    o_ref[...] = _ln_rows(y, g_ref[...], b_ref[...])


def _layer_norm(x, g, b, resid=None, alpha=1.0, tm=512):
    t, d = x.shape
    row = pl.BlockSpec((tm, d), lambda i: (i, 0))
    vec = pl.BlockSpec((1, d), lambda i: (0, 0))
    if resid is None:
        kern = lambda x_ref, g_ref, b_ref, o_ref: _ln_kernel(alpha, x_ref, None, g_ref, b_ref, o_ref)
        args, specs = (x, g.reshape(1, d), b.reshape(1, d)), [row, vec, vec]
    else:
        kern = functools.partial(_ln_kernel, alpha)
        args, specs = (x, resid, g.reshape(1, d), b.reshape(1, d)), [row, row, vec, vec]
    return pl.pallas_call(
        kern,
        out_shape=jax.ShapeDtypeStruct((t, d), F32),
        grid=(t // tm,),
        in_specs=specs,
        out_specs=row,
        compiler_params=_cparams(("parallel",)),
        name="layer_norm",
    )(*args)


def _da_kernel(lam_init, tk, lp_ref, g_ref, q_ref, k_ref, v_ref, o_ref):
    tq = q_ref.shape[0]
    nk = k_ref.shape[0] // tk
    q = q_ref[...]
    qs = (q[:, :DA_HD], q[:, DA_HD:])

    def body(i, carry):
        off = pl.multiple_of(i * tk, tk)
        k = k_ref[pl.ds(off, tk), :]
        v = v_ref[pl.ds(off, tk), :]
        out = []
        for c in range(2):
            m_prev, l_prev, acc_prev = carry[c]
            s = lax.dot_general(qs[c], k[:, c * DA_HD:(c + 1) * DA_HD], (((1,), (1,)), ((), ())),
                                preferred_element_type=F32)
            m_new = jnp.maximum(m_prev, jnp.max(s, axis=-1, keepdims=True))
            a = jnp.exp(m_prev - m_new)
            p = jnp.exp(s - m_new)
            l_new = a * l_prev + jnp.sum(p, axis=-1, keepdims=True)
            acc_new = a * acc_prev + jnp.dot(p.astype(BF16), v, preferred_element_type=F32)
            out.append((m_new, l_new, acc_new))
        return tuple(out)

    init = tuple((jnp.full((tq, 1), -jnp.inf, F32), jnp.zeros((tq, 1), F32), jnp.zeros((tq, DA_VD), F32))
                 for _ in range(2))
    (_, l0, a0), (_, l1, a1) = lax.fori_loop(0, nk, body, init)
    lp = lp_ref[...]
    lam = (jnp.exp(jnp.sum(lp[0:1] * lp[1:2], axis=-1, keepdims=True))
           - jnp.exp(jnp.sum(lp[2:3] * lp[3:4], axis=-1, keepdims=True)) + lam_init)
    o = a0 / l0 - lam * (a1 / l1)
    o = o * lax.rsqrt(jnp.mean(o * o, axis=-1, keepdims=True) + 1e-6) * g_ref[...]
    o_ref[...] = o * (1.0 - lam_init)


def _diff_attention(qr, kr, vr, lam_params, norm_g, lam_init, tq=512, tk=2048):
    b, h, l, _ = qr.shape
    nq = l // tq
    tk = min(tk, l)
    return pl.pallas_call(
        functools.partial(_da_kernel, lam_init, tk),
        out_shape=jax.ShapeDtypeStruct((b * l, BRANCH_W), F32),
        grid=(b, h, nq),
        in_specs=[
            pl.BlockSpec((4, DA_HD), lambda bi, hi, qi: (0, 0)),
            pl.BlockSpec((1, DA_VD), lambda bi, hi, qi: (0, 0)),
            pl.BlockSpec((None, None, tq, DA_VD), lambda bi, hi, qi: (bi, hi, qi, 0)),
            pl.BlockSpec((None, None, l, DA_VD), lambda bi, hi, qi: (bi, hi, 0, 0)),
            pl.BlockSpec((None, None, l, DA_VD), lambda bi, hi, qi: (bi, hi, 0, 0)),
        ],
        out_specs=pl.BlockSpec((tq, DA_VD), lambda bi, hi, qi: (bi * nq + qi, hi)),
        compiler_params=_cparams(("parallel", "parallel", "arbitrary")),
        name="diff_attention",
    )(lam_params, norm_g.reshape(1, DA_VD), qr, kr, vr)


def _mem_attn_kernel(q_ref, kv_ref, o_ref):
    q = q_ref[...]
    kv = kv_ref[...]
    scale = MEM_HD ** -0.5
    for h in range(MEM_HEADS):
        qh = q[:, h * MEM_HD:(h + 1) * MEM_HD].astype(BF16)
        kh = kv[:, h * MEM_HD:(h + 1) * MEM_HD].astype(BF16)
        vh = kv[:, BRANCH_W + h * MEM_HD:BRANCH_W + (h + 1) * MEM_HD].astype(BF16)
        s = lax.dot_general(qh, kh, (((1,), (1,)), ((), ())), preferred_element_type=F32) * scale
        e = jnp.exp(s - jnp.max(s, axis=-1, keepdims=True))
        p = e / jnp.sum(e, axis=-1, keepdims=True)
        o_ref[:, h * MEM_HD:(h + 1) * MEM_HD] = jnp.dot(p.astype(BF16), vh, preferred_element_type=F32)


def _memory_attention(z, kv, b, l, tm=512):
    nm = l // tm
    m = kv.shape[1]
    return pl.pallas_call(
        _mem_attn_kernel,
        out_shape=jax.ShapeDtypeStruct((b * l, BRANCH_W), F32),
        grid=(b, nm),
        in_specs=[
            pl.BlockSpec((tm, ME_COLS), lambda bi, i: (bi * nm + i, ME_OFF // ME_COLS)),
            pl.BlockSpec((None, m, 2 * BRANCH_W), lambda bi, i: (bi, 0, 0)),
        ],
        out_specs=pl.BlockSpec((tm, BRANCH_W), lambda bi, i: (bi * nm + i, 0)),
        compiler_params=_cparams(("parallel", "arbitrary")),
        name="memory_attention",
    )(z, kv)


DFT_NA = 128
DFT_NB = 128
HY_LBLK = 4096
HY_KB = 8
HIGHEST = lax.Precision.HIGHEST


def _dft_tables(l):
    assert 2 * l == DFT_NA * DFT_NB
    n = DFT_NA * DFT_NB
    i = jnp.arange(DFT_NA, dtype=jnp.int32)
    ang1 = (2.0 * math.pi / DFT_NA) * ((i[:, None] * i[None, :]) % DFT_NA).astype(F32)
    c1, s1 = jnp.cos(ang1), jnp.sin(ang1)
    half = DFT_NA // 2
    kk = i[:, None, None] + DFT_NA * i[None, :, None]
    ang = (2.0 * math.pi / n) * ((kk * i[None, None, :]) % n).astype(F32)
    gr, gi = jnp.cos(ang), -jnp.sin(ang)
    t = dict(
        f=jnp.concatenate([c1[:, :half], -s1[:, :half]], axis=0),
        gr=gr, gi=gi,
        gg=jnp.concatenate([gr, gi], axis=1),
        hh=jnp.concatenate([gr.transpose(0, 2, 1), gi.transpose(0, 2, 1)], axis=1),
        er=c1.T[:half] / n, ei=s1.T[:half] / n,
    )
    for name in ('f', 'gg', 'hh', 'er', 'ei'):
        t[name + '_bf'] = t[name].astype(BF16)
    return t


def _dft1_kernel(prec, x_ref, f_ref, ar_ref, ai_ref):
    x = x_ref[...]
    f = f_ref[...]
    if prec is None:
        a = jnp.dot(f, x.astype(BF16), preferred_element_type=F32)
    else:
        a = jnp.dot(f, x, precision=prec, preferred_element_type=F32)
    ar_ref[...] = a[:DFT_NA].astype(ar_ref.dtype)
    ai_ref[...] = a[DFT_NA:].astype(ai_ref.dtype)


def _dft_stage1(x, f, out_dtype, prec):
    g, r, w = x.shape
    spec_o = pl.BlockSpec((None, DFT_NA, HY_LBLK), lambda gi, j: (gi, 0, j))
    return pl.pallas_call(
        functools.partial(_dft1_kernel, prec),
        out_shape=[jax.ShapeDtypeStruct((g, DFT_NA, w), out_dtype)] * 2,
        grid=(g, w // HY_LBLK),
        in_specs=[pl.BlockSpec((None, r, HY_LBLK), lambda gi, j: (gi, 0, j)),
                  pl.BlockSpec((2 * DFT_NA, r), lambda gi, j: (0, 0))],
        out_specs=[spec_o, spec_o],
        compiler_params=_cparams(("parallel", "parallel")),
        name="hyena_dft_stage1",
    )(x, f)


def _filter_mlp_kernel(l, feats_ref, w1_ref, b1_ref, w2_ref, b2_ref, w3_ref, b3_ref, fq_ref, dl_ref, h_ref, nrm_ref):
    i = pl.program_id(0)
    tm = feats_ref.shape[0]
    dot = functools.partial(jnp.dot, precision=HIGHEST, preferred_element_type=F32)
    h = jnp.sin(fq_ref[0:1] * (dot(feats_ref[...], w1_ref[...]) + b1_ref[...]))
    h = jnp.sin(fq_ref[1:2] * (dot(h, w2_ref[...]) + b2_ref[...]))
    h = dot(h, w3_ref[...]) + b3_ref[...]
    row = i * tm + lax.broadcasted_iota(jnp.int32, (tm, 1), 0)
    t = row.astype(F32) * (1.0 / (l - 1))
    decay = jnp.exp(-t * dl_ref[...])
    h = h * jnp.concatenate([decay] * (2 * HY_ORDER), axis=1)
    col = lax.broadcasted_iota(jnp.int32, (1, 2 * HY_ORDER * HY_W), 1)
    bwd = jnp.bitwise_and(lax.shift_right_logical(col, HY_W.bit_length() - 1), 1) == 1
    h = jnp.where(jnp.logical_and(row == 0, bwd), 0.0, h)
    h_ref[...] = h

    @pl.when(i == 0)
    def _():
        nrm_ref[...] = jnp.zeros_like(nrm_ref)

    nrm_ref[...] += jnp.sum(jnp.abs(h), axis=0, keepdims=True)


def _filter_mlp(feats, w1, b1, w2, b2, w3, b3, freq, deltas, tm=1024):
    l = feats.shape[0]
    wcols = 2 * HY_ORDER * HY_W
    full = lambda a: pl.BlockSpec(a.shape, lambda i: (0,) * a.ndim)
    args = (feats, w1, b1.reshape(1, -1), w2, b2.reshape(1, -1), w3, b3.reshape(1, -1), freq, deltas.reshape(1, -1))
    return pl.pallas_call(
        functools.partial(_filter_mlp_kernel, l),
        out_shape=[jax.ShapeDtypeStruct((l, wcols), F32), jax.ShapeDtypeStruct((1, wcols), F32)],
        grid=(l // tm,),
        in_specs=[pl.BlockSpec((tm, feats.shape[1]), lambda i: (i, 0))] + [full(a) for a in args[1:]],
        out_specs=[pl.BlockSpec((tm, wcols), lambda i: (i, 0)), pl.BlockSpec((1, wcols), lambda i: (0, 0))],
        compiler_params=_cparams(("arbitrary",)),
        name="hyena_filter_mlp",
    )(*args)


def _filter_spec_kernel(afr_ref, afi_ref, abr_ref, abi_ref, gr_ref, gi_ref, nf_ref, nb_ref, kr_ref, ki_ref):
    dot = functools.partial(jnp.dot, precision=HIGHEST, preferred_element_type=F32)
    inv = 1.0 / (nf_ref[...] + nb_ref[...])
    for j in range(gr_ref.shape[0]):
        gr, gi = gr_ref[j], gi_ref[j]
        sr = afr_ref[j] + abr_ref[j]
        si = afi_ref[j] + abi_ref[j]
        dr = afr_ref[j] - abr_ref[j]
        di = afi_ref[j] - abi_ref[j]
        kr_ref[j] = (dot(gr, sr) - dot(gi, si)) * inv
        ki_ref[j] = (dot(gr, di) + dot(gi, dr)) * inv


def _filter_spectrum(ar, ai, gr, gi, nrm, kb=4, cw=256):
    ncw = HY_W // cw

    def a_spec(d):
        return pl.BlockSpec((kb, DFT_NB, cw), lambda o, ki, ci: (ki, 0, (2 * o + d) * ncw + ci))

    def n_spec(d):
        return pl.BlockSpec((1, cw), lambda o, ki, ci: (0, (2 * o + d) * ncw + ci))

    g_spec = pl.BlockSpec((kb, DFT_NB, DFT_NB), lambda o, ki, ci: (ki, 0, 0))
    o_spec = pl.BlockSpec((None, kb, DFT_NB, cw), lambda o, ki, ci: (o, ki, 0, ci))
    return pl.pallas_call(
        _filter_spec_kernel,
        out_shape=[jax.ShapeDtypeStruct((HY_ORDER, DFT_NA, DFT_NB, HY_W), F32)] * 2,
        grid=(HY_ORDER, DFT_NA // kb, ncw),
        in_specs=[a_spec(0), a_spec(0), a_spec(1), a_spec(1), g_spec, g_spec, n_spec(0), n_spec(1)],
        out_specs=[o_spec, o_spec],
        compiler_params=_cparams(("parallel", "parallel", "parallel")),
        name="hyena_filter_spectrum",
    )(ar, ai, ar, ai, gr, gi, nrm, nrm)


def _hy_mid_kernel(ar_ref, ai_ref, gg_ref, hh_ref, kr_ref, ki_ref, br_ref, bi_ref):
    for j in range(HY_KB):
        gg = gg_ref[j]
        p = jnp.dot(gg, ar_ref[j], preferred_element_type=F32)
        q = jnp.dot(gg, ai_ref[j], preferred_element_type=F32)
        xr = p[:DFT_NB] - q[DFT_NB:]
        xi = q[:DFT_NB] + p[DFT_NB:]
        kr, ki = kr_ref[j], ki_ref[j]
        yr = (xr * kr - xi * ki).astype(BF16)
        yi = (xr * ki + xi * kr).astype(BF16)
        hh = hh_ref[j]
        p = jnp.dot(hh, yr, preferred_element_type=F32)
        q = jnp.dot(hh, yi, preferred_element_type=F32)
        br_ref[j] = (p[:DFT_NB] + q[DFT_NB:]).astype(BF16)
        bi_ref[j] = (q[:DFT_NB] - p[DFT_NB:]).astype(BF16)


def _hy_mid(ar, ai, gg, hh, kr, ki):
    b = ar.shape[0]
    a_spec = pl.BlockSpec((None, HY_KB, DFT_NB, HY_W), lambda ki_, bi: (bi, ki_, 0, 0))
    t_spec = pl.BlockSpec((HY_KB, 2 * DFT_NB, DFT_NB), lambda ki_, bi: (ki_, 0, 0))
    k_spec = pl.BlockSpec((HY_KB, DFT_NB, HY_W), lambda ki_, bi: (ki_, 0, 0))
    return pl.pallas_call(
        _hy_mid_kernel,
        out_shape=[jax.ShapeDtypeStruct(ar.shape, BF16)] * 2,
        grid=(DFT_NA // HY_KB, b),
        in_specs=[a_spec, a_spec, t_spec, t_spec, k_spec, k_spec],
        out_specs=[a_spec, a_spec],
        compiler_params=_cparams(("parallel", "arbitrary")),
        name="hyena_dft_mid",
    )(ar, ai, gg, hh, kr, ki)


def _hy_out_kernel(br_ref, bi_ref, er_ref, ei_ref, u_ref, gate_ref, skip_ref, o_ref):
    y = (jnp.dot(er_ref[...], br_ref[...], preferred_element_type=F32)
         - jnp.dot(ei_ref[...], bi_ref[...], preferred_element_type=F32))
    o_ref[...] = gate_ref[...] * (y + u_ref[...] * skip_ref[...])


def _hy_out(br, bi, er, ei, u, gate, skip_row):
    b, r, w = u.shape
    row = pl.BlockSpec((None, r, HY_LBLK), lambda bi_, j: (bi_, 0, j))
    bspec = pl.BlockSpec((None, DFT_NA, HY_LBLK), lambda bi_, j: (bi_, 0, j))
    espec = pl.BlockSpec((r, DFT_NA), lambda bi_, j: (0, 0))
    return pl.pallas_call(
        _hy_out_kernel,
        out_shape=jax.ShapeDtypeStruct(u.shape, F32),
        grid=(b, w // HY_LBLK),
        in_specs=[bspec, bspec, espec, espec, row, row, pl.BlockSpec((1, HY_LBLK), lambda bi_, j: (0, 0))],
        out_specs=row,
        compiler_params=_cparams(("parallel", "parallel")),
        name="hyena_dft_out",
    )(br, bi, er, ei, u, gate, skip_row)


def _short_conv_kernel(z_ref, w_ref, b_ref, o_ref):
    z = z_ref[...]
    l = z.shape[0]
    row = lax.broadcasted_iota(jnp.int32, (l, 1), 0)
    prev = jnp.where(row == 0, 0.0, pltpu.roll(z, 1, 0))
    nxt = jnp.where(row == l - 1, 0.0, pltpu.roll(z, l - 1, 0))
    w = w_ref[...]
    o_ref[...] = prev * w[0:1] + z * w[1:2] + nxt * w[2:3] + b_ref[...]


def _short_conv(z, w, bias, b, l):
    nc = HY_W // LANE
    return pl.pallas_call(
        _short_conv_kernel,
        out_shape=jax.ShapeDtypeStruct((3, b * l, HY_W), F32),
        grid=(b, 3, nc),
        in_specs=[pl.BlockSpec((l, LANE), lambda bi, p, ci: (bi, p * nc + ci)),
                  pl.BlockSpec((3, LANE), lambda bi, p, ci: (0, p * nc + ci)),
                  pl.BlockSpec((1, LANE), lambda bi, p, ci: (0, p * nc + ci))],
        out_specs=pl.BlockSpec((None, l, LANE), lambda bi, p, ci: (p, bi, ci)),
        compiler_params=_cparams(("parallel", "parallel", "parallel")),
        name="hyena_short_conv",
    )(z, w, bias.reshape(1, -1))


def _hyena_filter_spectrum(l, w1, b1, w2, b2, w3, b3, freq, tabs):
    t = jnp.linspace(0.0, 1.0, l, dtype=F32)[:, None]
    w = 2.0 * math.pi * jnp.arange(l, dtype=F32)[:, None] / l
    fr = jnp.linspace(1e-4, HY_BANDS - 1, HY_BANDS, dtype=F32)[None, :]
    feats = jnp.concatenate([t, jnp.cos(fr * w), -jnp.sin(fr * w)], axis=-1)
    deltas = jnp.abs(jnp.linspace(HY_MIN_DECAY, HY_MAX_DECAY, HY_W, dtype=F32))
    h, nrm = _filter_mlp(feats, w1, b1, w2, b2, w3, b3, freq, deltas)
    wcols = h.shape[1]
    ar, ai = _dft_stage1(h.reshape(1, l // DFT_NB, DFT_NB * wcols), tabs['f'], F32, HIGHEST)
    return _filter_spectrum(ar.reshape(DFT_NA, DFT_NB, wcols), ai.reshape(DFT_NA, DFT_NB, wcols),
                            tabs['gr'], tabs['gi'], nrm)


def _hyena(z, short_w, short_b, skip, kr, ki, tabs, b, l):
    u = _short_conv(z, short_w, short_b, b, l)
    rows = l // DFT_NB
    view = lambda a: a.reshape(b, rows, DFT_NB * HY_W)
    v, gates = view(u[0]), (view(u[1]), view(u[2]))
    for o in range(HY_ORDER):
        ar, ai = _dft_stage1(v, tabs['f_bf'], BF16, None)
        shp4 = (b, DFT_NA, DFT_NB, HY_W)
        br, bi = _hy_mid(ar.reshape(shp4), ai.reshape(shp4), tabs['gg_bf'], tabs['hh_bf'], kr[o], ki[o])
        shp3 = (b, DFT_NA, DFT_NB * HY_W)
        skip_row = jnp.tile(skip[o], HY_LBLK // HY_W).reshape(1, HY_LBLK)
        v = _hy_out(br.reshape(shp3), bi.reshape(shp3), tabs['er_bf'], tabs['ei_bf'], v, gates[o], skip_row)
    return v.reshape(b * l, HY_W)


HG_REC_CHUNK = 64
HG_SUB = 16
HG_TILE = 512
HG_HEADS_PER_STEP = 2
HG_UNROLL = 1


def _gate_terms(z, lb):
    a = jnp.log(lb)
    b = jnp.log1p(-lb) + (jnp.minimum(z, 0.0) - jnp.log1p(jnp.exp(-jnp.abs(z))))
    log_f = jnp.maximum(a, b) + jnp.log1p(jnp.exp(-jnp.abs(a - b)))
    return log_f, (1.0 - lb) * jax.nn.sigmoid(-z)


SUBLANES = 8


def _running_sum_rows(x, fwd):
    r = x.shape[0]
    row = jnp.bitwise_and(lax.broadcasted_iota(jnp.int32, (r, 1), 0), SUBLANES - 1)
    s = 1
    while s < SUBLANES:
        if fwd:
            x = x + jnp.where(row >= s, pltpu.roll(x, s, 0), 0.0)
        else:
            x = x + jnp.where(row < SUBLANES - s, pltpu.roll(x, r - s, 0), 0.0)
        s *= 2
    groups = [x[i:i + SUBLANES] for i in range(0, r, SUBLANES)]
    order = range(1, len(groups)) if fwd else range(len(groups) - 2, -1, -1)
    for i in order:
        prev = groups[i - 1][SUBLANES - 1:SUBLANES] if fwd else groups[i + 1][0:1]
        groups[i] = groups[i] + prev
    return jnp.concatenate(groups, axis=0)


def _gla_chunk(q, k, v, g, st, fwd):
    ch = q.shape[0]
    b = _running_sum_rows(g, fwd)
    nsb = ch // HG_SUB
    parts = [None] * nsb

    def add(i, val):
        parts[i] = val if parts[i] is None else parts[i] + val

    vb = v.astype(BF16)
    li = lax.broadcasted_iota(jnp.int32, (ch, HG_SUB), 1)
    si = jnp.bitwise_and(lax.broadcasted_iota(jnp.int32, (ch, HG_SUB), 0), HG_SUB - 1)
    keep = (li <= si) if fwd else (li >= si)
    a = jnp.zeros((ch, HG_SUB), F32)
    for s in range(HG_SUB):
        src = [i * HG_SUB + s for i in range(nsb)]
        bs = jnp.concatenate([jnp.broadcast_to(b[r:r + 1], (HG_SUB, LANE)) for r in src], axis=0)
        ks = jnp.concatenate([jnp.broadcast_to(k[r:r + 1], (HG_SUB, LANE)) for r in src], axis=0)
        e = jnp.exp(jnp.minimum(b - bs, 0.0))
        a = jnp.where(li == s, jnp.sum(q * e * ks, axis=-1, keepdims=True), a)
    a = jnp.where(keep, a, 0.0).astype(BF16)
    for i in range(nsb):
        sl = slice(i * HG_SUB, (i + 1) * HG_SUB)
        add(i, jnp.dot(a[sl], vb[sl], preferred_element_type=F32))
    h = HG_SUB
    while h < ch:
        for j in range(ch // (2 * h)):
            lo = slice(2 * h * j, 2 * h * j + h)
            hi = slice(2 * h * j + h, 2 * h * j + 2 * h)
            if fwd:
                rows, cols, ref = hi, lo, b[2 * h * j + h - 1:2 * h * j + h]
            else:
                rows, cols, ref = lo, hi, b[2 * h * j + h:2 * h * j + h + 1]
            qt = (q[rows] * jnp.exp(b[rows] - ref)).astype(BF16)
            kt = (k[cols] * jnp.exp(ref - b[cols])).astype(BF16)
            a = lax.dot_general(qt, kt, (((1,), (1,)), ((), ())), preferred_element_type=F32)
            ov = jnp.dot(a.astype(BF16), vb[cols], preferred_element_type=F32)
            for ii in range(h // HG_SUB):
                add(rows.start // HG_SUB + ii, ov[ii * HG_SUB:(ii + 1) * HG_SUB])
        h *= 2
    o = jnp.concatenate(parts, axis=0)
    qh = (q * jnp.exp(b)).astype(BF16)
    o = o + lax.dot_general(qh, st.astype(BF16), (((1,), (1,)), ((), ())), preferred_element_type=F32)
    btot = b[ch - 1:ch] if fwd else b[0:1]
    kh = (k * jnp.exp(btot - b)).astype(BF16)
    st_new = st * jnp.exp(btot) + lax.dot_general(vb, kh, (((0,), (0,)), ((), ())), preferred_element_type=F32)
    return o, st_new


def _hgrn2_kernel(lbf_ref, lbb_ref, ng_ref, zqf_ref, zif_ref, zff_ref, zgf_ref, zqb_ref, zib_ref, zbb_ref, zgb_ref,
                  o_ref, st_ref):
    c = pl.program_id(2)
    n = pl.num_programs(2)
    nch = HG_TILE // HG_REC_CHUNK

    @pl.when(c == 0)
    def _():
        st_ref[...] = jnp.zeros_like(st_ref)

    def body(second_pass, i, carry):
        off_f = pl.multiple_of(i * HG_REC_CHUNK, HG_REC_CHUNK)
        off_b = pl.multiple_of((nch - 1 - i) * HG_REC_CHUNK, HG_REC_CHUNK)
        for fwd, off, zq_ref, zi_ref, zd_ref, zg_ref, lb_ref, tile in (
                (True, off_f, zqf_ref, zif_ref, zff_ref, zgf_ref, lbf_ref, c),
                (False, off_b, zqb_ref, zib_ref, zbb_ref, zgb_ref, lbb_ref, n - 1 - c)):
            sl = pl.ds(off, HG_REC_CHUNK)
            rows = pl.ds(pl.multiple_of(tile * HG_TILE + off, HG_REC_CHUNK), HG_REC_CHUNK)
            d = 0 if fwd else 1
            for hh in range(HG_HEADS_PER_STEP):
                ln = slice(hh * LANE, (hh + 1) * LANE)
                g, k = _gate_terms(zd_ref[sl, ln], lb_ref[:, ln])
                o, st_new = _gla_chunk(jax.nn.silu(zq_ref[sl, ln]), k, zi_ref[sl, ln], g, st_ref[d, hh], fwd)
                st_ref[d, hh] = st_new
                if second_pass:
                    tot = o_ref[rows, ln] + o
                    tot = tot * lax.rsqrt(jnp.mean(tot * tot, axis=-1, keepdims=True) + 1e-6) * ng_ref[...]
                    o_ref[rows, ln] = tot * jax.nn.silu(zg_ref[sl, ln])
                else:
                    o_ref[rows, ln] = o
        return carry

    @pl.when(c < n // 2)
    def _():
        lax.fori_loop(0, nch, functools.partial(body, False), 0, unroll=HG_UNROLL)

    @pl.when(c >= n // 2)
    def _():
        lax.fori_loop(0, nch, functools.partial(body, True), 0, unroll=HG_UNROLL)


def _hgrn2(z, lb_fwd, lb_bwd, norm_g, b, l):
    n = l // HG_TILE
    assert n % 2 == 0
    w = HG_HEADS_PER_STEP * LANE
    base = HY_COLS // w
    nh = BRANCH_W // w

    def zspec(part, rev):
        if rev:
            return pl.BlockSpec((HG_TILE, w), lambda bi, hi, ci: (bi * n + n - 1 - ci, base + part * nh + hi))
        return pl.BlockSpec((HG_TILE, w), lambda bi, hi, ci: (bi * n + ci, base + part * nh + hi))

    lbspec = pl.BlockSpec((1, w), lambda bi, hi, ci: (0, hi))
    return pl.pallas_call(
        _hgrn2_kernel,
        out_shape=jax.ShapeDtypeStruct((b * l, BRANCH_W), F32),
        grid=(b, nh, n),
        in_specs=[lbspec, lbspec, pl.BlockSpec((1, LANE), lambda bi, hi, ci: (0, 0)),
                  zspec(0, False), zspec(1, False), zspec(2, False), zspec(4, False),
                  zspec(0, True), zspec(1, True), zspec(3, True), zspec(4, True)],
        out_specs=pl.BlockSpec((l, w), lambda bi, hi, ci: (bi, hi)),
        scratch_shapes=[pltpu.VMEM((2, HG_HEADS_PER_STEP, HG_DV, HG_DK), F32)],
        compiler_params=_cparams(("parallel", "parallel", "arbitrary")),
        name="hgrn2_scan",
    )(lb_fwd.reshape(1, BRANCH_W), lb_bwd.reshape(1, BRANCH_W), norm_g.reshape(1, HG_DV),
      z, z, z, z, z, z, z, z)


def _merge_kernel(x_ref, o0_ref, o1_ref, o2_ref, o3_ref, wg_ref, bg_ref, wbr_ref, wout_ref, g_ref, b_ref,
                  out_ref, acc_ref, xb_ref, ob_ref):
    j = pl.program_id(1)

    @pl.when(j == 0)
    def _():
        xb_ref[...] = x_ref[...].astype(BF16)
        for bi, o_ref in enumerate((o0_ref, o1_ref, o2_ref, o3_ref)):
            ob_ref[bi] = o_ref[...].astype(BF16)
        acc_ref[...] = jnp.zeros_like(acc_ref)

    xb = xb_ref[...]
    merged = None
    for bi in range(N_BRANCH):
        gate = jax.nn.sigmoid(jnp.dot(xb, wg_ref[bi], preferred_element_type=F32) + bg_ref[bi])
        term = gate * jnp.dot(ob_ref[bi], wbr_ref[bi], preferred_element_type=F32)
        merged = term if merged is None else merged + term
    acc_ref[...] += jnp.dot(merged.astype(BF16), wout_ref[...], preferred_element_type=F32)

    @pl.when(j == pl.num_programs(1) - 1)
    def _():
        out_ref[...] = _ln_rows(DN_ALPHA * x_ref[...] + acc_ref[...], g_ref[...], b_ref[...])


def _merge(x, branches, w_gate, b_gate, w_br, w_out, ln_g, ln_b, tm=512, tn=256):
    t, d = x.shape
    row = pl.BlockSpec((tm, d), lambda i, j: (i, 0))
    brow = pl.BlockSpec((tm, BRANCH_W), lambda i, j: (i, 0))
    vec = pl.BlockSpec((1, d), lambda i, j: (0, 0))
    return pl.pallas_call(
        _merge_kernel,
        out_shape=jax.ShapeDtypeStruct((t, d), F32),
        grid=(t // tm, d // tn),
        in_specs=[
            row, brow, brow, brow, brow,
            pl.BlockSpec((N_BRANCH, d, tn), lambda i, j: (0, 0, j)),
            pl.BlockSpec((N_BRANCH, 1, tn), lambda i, j: (0, 0, j)),
            pl.BlockSpec((N_BRANCH, BRANCH_W, tn), lambda i, j: (0, 0, j)),
            pl.BlockSpec((tn, d), lambda i, j: (j, 0)),
            vec, vec,
        ],
        out_specs=row,
        scratch_shapes=[pltpu.VMEM((tm, d), F32), pltpu.VMEM((tm, d), BF16),
                        pltpu.VMEM((N_BRANCH, tm, BRANCH_W), BF16)],
        compiler_params=_cparams(("parallel", "arbitrary")),
        name="gated_merge",
    )(x, *branches, w_gate, b_gate.reshape(N_BRANCH, 1, d), w_br, w_out, ln_g.reshape(1, d), ln_b.reshape(1, d))


def _expert_kernel(x_ref, gate_ref, w1_ref, w3_ref, w2_ref, o_ref):
    f = pl.program_id(2)
    x = x_ref[...]
    h1 = jnp.dot(x, w1_ref[...], preferred_element_type=F32)
    h3 = jnp.dot(x, w3_ref[...], preferred_element_type=F32)
    h = (jax.nn.silu(h1) * h3).astype(BF16)
    part = jnp.dot(h, w2_ref[...], preferred_element_type=F32)

    @pl.when(f == 0)
    def _():
        o_ref[...] = part

    @pl.when(f != 0)
    def _():
        o_ref[...] += part

    @pl.when(f == pl.num_programs(2) - 1)
    def _():
        o_ref[...] = o_ref[...] * gate_ref[...]


def _expert_ffn(xe, gate, w1, w3, w2, tm=1024, tf=512):
    e, c, d = xe.shape
    ff = w1.shape[2]
    tm = min(tm, c)
    return pl.pallas_call(
        _expert_kernel,
        out_shape=jax.ShapeDtypeStruct((e, c, d), F32),
        grid=(e, c // tm, ff // tf),
        in_specs=[
            pl.BlockSpec((None, tm, d), lambda ei, i, f: (ei, i, 0)),
            pl.BlockSpec((None, tm, 1), lambda ei, i, f: (ei, i, 0)),
            pl.BlockSpec((None, d, tf), lambda ei, i, f: (ei, 0, f)),
            pl.BlockSpec((None, d, tf), lambda ei, i, f: (ei, 0, f)),
            pl.BlockSpec((None, tf, d), lambda ei, i, f: (ei, f, 0)),
        ],
        out_specs=pl.BlockSpec((None, tm, d), lambda ei, i, f: (ei, i, 0)),
        compiler_params=_cparams(("parallel", "parallel", "arbitrary")),
        name="expert_ffn",
    )(xe, gate, w1, w3, w2)


RT_TM = 1024
RT_WIN = 32
RT_EAGER = 2
RT_CHUNKS = 4


def _router_kernel(x_ref, w_ref, o_ref):
    logits = lax.dot_general(w_ref[...], x_ref[...].astype(BF16), (((1,), (1,)), ((), ())),
                             preferred_element_type=F32)
    e = jnp.exp(logits - jnp.max(logits, axis=0, keepdims=True))
    aff = e / jnp.sum(e, axis=0, keepdims=True)
    for c in range(x_ref.shape[0] // LANE):
        o_ref[:, c, :] = aff[:, c * LANE:(c + 1) * LANE]


def _router(x, w_t):
    t, d = x.shape
    return pl.pallas_call(
        _router_kernel,
        out_shape=jax.ShapeDtypeStruct((N_EXPERTS, t // LANE, LANE), F32),
        grid=(t // RT_TM,),
        in_specs=[pl.BlockSpec((RT_TM, d), lambda i: (i, 0)), pl.BlockSpec((N_EXPERTS, d), lambda i: (0, 0))],
        out_specs=pl.BlockSpec((N_EXPERTS, RT_TM // LANE, LANE), lambda i: (0, i, 0)),
        compiler_params=_cparams(("parallel",)),
        name="moe_router",
    )(x, w_t)


def _prefix_count(mask):
    nch = mask.shape[0]
    m = jnp.where(mask, 1.0, 0.0)
    li = lax.broadcasted_iota(jnp.int32, (LANE, LANE), 0)
    lj = lax.broadcasted_iota(jnp.int32, (LANE, LANE), 1)
    incl = jnp.dot(m.astype(BF16), jnp.where(li <= lj, 1.0, 0.0).astype(BF16), preferred_element_type=F32)
    tot = jnp.broadcast_to(incl[:, LANE - 1:LANE], (nch, LANE))
    ri = lax.broadcasted_iota(jnp.int32, (nch, nch), 0)
    ci = lax.broadcasted_iota(jnp.int32, (nch, nch), 1)
    start = jnp.dot(jnp.where(ci < ri, 1.0, 0.0).astype(BF16), tot.astype(BF16), preferred_element_type=F32)
    return incl - m + start, start


def _route_select_kernel(cap, a_ref, pos_ref, start_ref):
    bits = pltpu.bitcast(a_ref[...], jnp.int32)

    def step(i, tau):
        cand = jnp.bitwise_or(tau, lax.shift_left(jnp.int32(1), 30 - i))
        cnt = jnp.sum(jnp.where(bits >= cand, 1.0, 0.0))
        return jnp.where(cnt >= cap, cand, tau)

    tau = lax.fori_loop(0, 31, step, jnp.int32(0))
    gt = bits > tau
    eq = bits == tau
    need = cap - jnp.sum(jnp.where(gt, 1.0, 0.0))
    eq_rank, _ = _prefix_count(eq)
    sel = jnp.logical_or(gt, jnp.logical_and(eq, eq_rank < need))
    pos, start = _prefix_count(sel)
    pos_ref[...] = jnp.where(sel, pos, -1.0)
    start_ref[...] = start


def _route_select(aff3, cap):
    e, nch, _ = aff3.shape
    spec = pl.BlockSpec((None, nch, LANE), lambda ei: (ei, 0, 0))
    return pl.pallas_call(
        functools.partial(_route_select_kernel, float(cap)),
        out_shape=[jax.ShapeDtypeStruct(aff3.shape, F32)] * 2,
        grid=(e,),
        in_specs=[spec],
        out_specs=[spec, spec],
        compiler_params=_cparams(("parallel",)),
        name="moe_route_select",
    )(aff3)


def _route_compact_kernel(cap, start_sm, pos_ref, a_ref, idx_ref, gate_ref, il_ref, gl_ref):
    ei = pl.program_id(0)
    nch = pos_ref.shape[0]
    il_ref[...] = jnp.zeros_like(il_ref)
    gl_ref[...] = jnp.zeros_like(gl_ref)
    s_iota = lax.broadcasted_iota(jnp.int32, (RT_WIN, LANE), 0).astype(F32)
    lane = lax.broadcasted_iota(jnp.int32, (1, LANE), 1).astype(F32)
    log_win = RT_WIN.bit_length() - 1

    def body(i, carry):
        rest = []
        for u in range(RT_CHUNKS):
            j = i * RT_CHUNKS + u
            first = start_sm[ei, j]
            nxt = jnp.where(j + 1 < nch, start_sm[ei, jnp.minimum(j + 1, nch - 1)], cap)
            base = lax.shift_left(lax.shift_right_logical(first, log_win), log_win)
            nwin = lax.shift_right_logical(nxt - base + (RT_WIN - 1), log_win)
            prow = pos_ref[pl.ds(j, 1), :]
            arow = a_ref[pl.ds(j, 1), :]
            tok = lane + (j * LANE).astype(F32)

            def window(w, c2, base=base, prow=prow, arow=arow, tok=tok):
                lo = pl.multiple_of(base + w * RT_WIN, RT_WIN)
                hit = (prow - lo.astype(F32)) == s_iota
                il_ref[pl.ds(lo, RT_WIN), :] += jnp.sum(jnp.where(hit, tok, 0.0), axis=1, keepdims=True)
                gl_ref[pl.ds(lo, RT_WIN), :] += jnp.sum(jnp.where(hit, arow, 0.0), axis=1, keepdims=True)
                return c2

            for w in range(RT_EAGER):
                window(w, 0)
            rest.append((nwin, window))
        for nwin, window in rest:
            lax.fori_loop(RT_EAGER, nwin, window, 0)
        return carry

    lax.fori_loop(0, nch // RT_CHUNKS, body, 0)
    diag = (lax.broadcasted_iota(jnp.int32, (LANE, LANE), 0) == lax.broadcasted_iota(jnp.int32, (LANE, LANE), 1))
    for r in range(idx_ref.shape[0]):
        rows = slice(r * LANE, (r + 1) * LANE)
        idx_ref[r:r + 1, :] = jnp.sum(jnp.where(diag, il_ref[rows, :], 0.0), axis=0, keepdims=True).astype(jnp.int32)
        gate_ref[r:r + 1, :] = jnp.sum(jnp.where(diag, gl_ref[rows, :], 0.0), axis=0, keepdims=True)


def _route_compact(start, pos3, aff3, cap):
    e, nch, _ = pos3.shape
    spec = pl.BlockSpec((None, nch, LANE), lambda ei, s: (ei, 0, 0))
    ospec = pl.BlockSpec((None, cap // LANE, LANE), lambda ei, s: (ei, 0, 0))
    return pl.pallas_call(
        functools.partial(_route_compact_kernel, cap),
        out_shape=[jax.ShapeDtypeStruct((e, cap // LANE, LANE), jnp.int32),
                   jax.ShapeDtypeStruct((e, cap // LANE, LANE), F32)],
        grid_spec=pltpu.PrefetchScalarGridSpec(
            num_scalar_prefetch=1, grid=(e,), in_specs=[spec, spec], out_specs=[ospec, ospec],
            scratch_shapes=[pltpu.VMEM((cap + 2 * LANE, LANE), F32), pltpu.VMEM((cap + 2 * LANE, LANE), F32)]),
        compiler_params=_cparams(("parallel",)),
        name="moe_route_compact",
    )(start, pos3, aff3)


def _expert_choice(x, w_t, cap):
    aff3 = _router(x, w_t)
    pos3, start3 = _route_select(aff3, cap)
    idx3, gate3 = _route_compact(start3[:, :, 0].astype(jnp.int32), pos3, aff3, cap)
    return idx3.reshape(N_EXPERTS, cap), gate3.reshape(N_EXPERTS, cap)


def _attn_prep_kernel(zq_ref, zk_ref, zv_ref, cos_ref, sin_ref, q_ref, k_ref, v_ref):
    lane = lax.broadcasted_iota(jnp.int32, (1, LANE), 1)
    first = jnp.bitwise_and(lane, DA_HD - 1) < ROPE_DIM // 2
    cos, sin = cos_ref[...], sin_ref[...]

    def rope(x):
        partner = jnp.where(first, pltpu.roll(x, LANE - ROPE_DIM // 2, 1), pltpu.roll(x, ROPE_DIM // 2, 1))
        return x * cos + partner * sin

    q_ref[...] = (rope(zq_ref[...]) * (DA_HD ** -0.5)).astype(BF16)
    k_ref[...] = rope(zk_ref[...]).astype(BF16)
    v_ref[...] = zv_ref[...].astype(BF16)


def _rope_lane_tables(l):
    inv = ROPE_THETA ** (-jnp.arange(0, ROPE_DIM, 2, dtype=F32) / ROPE_DIM)
    ang = jnp.arange(l, dtype=F32)[:, None] * inv[None, :]
    c, s = jnp.cos(ang), jnp.sin(ang)
    rest = DA_HD - ROPE_DIM
    cos_h = jnp.concatenate([c, c, jnp.ones((l, rest), F32)], axis=1)
    sin_h = jnp.concatenate([-s, s, jnp.zeros((l, rest), F32)], axis=1)
    return jnp.concatenate([cos_h, cos_h], axis=1), jnp.concatenate([sin_h, sin_h], axis=1)


def _attn_prep(z, cos, sin, b, l, tm=1024):
    n = l // tm
    base = DA_OFF // LANE

    def zspec(part):
        return pl.BlockSpec((tm, LANE), lambda bi, hi, i: (bi * n + i, base + part * DA_HEADS + hi))

    tspec = pl.BlockSpec((tm, LANE), lambda bi, hi, i: (i, 0))
    ospec = pl.BlockSpec((None, None, tm, LANE), lambda bi, hi, i: (bi, hi, i, 0))
    return pl.pallas_call(
        _attn_prep_kernel,
        out_shape=[jax.ShapeDtypeStruct((b, DA_HEADS, l, DA_VD), BF16)] * 3,
        grid=(b, DA_HEADS, n),
        in_specs=[zspec(0), zspec(1), zspec(2), tspec, tspec],
        out_specs=[ospec, ospec, ospec],
        compiler_params=_cparams(("parallel", "parallel", "parallel")),
        name="attention_prep",
    )(z, z, z, cos, sin)


def _trunk(x, mem, p):
    B, L, D = x.shape
    T = B * L
    cos, sin = _rope_lane_tables(L)
    lb_all = jnp.cumsum(jax.nn.softmax(p['hg_lb_raw'], axis=1), axis=1)
    lb_all = lb_all - lb_all[:, :1]
    xf = _layer_norm(x.reshape(T, D), p['ln_in_g'], p['ln_in_b'])
    memf = mem.reshape(B * mem.shape[1], D)
    cap = EC_FACTOR * T // N_EXPERTS
    for l in range(DEPTH):
        z = _matmul(xf, p['w_in'][l], 1024, 1024)
        hy_kr, hy_ki = p['hy_spectrum'][L][l]
        o_hy = _hyena(z, p['hy_short_w'][l], p['hy_short_b'][l], p['hy_skip'][l], hy_kr, hy_ki,
                      p['dft_tables'][L], B, L)
        o_hg = _hgrn2(z, lb_all[0, l], lb_all[1, l], p['hg_norm_g'][l], B, L)
        qr, kr, vr = _attn_prep(z, cos, sin, B, L)
        lam_init = 0.8 - 0.6 * math.exp(-0.3 * l)
        o_da = _diff_attention(qr, kr, vr, p['da_lambda'][l], p['da_norm_g'][l], lam_init)
        kv = _matmul(memf, p['w_mem_kv'][l], min(512, memf.shape[0]), 512).reshape(B, -1, 2 * BRANCH_W)
        o_me = _memory_attention(z, kv, B, L)
        xf = _merge(xf, (o_hy, o_hg, o_da, o_me), p['w_gate'][l], p['b_gate'][l], p['w_br'][l], p['w_out'][l],
                    p['ln1_g'][l], p['ln1_b'][l])
        idx, gate = _expert_choice(xf, p['w_router_t'][l], cap)
        xe = xf.astype(BF16)[idx]
        ye = _expert_ffn(xe, gate[..., None], p['w_e1'][l], p['w_e3'][l], p['w_e2'][l])
        y = jnp.zeros_like(xf).at[idx.reshape(-1)].add(ye.reshape(-1, D))
        xf = _layer_norm(xf, p['ln2_g'][l], p['ln2_b'][l], resid=y, alpha=DN_ALPHA)
    return xf.reshape(B, L, D)


def kernel(x_prompt, x_sample, mem_prompt, mem_sample, ln_in_g, ln_in_b, w_in, hy_short_w, hy_short_b, hy_ffn_w1, hy_ffn_b1, hy_ffn_w2, hy_ffn_b2, hy_ffn_w3, hy_ffn_b3, hy_freq, hy_skip, hg_lb_raw, hg_norm_g, da_lambda, da_norm_g, w_mem_kv, w_gate, b_gate, w_br, w_out, ln1_g, ln1_b, w_router, w_e1, w_e3, w_e2, ln2_g, ln2_b):
    p = dict(ln_in_g=ln_in_g, ln_in_b=ln_in_b, hy_short_w=hy_short_w, hy_short_b=hy_short_b,
             hy_ffn_w1=hy_ffn_w1, hy_ffn_b1=hy_ffn_b1, hy_ffn_w2=hy_ffn_w2, hy_ffn_b2=hy_ffn_b2,
             hy_ffn_w3=hy_ffn_w3, hy_ffn_b3=hy_ffn_b3, hy_freq=hy_freq, hy_skip=hy_skip, hg_lb_raw=hg_lb_raw,
             hg_norm_g=hg_norm_g, da_lambda=da_lambda, da_norm_g=da_norm_g, b_gate=b_gate,
             ln1_g=ln1_g, ln1_b=ln1_b, ln2_g=ln2_g, ln2_b=ln2_b)
    for name, w in (('w_in', w_in), ('w_mem_kv', w_mem_kv), ('w_gate', w_gate), ('w_br', w_br), ('w_out', w_out),
                    ('w_e1', w_e1), ('w_e3', w_e3), ('w_e2', w_e2)):
        p[name] = w.astype(BF16)
    p['w_router_t'] = jnp.swapaxes(w_router, 1, 2).astype(BF16)
    p['dft_tables'], p['hy_spectrum'] = {}, {}
    for seq_len in sorted({x_prompt.shape[1], x_sample.shape[1]}):
        tabs = _dft_tables(seq_len)
        p['dft_tables'][seq_len] = tabs
        p['hy_spectrum'][seq_len] = [
            _hyena_filter_spectrum(seq_len, hy_ffn_w1[l], hy_ffn_b1[l], hy_ffn_w2[l], hy_ffn_b2[l], hy_ffn_w3[l],
                                   hy_ffn_b3[l], hy_freq[l], tabs) for l in range(DEPTH)]
    y_prompt = _trunk(x_prompt, mem_prompt, p)
    y_sample = _trunk(x_sample, mem_sample, p)
    return (y_prompt, y_sample)
```

```python
import functools
import math

import jax
import jax.numpy as jnp
from jax import lax
from jax.experimental import pallas as pl
from jax.experimental.pallas import tpu as pltpu

F32 = jnp.float32
BF16 = jnp.bfloat16

D_MODEL = 2048
DEPTH = 4
N_BRANCH = 4
BRANCH_W = D_MODEL // 4
HY_W = BRANCH_W
HY_ORDER = 2
HY_EMB = 33
HY_BANDS = (HY_EMB - 1) // 2
HY_MIN_DECAY = math.log(1e-2) / 1.5
HY_MAX_DECAY = math.log(1e-2) / 0.3
HG_HEADS = 4
HG_DK = BRANCH_W // HG_HEADS
HG_DV = BRANCH_W // HG_HEADS
HG_CHUNK = 64
DA_HEADS = 4
DA_HD = BRANCH_W // (2 * DA_HEADS)
DA_VD = 2 * DA_HD
ROPE_DIM = DA_HD // 4
ROPE_THETA = 500000.0
MEM_HEADS = 4
MEM_HD = BRANCH_W // MEM_HEADS
N_EXPERTS = 16
EC_FACTOR = 2
EXPERT_FF = 2048
DN_ALPHA = (2 * DEPTH) ** 0.25
LN_EPS = 1e-5
HY_COLS = 3 * HY_W
HG_COLS = 5 * BRANCH_W
DA_COLS = 3 * BRANCH_W
ME_COLS = BRANCH_W
IN_W = HY_COLS + HG_COLS + DA_COLS + ME_COLS
DA_OFF = HY_COLS + HG_COLS
ME_OFF = DA_OFF + DA_COLS

V7X_VMEM_LIMIT_BYTES = 56 * 1024 * 1024
LANE = 128


def _cparams(sem):
    return pltpu.CompilerParams(dimension_semantics=sem, vmem_limit_bytes=V7X_VMEM_LIMIT_BYTES)


def _mm_kernel(a_ref, w_ref, o_ref):
    o_ref[...] = jnp.dot(a_ref[...].astype(BF16), w_ref[...], preferred_element_type=F32)


def _matmul(a, w, tm, tn):
    m, k = a.shape
    n = w.shape[1]
    return pl.pallas_call(
        _mm_kernel,
        out_shape=jax.ShapeDtypeStruct((m, n), F32),
        grid=(m // tm, n // tn),
        in_specs=[pl.BlockSpec((tm, k), lambda i, j: (i, 0)), pl.BlockSpec((k, tn), lambda i, j: (0, j))],
        out_specs=pl.BlockSpec((tm, tn), lambda i, j: (i, j)),
        compiler_params=_cparams(("parallel", "arbitrary")),
        name="dense_matmul",
    )(a, w)


def _ln_rows(y, g, b):
    mu = jnp.mean(y, axis=-1, keepdims=True)
    d = y - mu
    var = jnp.mean(d * d, axis=-1, keepdims=True)
    return d * lax.rsqrt(var + LN_EPS) * g + b


def _ln_kernel(alpha, x_ref, r_ref, g_ref, b_ref, o_ref):
    y = x_ref[...]
    if r_ref is not None:
        y = alpha * y + r_ref[...]
    o_ref[...] = _ln_rows(y, g_ref[...], b_ref[...])


def _layer_norm(x, g, b, resid=None, alpha=1.0, tm=512):
    t, d = x.shape
    row = pl.BlockSpec((tm, d), lambda i: (i, 0))
    vec = pl.BlockSpec((1, d), lambda i: (0, 0))
    if resid is None:
        kern = lambda x_ref, g_ref, b_ref, o_ref: _ln_kernel(alpha, x_ref, None, g_ref, b_ref, o_ref)
        args, specs = (x, g.reshape(1, d), b.reshape(1, d)), [row, vec, vec]
    else:
        kern = functools.partial(_ln_kernel, alpha)
        args, specs = (x, resid, g.reshape(1, d), b.reshape(1, d)), [row, row, vec, vec]
    return pl.pallas_call(
        kern,
        out_shape=jax.ShapeDtypeStruct((t, d), F32),
        grid=(t // tm,),
        in_specs=specs,
        out_specs=row,
        compiler_params=_cparams(("parallel",)),
        name="layer_norm",
    )(*args)


def _da_kernel(lam_init, tk, lp_ref, g_ref, q_ref, k_ref, v_ref, o_ref):
    tq = q_ref.shape[0]
    nk = k_ref.shape[0] // tk
    q = q_ref[...]
    qs = (q[:, :DA_HD], q[:, DA_HD:])

    def body(i, carry):
        off = pl.multiple_of(i * tk, tk)
        k = k_ref[pl.ds(off, tk), :]
        v = v_ref[pl.ds(off, tk), :]
        out = []
        for c in range(2):
            m_prev, l_prev, acc_prev = carry[c]
            s = lax.dot_general(qs[c], k[:, c * DA_HD:(c + 1) * DA_HD], (((1,), (1,)), ((), ())),
                                preferred_element_type=F32)
            m_new = jnp.maximum(m_prev, jnp.max(s, axis=-1, keepdims=True))
            a = jnp.exp(m_prev - m_new)
            p = jnp.exp(s - m_new)
            l_new = a * l_prev + jnp.sum(p, axis=-1, keepdims=True)
            acc_new = a * acc_prev + jnp.dot(p.astype(BF16), v, preferred_element_type=F32)
            out.append((m_new, l_new, acc_new))
        return tuple(out)

    init = tuple((jnp.full((tq, 1), -jnp.inf, F32), jnp.zeros((tq, 1), F32), jnp.zeros((tq, DA_VD), F32))
                 for _ in range(2))
    (_, l0, a0), (_, l1, a1) = lax.fori_loop(0, nk, body, init)
    lp = lp_ref[...]
    lam = (jnp.exp(jnp.sum(lp[0:1] * lp[1:2], axis=-1, keepdims=True))
           - jnp.exp(jnp.sum(lp[2:3] * lp[3:4], axis=-1, keepdims=True)) + lam_init)
    o = a0 / l0 - lam * (a1 / l1)
    o = o * lax.rsqrt(jnp.mean(o * o, axis=-1, keepdims=True) + 1e-6) * g_ref[...]
    o_ref[...] = o * (1.0 - lam_init)


def _diff_attention(qr, kr, vr, lam_params, norm_g, lam_init, tq=512, tk=8192):
    b, h, l, _ = qr.shape
    nq = l // tq
    tk = min(tk, l)
    return pl.pallas_call(
        functools.partial(_da_kernel, lam_init, tk),
        out_shape=jax.ShapeDtypeStruct((b * l, BRANCH_W), F32),
        grid=(b, h, nq),
        in_specs=[
            pl.BlockSpec((4, DA_HD), lambda bi, hi, qi: (0, 0)),
            pl.BlockSpec((1, DA_VD), lambda bi, hi, qi: (0, 0)),
            pl.BlockSpec((None, None, tq, DA_VD), lambda bi, hi, qi: (bi, hi, qi, 0)),
            pl.BlockSpec((None, None, l, DA_VD), lambda bi, hi, qi: (bi, hi, 0, 0)),
            pl.BlockSpec((None, None, l, DA_VD), lambda bi, hi, qi: (bi, hi, 0, 0)),
        ],
        out_specs=pl.BlockSpec((tq, DA_VD), lambda bi, hi, qi: (bi * nq + qi, hi)),
        compiler_params=_cparams(("parallel", "parallel", "arbitrary")),
        name="diff_attention",
    )(lam_params, norm_g.reshape(1, DA_VD), qr, kr, vr)


def _mem_attn_kernel(q_ref, kv_ref, o_ref):
    q = q_ref[...]
    kv = kv_ref[...]
    scale = MEM_HD ** -0.5
    for h in range(MEM_HEADS):
        qh = q[:, h * MEM_HD:(h + 1) * MEM_HD].astype(BF16)
        kh = kv[:, h * MEM_HD:(h + 1) * MEM_HD].astype(BF16)
        vh = kv[:, BRANCH_W + h * MEM_HD:BRANCH_W + (h + 1) * MEM_HD].astype(BF16)
        s = lax.dot_general(qh, kh, (((1,), (1,)), ((), ())), preferred_element_type=F32) * scale
        e = jnp.exp(s - jnp.max(s, axis=-1, keepdims=True))
        p = e / jnp.sum(e, axis=-1, keepdims=True)
        o_ref[:, h * MEM_HD:(h + 1) * MEM_HD] = jnp.dot(p.astype(BF16), vh, preferred_element_type=F32)


def _memory_attention(z, kv, b, l, tm=512):
    nm = l // tm
    m = kv.shape[1]
    return pl.pallas_call(
        _mem_attn_kernel,
        out_shape=jax.ShapeDtypeStruct((b * l, BRANCH_W), F32),
        grid=(b, nm),
        in_specs=[
            pl.BlockSpec((tm, ME_COLS), lambda bi, i: (bi * nm + i, ME_OFF // ME_COLS)),
            pl.BlockSpec((None, m, 2 * BRANCH_W), lambda bi, i: (bi, 0, 0)),
        ],
        out_specs=pl.BlockSpec((tm, BRANCH_W), lambda bi, i: (bi * nm + i, 0)),
        compiler_params=_cparams(("parallel", "arbitrary")),
        name="memory_attention",
    )(z, kv)


DFT_NA = 128
DFT_NB = 128
HY_LBLK = 4096
HY_KB = 8
HIGHEST = lax.Precision.HIGHEST


def _dft_tables(l):
    assert 2 * l == DFT_NA * DFT_NB
    n = DFT_NA * DFT_NB
    i = jnp.arange(DFT_NA, dtype=jnp.int32)
    ang1 = (2.0 * math.pi / DFT_NA) * ((i[:, None] * i[None, :]) % DFT_NA).astype(F32)
    c1, s1 = jnp.cos(ang1), jnp.sin(ang1)
    half = DFT_NA // 2
    kk = i[:, None, None] + DFT_NA * i[None, :, None]
    ang = (2.0 * math.pi / n) * ((kk * i[None, None, :]) % n).astype(F32)
    gr, gi = jnp.cos(ang), -jnp.sin(ang)
    t = dict(
        f=jnp.concatenate([c1[:, :half], -s1[:, :half]], axis=0),
        gr=gr, gi=gi,
        gg=jnp.concatenate([gr, gi], axis=1),
        hh=jnp.concatenate([gr.transpose(0, 2, 1), gi.transpose(0, 2, 1)], axis=1),
        er=c1.T[:half] / n, ei=s1.T[:half] / n,
    )
    for name in ('f', 'gg', 'hh', 'er', 'ei'):
        t[name + '_bf'] = t[name].astype(BF16)
    return t


def _dft1_kernel(prec, x_ref, f_ref, ar_ref, ai_ref):
    x = x_ref[...]
    f = f_ref[...]
    if prec is None:
        a = jnp.dot(f, x.astype(BF16), preferred_element_type=F32)
    else:
        a = jnp.dot(f, x, precision=prec, preferred_element_type=F32)
    ar_ref[...] = a[:DFT_NA].astype(ar_ref.dtype)
    ai_ref[...] = a[DFT_NA:].astype(ai_ref.dtype)


def _dft_stage1(x, f, out_dtype, prec):
    g, r, w = x.shape
    spec_o = pl.BlockSpec((None, DFT_NA, HY_LBLK), lambda gi, j: (gi, 0, j))
    return pl.pallas_call(
        functools.partial(_dft1_kernel, prec),
        out_shape=[jax.ShapeDtypeStruct((g, DFT_NA, w), out_dtype)] * 2,
        grid=(g, w // HY_LBLK),
        in_specs=[pl.BlockSpec((None, r, HY_LBLK), lambda gi, j: (gi, 0, j)),
                  pl.BlockSpec((2 * DFT_NA, r), lambda gi, j: (0, 0))],
        out_specs=[spec_o, spec_o],
        compiler_params=_cparams(("parallel", "parallel")),
        name="hyena_dft_stage1",
    )(x, f)


def _filter_mlp_kernel(l, feats_ref, w1_ref, b1_ref, w2_ref, b2_ref, w3_ref, b3_ref, fq_ref, dl_ref, h_ref, nrm_ref):
    i = pl.program_id(0)
    tm = feats_ref.shape[0]
    dot = functools.partial(jnp.dot, precision=HIGHEST, preferred_element_type=F32)
    h = jnp.sin(fq_ref[0:1] * (dot(feats_ref[...], w1_ref[...]) + b1_ref[...]))
    h = jnp.sin(fq_ref[1:2] * (dot(h, w2_ref[...]) + b2_ref[...]))
    h = dot(h, w3_ref[...]) + b3_ref[...]
    row = i * tm + lax.broadcasted_iota(jnp.int32, (tm, 1), 0)
    t = row.astype(F32) * (1.0 / (l - 1))
    decay = jnp.exp(-t * dl_ref[...])
    h = h * jnp.concatenate([decay] * (2 * HY_ORDER), axis=1)
    col = lax.broadcasted_iota(jnp.int32, (1, 2 * HY_ORDER * HY_W), 1)
    bwd = jnp.bitwise_and(lax.shift_right_logical(col, HY_W.bit_length() - 1), 1) == 1
    h = jnp.where(jnp.logical_and(row == 0, bwd), 0.0, h)
    h_ref[...] = h

    @pl.when(i == 0)
    def _():
        nrm_ref[...] = jnp.zeros_like(nrm_ref)

    nrm_ref[...] += jnp.sum(jnp.abs(h), axis=0, keepdims=True)


def _filter_mlp(feats, w1, b1, w2, b2, w3, b3, freq, deltas, tm=1024):
    l = feats.shape[0]
    wcols = 2 * HY_ORDER * HY_W
    full = lambda a: pl.BlockSpec(a.shape, lambda i: (0,) * a.ndim)
    args = (feats, w1, b1.reshape(1, -1), w2, b2.reshape(1, -1), w3, b3.reshape(1, -1), freq, deltas.reshape(1, -1))
    return pl.pallas_call(
        functools.partial(_filter_mlp_kernel, l),
        out_shape=[jax.ShapeDtypeStruct((l, wcols), F32), jax.ShapeDtypeStruct((1, wcols), F32)],
        grid=(l // tm,),
        in_specs=[pl.BlockSpec((tm, feats.shape[1]), lambda i: (i, 0))] + [full(a) for a in args[1:]],
        out_specs=[pl.BlockSpec((tm, wcols), lambda i: (i, 0)), pl.BlockSpec((1, wcols), lambda i: (0, 0))],
        compiler_params=_cparams(("arbitrary",)),
        name="hyena_filter_mlp",
    )(*args)


def _filter_spec_kernel(afr_ref, afi_ref, abr_ref, abi_ref, gr_ref, gi_ref, nf_ref, nb_ref, kr_ref, ki_ref):
    dot = functools.partial(jnp.dot, precision=HIGHEST, preferred_element_type=F32)
    inv = 1.0 / (nf_ref[...] + nb_ref[...])
    for j in range(gr_ref.shape[0]):
        gr, gi = gr_ref[j], gi_ref[j]
        sr = afr_ref[j] + abr_ref[j]
        si = afi_ref[j] + abi_ref[j]
        dr = afr_ref[j] - abr_ref[j]
        di = afi_ref[j] - abi_ref[j]
        kr_ref[j] = (dot(gr, sr) - dot(gi, si)) * inv
        ki_ref[j] = (dot(gr, di) + dot(gi, dr)) * inv


def _filter_spectrum(ar, ai, gr, gi, nrm, kb=4, cw=256):
    ncw = HY_W // cw

    def a_spec(d):
        return pl.BlockSpec((kb, DFT_NB, cw), lambda o, ki, ci: (ki, 0, (2 * o + d) * ncw + ci))

    def n_spec(d):
        return pl.BlockSpec((1, cw), lambda o, ki, ci: (0, (2 * o + d) * ncw + ci))

    g_spec = pl.BlockSpec((kb, DFT_NB, DFT_NB), lambda o, ki, ci: (ki, 0, 0))
    o_spec = pl.BlockSpec((None, kb, DFT_NB, cw), lambda o, ki, ci: (o, ki, 0, ci))
    return pl.pallas_call(
        _filter_spec_kernel,
        out_shape=[jax.ShapeDtypeStruct((HY_ORDER, DFT_NA, DFT_NB, HY_W), F32)] * 2,
        grid=(HY_ORDER, DFT_NA // kb, ncw),
        in_specs=[a_spec(0), a_spec(0), a_spec(1), a_spec(1), g_spec, g_spec, n_spec(0), n_spec(1)],
        out_specs=[o_spec, o_spec],
        compiler_params=_cparams(("parallel", "parallel", "parallel")),
        name="hyena_filter_spectrum",
    )(ar, ai, ar, ai, gr, gi, nrm, nrm)


def _hy_mid_kernel(ar_ref, ai_ref, gg_ref, hh_ref, kr_ref, ki_ref, br_ref, bi_ref):
    for j in range(HY_KB):
        gg = gg_ref[j]
        p = jnp.dot(gg, ar_ref[j], preferred_element_type=F32)
        q = jnp.dot(gg, ai_ref[j], preferred_element_type=F32)
        xr = p[:DFT_NB] - q[DFT_NB:]
        xi = q[:DFT_NB] + p[DFT_NB:]
        kr, ki = kr_ref[j], ki_ref[j]
        yr = (xr * kr - xi * ki).astype(BF16)
        yi = (xr * ki + xi * kr).astype(BF16)
        hh = hh_ref[j]
        p = jnp.dot(hh, yr, preferred_element_type=F32)
        q = jnp.dot(hh, yi, preferred_element_type=F32)
        br_ref[j] = (p[:DFT_NB] + q[DFT_NB:]).astype(BF16)
        bi_ref[j] = (q[:DFT_NB] - p[DFT_NB:]).astype(BF16)


def _hy_mid(ar, ai, gg, hh, kr, ki):
    b = ar.shape[0]
    a_spec = pl.BlockSpec((None, HY_KB, DFT_NB, HY_W), lambda ki_, bi: (bi, ki_, 0, 0))
    t_spec = pl.BlockSpec((HY_KB, 2 * DFT_NB, DFT_NB), lambda ki_, bi: (ki_, 0, 0))
    k_spec = pl.BlockSpec((HY_KB, DFT_NB, HY_W), lambda ki_, bi: (ki_, 0, 0))
    return pl.pallas_call(
        _hy_mid_kernel,
        out_shape=[jax.ShapeDtypeStruct(ar.shape, BF16)] * 2,
        grid=(DFT_NA // HY_KB, b),
        in_specs=[a_spec, a_spec, t_spec, t_spec, k_spec, k_spec],
        out_specs=[a_spec, a_spec],
        compiler_params=_cparams(("parallel", "arbitrary")),
        name="hyena_dft_mid",
    )(ar, ai, gg, hh, kr, ki)


def _hy_out_kernel(br_ref, bi_ref, er_ref, ei_ref, u_ref, gate_ref, skip_ref, o_ref):
    y = (jnp.dot(er_ref[...], br_ref[...], preferred_element_type=F32)
         - jnp.dot(ei_ref[...], bi_ref[...], preferred_element_type=F32))
    o_ref[...] = gate_ref[...] * (y + u_ref[...] * skip_ref[...])


def _hy_out(br, bi, er, ei, u, gate, skip_row):
    b, r, w = u.shape
    row = pl.BlockSpec((None, r, HY_LBLK), lambda bi_, j: (bi_, 0, j))
    bspec = pl.BlockSpec((None, DFT_NA, HY_LBLK), lambda bi_, j: (bi_, 0, j))
    espec = pl.BlockSpec((r, DFT_NA), lambda bi_, j: (0, 0))
    return pl.pallas_call(
        _hy_out_kernel,
        out_shape=jax.ShapeDtypeStruct(u.shape, F32),
        grid=(b, w // HY_LBLK),
        in_specs=[bspec, bspec, espec, espec, row, row, pl.BlockSpec((1, HY_LBLK), lambda bi_, j: (0, 0))],
        out_specs=row,
        compiler_params=_cparams(("parallel", "parallel")),
        name="hyena_dft_out",
    )(br, bi, er, ei, u, gate, skip_row)


def _short_conv_kernel(z_ref, w_ref, b_ref, o_ref):
    z = z_ref[...]
    l = z.shape[0]
    row = lax.broadcasted_iota(jnp.int32, (l, 1), 0)
    prev = jnp.where(row == 0, 0.0, pltpu.roll(z, 1, 0))
    nxt = jnp.where(row == l - 1, 0.0, pltpu.roll(z, l - 1, 0))
    w = w_ref[...]
    o_ref[...] = prev * w[0:1] + z * w[1:2] + nxt * w[2:3] + b_ref[...]


def _short_conv(z, w, bias, b, l):
    nc = HY_W // LANE
    return pl.pallas_call(
        _short_conv_kernel,
        out_shape=jax.ShapeDtypeStruct((3, b * l, HY_W), F32),
        grid=(b, 3, nc),
        in_specs=[pl.BlockSpec((l, LANE), lambda bi, p, ci: (bi, p * nc + ci)),
                  pl.BlockSpec((3, LANE), lambda bi, p, ci: (0, p * nc + ci)),
                  pl.BlockSpec((1, LANE), lambda bi, p, ci: (0, p * nc + ci))],
        out_specs=pl.BlockSpec((None, l, LANE), lambda bi, p, ci: (p, bi, ci)),
        compiler_params=_cparams(("parallel", "parallel", "parallel")),
        name="hyena_short_conv",
    )(z, w, bias.reshape(1, -1))


def _hyena_filter_spectrum(l, w1, b1, w2, b2, w3, b3, freq, tabs):
    t = jnp.linspace(0.0, 1.0, l, dtype=F32)[:, None]
    w = 2.0 * math.pi * jnp.arange(l, dtype=F32)[:, None] / l
    fr = jnp.linspace(1e-4, HY_BANDS - 1, HY_BANDS, dtype=F32)[None, :]
    feats = jnp.concatenate([t, jnp.cos(fr * w), -jnp.sin(fr * w)], axis=-1)
    deltas = jnp.abs(jnp.linspace(HY_MIN_DECAY, HY_MAX_DECAY, HY_W, dtype=F32))
    h, nrm = _filter_mlp(feats, w1, b1, w2, b2, w3, b3, freq, deltas)
    wcols = h.shape[1]
    ar, ai = _dft_stage1(h.reshape(1, l // DFT_NB, DFT_NB * wcols), tabs['f'], F32, HIGHEST)
    return _filter_spectrum(ar.reshape(DFT_NA, DFT_NB, wcols), ai.reshape(DFT_NA, DFT_NB, wcols),
                            tabs['gr'], tabs['gi'], nrm)


def _hyena(z, short_w, short_b, skip, kr, ki, tabs, b, l):
    u = _short_conv(z, short_w, short_b, b, l)
    rows = l // DFT_NB
    view = lambda a: a.reshape(b, rows, DFT_NB * HY_W)
    v, gates = view(u[0]), (view(u[1]), view(u[2]))
    for o in range(HY_ORDER):
        ar, ai = _dft_stage1(v, tabs['f_bf'], BF16, None)
        shp4 = (b, DFT_NA, DFT_NB, HY_W)
        br, bi = _hy_mid(ar.reshape(shp4), ai.reshape(shp4), tabs['gg_bf'], tabs['hh_bf'], kr[o], ki[o])
        shp3 = (b, DFT_NA, DFT_NB * HY_W)
        skip_row = jnp.tile(skip[o], HY_LBLK // HY_W).reshape(1, HY_LBLK)
        v = _hy_out(br.reshape(shp3), bi.reshape(shp3), tabs['er_bf'], tabs['ei_bf'], v, gates[o], skip_row)
    return v.reshape(b * l, HY_W)


HG_REC_CHUNK = 64
HG_SUB = 16
HG_TILE = 512
HG_HEADS_PER_STEP = 2
HG_UNROLL = 1


def _gate_terms(z, lb):
    a = jnp.log(lb)
    b = jnp.log1p(-lb) + (jnp.minimum(z, 0.0) - jnp.log1p(jnp.exp(-jnp.abs(z))))
    log_f = jnp.maximum(a, b) + jnp.log1p(jnp.exp(-jnp.abs(a - b)))
    return log_f, (1.0 - lb) * jax.nn.sigmoid(-z)


SUBLANES = 8


def _running_sum_rows(x, fwd):
    r = x.shape[0]
    row = jnp.bitwise_and(lax.broadcasted_iota(jnp.int32, (r, 1), 0), SUBLANES - 1)
    s = 1
    while s < SUBLANES:
        if fwd:
            x = x + jnp.where(row >= s, pltpu.roll(x, s, 0), 0.0)
        else:
            x = x + jnp.where(row < SUBLANES - s, pltpu.roll(x, r - s, 0), 0.0)
        s *= 2
    groups = [x[i:i + SUBLANES] for i in range(0, r, SUBLANES)]
    order = range(1, len(groups)) if fwd else range(len(groups) - 2, -1, -1)
    for i in order:
        prev = groups[i - 1][SUBLANES - 1:SUBLANES] if fwd else groups[i + 1][0:1]
        groups[i] = groups[i] + prev
    return jnp.concatenate(groups, axis=0)


def _gla_chunk(q, k, v, g, st, fwd):
    ch = q.shape[0]
    b = _running_sum_rows(g, fwd)
    nsb = ch // HG_SUB
    parts = [None] * nsb

    def add(i, val):
        parts[i] = val if parts[i] is None else parts[i] + val

    vb = v.astype(BF16)
    li = lax.broadcasted_iota(jnp.int32, (ch, HG_SUB), 1)
    si = jnp.bitwise_and(lax.broadcasted_iota(jnp.int32, (ch, HG_SUB), 0), HG_SUB - 1)
    keep = (li <= si) if fwd else (li >= si)
    a = jnp.zeros((ch, HG_SUB), F32)
    for s in range(HG_SUB):
        src = [i * HG_SUB + s for i in range(nsb)]
        bs = jnp.concatenate([jnp.broadcast_to(b[r:r + 1], (HG_SUB, LANE)) for r in src], axis=0)
        ks = jnp.concatenate([jnp.broadcast_to(k[r:r + 1], (HG_SUB, LANE)) for r in src], axis=0)
        e = jnp.exp(jnp.minimum(b - bs, 0.0))
        a = jnp.where(li == s, jnp.sum(q * e * ks, axis=-1, keepdims=True), a)
    a = jnp.where(keep, a, 0.0).astype(BF16)
    for i in range(nsb):
        sl = slice(i * HG_SUB, (i + 1) * HG_SUB)
        add(i, jnp.dot(a[sl], vb[sl], preferred_element_type=F32))
    h = HG_SUB
    while h < ch:
        for j in range(ch // (2 * h)):
            lo = slice(2 * h * j, 2 * h * j + h)
            hi = slice(2 * h * j + h, 2 * h * j + 2 * h)
            if fwd:
                rows, cols, ref = hi, lo, b[2 * h * j + h - 1:2 * h * j + h]
            else:
                rows, cols, ref = lo, hi, b[2 * h * j + h:2 * h * j + h + 1]
            qt = (q[rows] * jnp.exp(b[rows] - ref)).astype(BF16)
            kt = (k[cols] * jnp.exp(ref - b[cols])).astype(BF16)
            a = lax.dot_general(qt, kt, (((1,), (1,)), ((), ())), preferred_element_type=F32)
            ov = jnp.dot(a.astype(BF16), vb[cols], preferred_element_type=F32)
            for ii in range(h // HG_SUB):
                add(rows.start // HG_SUB + ii, ov[ii * HG_SUB:(ii + 1) * HG_SUB])
        h *= 2
    o = jnp.concatenate(parts, axis=0)
    qh = (q * jnp.exp(b)).astype(BF16)
    o = o + lax.dot_general(qh, st.astype(BF16), (((1,), (1,)), ((), ())), preferred_element_type=F32)
    btot = b[ch - 1:ch] if fwd else b[0:1]
    kh = (k * jnp.exp(btot - b)).astype(BF16)
    st_new = st * jnp.exp(btot) + lax.dot_general(vb, kh, (((0,), (0,)), ((), ())), preferred_element_type=F32)
    return o, st_new


def _hgrn2_kernel(lbf_ref, lbb_ref, ng_ref, zqf_ref, zif_ref, zff_ref, zgf_ref, zqb_ref, zib_ref, zbb_ref, zgb_ref,
                  o_ref, st_ref):
    c = pl.program_id(2)
    n = pl.num_programs(2)
    nch = HG_TILE // HG_REC_CHUNK

    @pl.when(c == 0)
    def _():
        st_ref[...] = jnp.zeros_like(st_ref)

    def body(second_pass, i, carry):
        off_f = pl.multiple_of(i * HG_REC_CHUNK, HG_REC_CHUNK)
        off_b = pl.multiple_of((nch - 1 - i) * HG_REC_CHUNK, HG_REC_CHUNK)
        for fwd, off, zq_ref, zi_ref, zd_ref, zg_ref, lb_ref, tile in (
                (True, off_f, zqf_ref, zif_ref, zff_ref, zgf_ref, lbf_ref, c),
                (False, off_b, zqb_ref, zib_ref, zbb_ref, zgb_ref, lbb_ref, n - 1 - c)):
            sl = pl.ds(off, HG_REC_CHUNK)
            rows = pl.ds(pl.multiple_of(tile * HG_TILE + off, HG_REC_CHUNK), HG_REC_CHUNK)
            d = 0 if fwd else 1
            for hh in range(HG_HEADS_PER_STEP):
                ln = slice(hh * LANE, (hh + 1) * LANE)
                g, k = _gate_terms(zd_ref[sl, ln], lb_ref[:, ln])
                o, st_new = _gla_chunk(jax.nn.silu(zq_ref[sl, ln]), k, zi_ref[sl, ln], g, st_ref[d, hh], fwd)
                st_ref[d, hh] = st_new
                if second_pass:
                    tot = o_ref[rows, ln] + o
                    tot = tot * lax.rsqrt(jnp.mean(tot * tot, axis=-1, keepdims=True) + 1e-6) * ng_ref[...]
                    o_ref[rows, ln] = tot * jax.nn.silu(zg_ref[sl, ln])
                else:
                    o_ref[rows, ln] = o
        return carry

    @pl.when(c < n // 2)
    def _():
        lax.fori_loop(0, nch, functools.partial(body, False), 0, unroll=HG_UNROLL)

    @pl.when(c >= n // 2)
    def _():
        lax.fori_loop(0, nch, functools.partial(body, True), 0, unroll=HG_UNROLL)


def _hgrn2(z, lb_fwd, lb_bwd, norm_g, b, l):
    n = l // HG_TILE
    assert n % 2 == 0
    w = HG_HEADS_PER_STEP * LANE
    base = HY_COLS // w
    nh = BRANCH_W // w

    def zspec(part, rev):
        if rev:
            return pl.BlockSpec((HG_TILE, w), lambda bi, hi, ci: (bi * n + n - 1 - ci, base + part * nh + hi))
        return pl.BlockSpec((HG_TILE, w), lambda bi, hi, ci: (bi * n + ci, base + part * nh + hi))

    lbspec = pl.BlockSpec((1, w), lambda bi, hi, ci: (0, hi))
    return pl.pallas_call(
        _hgrn2_kernel,
        out_shape=jax.ShapeDtypeStruct((b * l, BRANCH_W), F32),
        grid=(b, nh, n),
        in_specs=[lbspec, lbspec, pl.BlockSpec((1, LANE), lambda bi, hi, ci: (0, 0)),
                  zspec(0, False), zspec(1, False), zspec(2, False), zspec(4, False),
                  zspec(0, True), zspec(1, True), zspec(3, True), zspec(4, True)],
        out_specs=pl.BlockSpec((l, w), lambda bi, hi, ci: (bi, hi)),
        scratch_shapes=[pltpu.VMEM((2, HG_HEADS_PER_STEP, HG_DV, HG_DK), F32)],
        compiler_params=_cparams(("parallel", "parallel", "arbitrary")),
        name="hgrn2_scan",
    )(lb_fwd.reshape(1, BRANCH_W), lb_bwd.reshape(1, BRANCH_W), norm_g.reshape(1, HG_DV),
      z, z, z, z, z, z, z, z)


def _merge_kernel(x_ref, o0_ref, o1_ref, o2_ref, o3_ref, wg_ref, bg_ref, wbr_ref, wout_ref, g_ref, b_ref,
                  out_ref, acc_ref, xb_ref, ob_ref):
    j = pl.program_id(1)

    @pl.when(j == 0)
    def _():
        xb_ref[...] = x_ref[...].astype(BF16)
        for bi, o_ref in enumerate((o0_ref, o1_ref, o2_ref, o3_ref)):
            ob_ref[bi] = o_ref[...].astype(BF16)
        acc_ref[...] = jnp.zeros_like(acc_ref)

    xb = xb_ref[...]
    merged = None
    for bi in range(N_BRANCH):
        gate = jax.nn.sigmoid(jnp.dot(xb, wg_ref[bi], preferred_element_type=F32) + bg_ref[bi])
        term = gate * jnp.dot(ob_ref[bi], wbr_ref[bi], preferred_element_type=F32)
        merged = term if merged is None else merged + term
    acc_ref[...] += jnp.dot(merged.astype(BF16), wout_ref[...], preferred_element_type=F32)

    @pl.when(j == pl.num_programs(1) - 1)
    def _():
        out_ref[...] = _ln_rows(DN_ALPHA * x_ref[...] + acc_ref[...], g_ref[...], b_ref[...])


def _merge(x, branches, w_gate, b_gate, w_br, w_out, ln_g, ln_b, tm=512, tn=256):
    t, d = x.shape
    row = pl.BlockSpec((tm, d), lambda i, j: (i, 0))
    brow = pl.BlockSpec((tm, BRANCH_W), lambda i, j: (i, 0))
    vec = pl.BlockSpec((1, d), lambda i, j: (0, 0))
    return pl.pallas_call(
        _merge_kernel,
        out_shape=jax.ShapeDtypeStruct((t, d), F32),
        grid=(t // tm, d // tn),
        in_specs=[
            row, brow, brow, brow, brow,
            pl.BlockSpec((N_BRANCH, d, tn), lambda i, j: (0, 0, j)),
            pl.BlockSpec((N_BRANCH, 1, tn), lambda i, j: (0, 0, j)),
            pl.BlockSpec((N_BRANCH, BRANCH_W, tn), lambda i, j: (0, 0, j)),
            pl.BlockSpec((tn, d), lambda i, j: (j, 0)),
            vec, vec,
        ],
        out_specs=row,
        scratch_shapes=[pltpu.VMEM((tm, d), F32), pltpu.VMEM((tm, d), BF16),
                        pltpu.VMEM((N_BRANCH, tm, BRANCH_W), BF16)],
        compiler_params=_cparams(("parallel", "arbitrary")),
        name="gated_merge",
    )(x, *branches, w_gate, b_gate.reshape(N_BRANCH, 1, d), w_br, w_out, ln_g.reshape(1, d), ln_b.reshape(1, d))


def _expert_kernel(x_ref, gate_ref, w1_ref, w3_ref, w2_ref, o_ref):
    f = pl.program_id(2)
    x = x_ref[...]
    h1 = jnp.dot(x, w1_ref[...], preferred_element_type=F32)
    h3 = jnp.dot(x, w3_ref[...], preferred_element_type=F32)
    h = (jax.nn.silu(h1) * h3).astype(BF16)
    part = jnp.dot(h, w2_ref[...], preferred_element_type=F32)

    @pl.when(f == 0)
    def _():
        o_ref[...] = part

    @pl.when(f != 0)
    def _():
        o_ref[...] += part

    @pl.when(f == pl.num_programs(2) - 1)
    def _():
        o_ref[...] = o_ref[...] * gate_ref[...]


def _expert_ffn(xe, gate, w1, w3, w2, tm=512, tf=1024):
    e, c, d = xe.shape
    ff = w1.shape[2]
    tm = min(tm, c)
    return pl.pallas_call(
        _expert_kernel,
        out_shape=jax.ShapeDtypeStruct((e, c, d), F32),
        grid=(e, c // tm, ff // tf),
        in_specs=[
            pl.BlockSpec((None, tm, d), lambda ei, i, f: (ei, i, 0)),
            pl.BlockSpec((None, tm, 1), lambda ei, i, f: (ei, i, 0)),
            pl.BlockSpec((None, d, tf), lambda ei, i, f: (ei, 0, f)),
            pl.BlockSpec((None, d, tf), lambda ei, i, f: (ei, 0, f)),
            pl.BlockSpec((None, tf, d), lambda ei, i, f: (ei, f, 0)),
        ],
        out_specs=pl.BlockSpec((None, tm, d), lambda ei, i, f: (ei, i, 0)),
        compiler_params=_cparams(("parallel", "parallel", "arbitrary")),
        name="expert_ffn",
    )(xe, gate, w1, w3, w2)


RT_TM = 1024
RT_WIN = 32
RT_EAGER = 2
RT_CHUNKS = 4


def _router_kernel(x_ref, w_ref, o_ref):
    logits = lax.dot_general(w_ref[...], x_ref[...].astype(BF16), (((1,), (1,)), ((), ())),
                             preferred_element_type=F32)
    e = jnp.exp(logits - jnp.max(logits, axis=0, keepdims=True))
    aff = e / jnp.sum(e, axis=0, keepdims=True)
    for c in range(x_ref.shape[0] // LANE):
        o_ref[:, c, :] = aff[:, c * LANE:(c + 1) * LANE]


def _router(x, w_t):
    t, d = x.shape
    return pl.pallas_call(
        _router_kernel,
        out_shape=jax.ShapeDtypeStruct((N_EXPERTS, t // LANE, LANE), F32),
        grid=(t // RT_TM,),
        in_specs=[pl.BlockSpec((RT_TM, d), lambda i: (i, 0)), pl.BlockSpec((N_EXPERTS, d), lambda i: (0, 0))],
        out_specs=pl.BlockSpec((N_EXPERTS, RT_TM // LANE, LANE), lambda i: (0, i, 0)),
        compiler_params=_cparams(("parallel",)),
        name="moe_router",
    )(x, w_t)


def _prefix_count(mask):
    nch = mask.shape[0]
    m = jnp.where(mask, 1.0, 0.0)
    li = lax.broadcasted_iota(jnp.int32, (LANE, LANE), 0)
    lj = lax.broadcasted_iota(jnp.int32, (LANE, LANE), 1)
    incl = jnp.dot(m.astype(BF16), jnp.where(li <= lj, 1.0, 0.0).astype(BF16), preferred_element_type=F32)
    tot = jnp.broadcast_to(incl[:, LANE - 1:LANE], (nch, LANE))
    ri = lax.broadcasted_iota(jnp.int32, (nch, nch), 0)
    ci = lax.broadcasted_iota(jnp.int32, (nch, nch), 1)
    start = jnp.dot(jnp.where(ci < ri, 1.0, 0.0).astype(BF16), tot.astype(BF16), preferred_element_type=F32)
    return incl - m + start, start


def _route_select_kernel(cap, a_ref, pos_ref, start_ref):
    bits = pltpu.bitcast(a_ref[...], jnp.int32)

    def step(i, tau):
        cand = jnp.bitwise_or(tau, lax.shift_left(jnp.int32(1), 30 - i))
        cnt = jnp.sum(jnp.where(bits >= cand, 1.0, 0.0))
        return jnp.where(cnt >= cap, cand, tau)

    tau = lax.fori_loop(0, 31, step, jnp.int32(0))
    gt = bits > tau
    eq = bits == tau
    need = cap - jnp.sum(jnp.where(gt, 1.0, 0.0))
    eq_rank, _ = _prefix_count(eq)
    sel = jnp.logical_or(gt, jnp.logical_and(eq, eq_rank < need))
    pos, start = _prefix_count(sel)
    pos_ref[...] = jnp.where(sel, pos, -1.0)
    start_ref[...] = start


def _route_select(aff3, cap):
    e, nch, _ = aff3.shape
    spec = pl.BlockSpec((None, nch, LANE), lambda ei: (ei, 0, 0))
    return pl.pallas_call(
        functools.partial(_route_select_kernel, float(cap)),
        out_shape=[jax.ShapeDtypeStruct(aff3.shape, F32)] * 2,
        grid=(e,),
        in_specs=[spec],
        out_specs=[spec, spec],
        compiler_params=_cparams(("parallel",)),
        name="moe_route_select",
    )(aff3)


def _route_compact_kernel(cap, start_sm, pos_ref, a_ref, idx_ref, gate_ref, il_ref, gl_ref):
    ei = pl.program_id(0)
    nch = pos_ref.shape[0]
    il_ref[...] = jnp.zeros_like(il_ref)
    gl_ref[...] = jnp.zeros_like(gl_ref)
    s_iota = lax.broadcasted_iota(jnp.int32, (RT_WIN, LANE), 0).astype(F32)
    lane = lax.broadcasted_iota(jnp.int32, (1, LANE), 1).astype(F32)
    log_win = RT_WIN.bit_length() - 1

    def body(i, carry):
        rest = []
        for u in range(RT_CHUNKS):
            j = i * RT_CHUNKS + u
            first = start_sm[ei, j]
            nxt = jnp.where(j + 1 < nch, start_sm[ei, jnp.minimum(j + 1, nch - 1)], cap)
            base = lax.shift_left(lax.shift_right_logical(first, log_win), log_win)
            nwin = lax.shift_right_logical(nxt - base + (RT_WIN - 1), log_win)
            prow = pos_ref[pl.ds(j, 1), :]
            arow = a_ref[pl.ds(j, 1), :]
            tok = lane + (j * LANE).astype(F32)

            def window(w, c2, base=base, prow=prow, arow=arow, tok=tok):
                lo = pl.multiple_of(base + w * RT_WIN, RT_WIN)
                hit = (prow - lo.astype(F32)) == s_iota
                il_ref[pl.ds(lo, RT_WIN), :] += jnp.sum(jnp.where(hit, tok, 0.0), axis=1, keepdims=True)
                gl_ref[pl.ds(lo, RT_WIN), :] += jnp.sum(jnp.where(hit, arow, 0.0), axis=1, keepdims=True)
                return c2

            for w in range(RT_EAGER):
                window(w, 0)
            rest.append((nwin, window))
        for nwin, window in rest:
            lax.fori_loop(RT_EAGER, nwin, window, 0)
        return carry

    lax.fori_loop(0, nch // RT_CHUNKS, body, 0)
    diag = (lax.broadcasted_iota(jnp.int32, (LANE, LANE), 0) == lax.broadcasted_iota(jnp.int32, (LANE, LANE), 1))
    for r in range(idx_ref.shape[0]):
        rows = slice(r * LANE, (r + 1) * LANE)
        idx_ref[r:r + 1, :] = jnp.sum(jnp.where(diag, il_ref[rows, :], 0.0), axis=0, keepdims=True).astype(jnp.int32)
        gate_ref[r:r + 1, :] = jnp.sum(jnp.where(diag, gl_ref[rows, :], 0.0), axis=0, keepdims=True)


def _route_compact(start, pos3, aff3, cap):
    e, nch, _ = pos3.shape
    spec = pl.BlockSpec((None, nch, LANE), lambda ei, s: (ei, 0, 0))
    ospec = pl.BlockSpec((None, cap // LANE, LANE), lambda ei, s: (ei, 0, 0))
    return pl.pallas_call(
        functools.partial(_route_compact_kernel, cap),
        out_shape=[jax.ShapeDtypeStruct((e, cap // LANE, LANE), jnp.int32),
                   jax.ShapeDtypeStruct((e, cap // LANE, LANE), F32)],
        grid_spec=pltpu.PrefetchScalarGridSpec(
            num_scalar_prefetch=1, grid=(e,), in_specs=[spec, spec], out_specs=[ospec, ospec],
            scratch_shapes=[pltpu.VMEM((cap + 2 * LANE, LANE), F32), pltpu.VMEM((cap + 2 * LANE, LANE), F32)]),
        compiler_params=_cparams(("parallel",)),
        name="moe_route_compact",
    )(start, pos3, aff3)


def _expert_choice(x, w_t, cap):
    aff3 = _router(x, w_t)
    pos3, start3 = _route_select(aff3, cap)
    idx3, gate3 = _route_compact(start3[:, :, 0].astype(jnp.int32), pos3, aff3, cap)
    return idx3.reshape(N_EXPERTS, cap), gate3.reshape(N_EXPERTS, cap)


def _attn_prep_kernel(zq_ref, zk_ref, zv_ref, cos_ref, sin_ref, q_ref, k_ref, v_ref):
    lane = lax.broadcasted_iota(jnp.int32, (1, LANE), 1)
    first = jnp.bitwise_and(lane, DA_HD - 1) < ROPE_DIM // 2
    cos, sin = cos_ref[...], sin_ref[...]

    def rope(x):
        partner = jnp.where(first, pltpu.roll(x, LANE - ROPE_DIM // 2, 1), pltpu.roll(x, ROPE_DIM // 2, 1))
        return x * cos + partner * sin

    q_ref[...] = (rope(zq_ref[...]) * (DA_HD ** -0.5)).astype(BF16)
    k_ref[...] = rope(zk_ref[...]).astype(BF16)
    v_ref[...] = zv_ref[...].astype(BF16)


def _rope_lane_tables(l):
    inv = ROPE_THETA ** (-jnp.arange(0, ROPE_DIM, 2, dtype=F32) / ROPE_DIM)
    ang = jnp.arange(l, dtype=F32)[:, None] * inv[None, :]
    c, s = jnp.cos(ang), jnp.sin(ang)
    rest = DA_HD - ROPE_DIM
    cos_h = jnp.concatenate([c, c, jnp.ones((l, rest), F32)], axis=1)
    sin_h = jnp.concatenate([-s, s, jnp.zeros((l, rest), F32)], axis=1)
    return jnp.concatenate([cos_h, cos_h], axis=1), jnp.concatenate([sin_h, sin_h], axis=1)


def _attn_prep(z, cos, sin, b, l, tm=1024):
    n = l // tm
    base = DA_OFF // LANE

    def zspec(part):
        return pl.BlockSpec((tm, LANE), lambda bi, hi, i: (bi * n + i, base + part * DA_HEADS + hi))

    tspec = pl.BlockSpec((tm, LANE), lambda bi, hi, i: (i, 0))
    ospec = pl.BlockSpec((None, None, tm, LANE), lambda bi, hi, i: (bi, hi, i, 0))
    return pl.pallas_call(
        _attn_prep_kernel,
        out_shape=[jax.ShapeDtypeStruct((b, DA_HEADS, l, DA_VD), BF16)] * 3,
        grid=(b, DA_HEADS, n),
        in_specs=[zspec(0), zspec(1), zspec(2), tspec, tspec],
        out_specs=[ospec, ospec, ospec],
        compiler_params=_cparams(("parallel", "parallel", "parallel")),
        name="attention_prep",
    )(z, z, z, cos, sin)


def _trunk(x, mem, p):
    B, L, D = x.shape
    T = B * L
    cos, sin = _rope_lane_tables(L)
    lb_all = jnp.cumsum(jax.nn.softmax(p['hg_lb_raw'], axis=1), axis=1)
    lb_all = lb_all - lb_all[:, :1]
    xf = _layer_norm(x.reshape(T, D), p['ln_in_g'], p['ln_in_b'])
    memf = mem.reshape(B * mem.shape[1], D)
    cap = EC_FACTOR * T // N_EXPERTS
    for l in range(DEPTH):
        z = _matmul(xf, p['w_in'][l], 1024, 1024)
        hy_kr, hy_ki = p['hy_spectrum'][L][l]
        o_hy = _hyena(z, p['hy_short_w'][l], p['hy_short_b'][l], p['hy_skip'][l], hy_kr, hy_ki,
                      p['dft_tables'][L], B, L)
        o_hg = _hgrn2(z, lb_all[0, l], lb_all[1, l], p['hg_norm_g'][l], B, L)
        qr, kr, vr = _attn_prep(z, cos, sin, B, L)
        lam_init = 0.8 - 0.6 * math.exp(-0.3 * l)
        o_da = _diff_attention(qr, kr, vr, p['da_lambda'][l], p['da_norm_g'][l], lam_init)
        kv = _matmul(memf, p['w_mem_kv'][l], min(512, memf.shape[0]), 512).reshape(B, -1, 2 * BRANCH_W)
        o_me = _memory_attention(z, kv, B, L)
        xf = _merge(xf, (o_hy, o_hg, o_da, o_me), p['w_gate'][l], p['b_gate'][l], p['w_br'][l], p['w_out'][l],
                    p['ln1_g'][l], p['ln1_b'][l])
        idx, gate = _expert_choice(xf, p['w_router_t'][l], cap)
        xe = xf.astype(BF16)[idx]
        ye = _expert_ffn(xe, gate[..., None], p['w_e1'][l], p['w_e3'][l], p['w_e2'][l])
        y = jnp.zeros_like(xf).at[idx.reshape(-1)].add(ye.reshape(-1, D))
        xf = _layer_norm(xf, p['ln2_g'][l], p['ln2_b'][l], resid=y, alpha=DN_ALPHA)
    return xf.reshape(B, L, D)


def kernel(x_prompt, x_sample, mem_prompt, mem_sample, ln_in_g, ln_in_b, w_in, hy_short_w, hy_short_b, hy_ffn_w1, hy_ffn_b1, hy_ffn_w2, hy_ffn_b2, hy_ffn_w3, hy_ffn_b3, hy_freq, hy_skip, hg_lb_raw, hg_norm_g, da_lambda, da_norm_g, w_mem_kv, w_gate, b_gate, w_br, w_out, ln1_g, ln1_b, w_router, w_e1, w_e3, w_e2, ln2_g, ln2_b):
    p = dict(ln_in_g=ln_in_g, ln_in_b=ln_in_b, hy_short_w=hy_short_w, hy_short_b=hy_short_b,
             hy_ffn_w1=hy_ffn_w1, hy_ffn_b1=hy_ffn_b1, hy_ffn_w2=hy_ffn_w2, hy_ffn_b2=hy_ffn_b2,
             hy_ffn_w3=hy_ffn_w3, hy_ffn_b3=hy_ffn_b3, hy_freq=hy_freq, hy_skip=hy_skip, hg_lb_raw=hg_lb_raw,
             hg_norm_g=hg_norm_g, da_lambda=da_lambda, da_norm_g=da_norm_g, b_gate=b_gate,
             ln1_g=ln1_g, ln1_b=ln1_b, ln2_g=ln2_g, ln2_b=ln2_b)
    for name, w in (('w_in', w_in), ('w_mem_kv', w_mem_kv), ('w_gate', w_gate), ('w_br', w_br), ('w_out', w_out),
                    ('w_e1', w_e1), ('w_e3', w_e3), ('w_e2', w_e2)):
        p[name] = w.astype(BF16)
    p['w_router_t'] = jnp.swapaxes(w_router, 1, 2).astype(BF16)
    p['dft_tables'], p['hy_spectrum'] = {}, {}
    for seq_len in sorted({x_prompt.shape[1], x_sample.shape[1]}):
        tabs = _dft_tables(seq_len)
        p['dft_tables'][seq_len] = tabs
        p['hy_spectrum'][seq_len] = [
            _hyena_filter_spectrum(seq_len, hy_ffn_w1[l], hy_ffn_b1[l], hy_ffn_w2[l], hy_ffn_b2[l], hy_ffn_w3[l],
                                   hy_ffn_b3[l], hy_freq[l], tabs) for l in range(DEPTH)]
    y_prompt = _trunk(x_prompt, mem_prompt, p)
    y_sample = _trunk(x_sample, mem_sample, p)
    return (y_prompt, y_sample)
```

```python
import functools
import math

import jax
import jax.numpy as jnp
from jax import lax
from jax.experimental import pallas as pl
from jax.experimental.pallas import tpu as pltpu

F32 = jnp.float32
BF16 = jnp.bfloat16

D_MODEL = 2048
DEPTH = 4
N_BRANCH = 4
BRANCH_W = D_MODEL // 4
HY_W = BRANCH_W
HY_ORDER = 2
HY_EMB = 33
HY_BANDS = (HY_EMB - 1) // 2
HY_MIN_DECAY = math.log(1e-2) / 1.5
HY_MAX_DECAY = math.log(1e-2) / 0.3
HG_HEADS = 4
HG_DK = BRANCH_W // HG_HEADS
HG_DV = BRANCH_W // HG_HEADS
HG_CHUNK = 64
DA_HEADS = 4
DA_HD = BRANCH_W // (2 * DA_HEADS)
DA_VD = 2 * DA_HD
ROPE_DIM = DA_HD // 4
ROPE_THETA = 500000.0
MEM_HEADS = 4
MEM_HD = BRANCH_W // MEM_HEADS
N_EXPERTS = 16
EC_FACTOR = 2
EXPERT_FF = 2048
DN_ALPHA = (2 * DEPTH) ** 0.25
LN_EPS = 1e-5
HY_COLS = 3 * HY_W
HG_COLS = 5 * BRANCH_W
DA_COLS = 3 * BRANCH_W
ME_COLS = BRANCH_W
IN_W = HY_COLS + HG_COLS + DA_COLS + ME_COLS
DA_OFF = HY_COLS + HG_COLS
ME_OFF = DA_OFF + DA_COLS

V7X_VMEM_LIMIT_BYTES = 56 * 1024 * 1024
LANE = 128


def _cparams(sem):
    return pltpu.CompilerParams(dimension_semantics=sem, vmem_limit_bytes=V7X_VMEM_LIMIT_BYTES)


def _mm_kernel(a_ref, w_ref, o_ref):
    o_ref[...] = jnp.dot(a_ref[...].astype(BF16), w_ref[...], preferred_element_type=F32)


def _matmul(a, w, tm, tn):
    m, k = a.shape
    n = w.shape[1]
    return pl.pallas_call(
        _mm_kernel,
        out_shape=jax.ShapeDtypeStruct((m, n), F32),
        grid=(m // tm, n // tn),
        in_specs=[pl.BlockSpec((tm, k), lambda i, j: (i, 0)), pl.BlockSpec((k, tn), lambda i, j: (0, j))],
        out_specs=pl.BlockSpec((tm, tn), lambda i, j: (i, j)),
        compiler_params=_cparams(("parallel", "arbitrary")),
        name="dense_matmul",
    )(a, w)


def _ln_rows(y, g, b):
    mu = jnp.mean(y, axis=-1, keepdims=True)
    d = y - mu
    var = jnp.mean(d * d, axis=-1, keepdims=True)
    return d * lax.rsqrt(var + LN_EPS) * g + b


def _ln_kernel(alpha, x_ref, r_ref, g_ref, b_ref, o_ref):
    y = x_ref[...]
    if r_ref is not None:
        y = alpha * y + r_ref[...]
    o_ref[...] = _ln_rows(y, g_ref[...], b_ref[...])


def _layer_norm(x, g, b, resid=None, alpha=1.0, tm=512):
    t, d = x.shape
    row = pl.BlockSpec((tm, d), lambda i: (i, 0))
    vec = pl.BlockSpec((1, d), lambda i: (0, 0))
    if resid is None:
        kern = lambda x_ref, g_ref, b_ref, o_ref: _ln_kernel(alpha, x_ref, None, g_ref, b_ref, o_ref)
        args, specs = (x, g.reshape(1, d), b.reshape(1, d)), [row, vec, vec]
    else:
        kern = functools.partial(_ln_kernel, alpha)
        args, specs = (x, resid, g.reshape(1, d), b.reshape(1, d)), [row, row, vec, vec]
    return pl.pallas_call(
        kern,
        out_shape=jax.ShapeDtypeStruct((t, d), F32),
        grid=(t // tm,),
        in_specs=specs,
        out_specs=row,
        compiler_params=_cparams(("parallel",)),
        name="layer_norm",
    )(*args)


def _da_kernel(lam_init, tk, lp_ref, g_ref, q_ref, k_ref, v_ref, o_ref):
    tq = q_ref.shape[0]
    nk = k_ref.shape[0] // tk
    q = q_ref[...]
    qs = (q[:, :DA_HD], q[:, DA_HD:])

    def body(i, carry):
        off = pl.multiple_of(i * tk, tk)
        k = k_ref[pl.ds(off, tk), :]
        v = v_ref[pl.ds(off, tk), :]
        out = []
        for c in range(2):
            m_prev, l_prev, acc_prev = carry[c]
            s = lax.dot_general(qs[c], k[:, c * DA_HD:(c + 1) * DA_HD], (((1,), (1,)), ((), ())),
                                preferred_element_type=F32)
            m_new = jnp.maximum(m_prev, jnp.max(s, axis=-1, keepdims=True))
            a = jnp.exp(m_prev - m_new)
            p = jnp.exp(s - m_new)
            l_new = a * l_prev + jnp.sum(p, axis=-1, keepdims=True)
            acc_new = a * acc_prev + jnp.dot(p.astype(BF16), v, preferred_element_type=F32)
            out.append((m_new, l_new, acc_new))
        return tuple(out)

    init = tuple((jnp.full((tq, 1), -jnp.inf, F32), jnp.zeros((tq, 1), F32), jnp.zeros((tq, DA_VD), F32))
                 for _ in range(2))
    (_, l0, a0), (_, l1, a1) = lax.fori_loop(0, nk, body, init)
    lp = lp_ref[...]
    lam = (jnp.exp(jnp.sum(lp[0:1] * lp[1:2], axis=-1, keepdims=True))
           - jnp.exp(jnp.sum(lp[2:3] * lp[3:4], axis=-1, keepdims=True)) + lam_init)
    o = a0 / l0 - lam * (a1 / l1)
    o = o * lax.rsqrt(jnp.mean(o * o, axis=-1, keepdims=True) + 1e-6) * g_ref[...]
    o_ref[...] = o * (1.0 - lam_init)


def _diff_attention(qr, kr, vr, lam_params, norm_g, lam_init, tq=512, tk=8192):
    b, h, l, _ = qr.shape
    nq = l // tq
    tk = min(tk, l)
    return pl.pallas_call(
        functools.partial(_da_kernel, lam_init, tk),
        out_shape=jax.ShapeDtypeStruct((b * l, BRANCH_W), F32),
        grid=(b, h, nq),
        in_specs=[
            pl.BlockSpec((4, DA_HD), lambda bi, hi, qi: (0, 0)),
            pl.BlockSpec((1, DA_VD), lambda bi, hi, qi: (0, 0)),
            pl.BlockSpec((None, None, tq, DA_VD), lambda bi, hi, qi: (bi, hi, qi, 0)),
            pl.BlockSpec((None, None, l, DA_VD), lambda bi, hi, qi: (bi, hi, 0, 0)),
            pl.BlockSpec((None, None, l, DA_VD), lambda bi, hi, qi: (bi, hi, 0, 0)),
        ],
        out_specs=pl.BlockSpec((tq, DA_VD), lambda bi, hi, qi: (bi * nq + qi, hi)),
        compiler_params=_cparams(("parallel", "parallel", "arbitrary")),
        name="diff_attention",
    )(lam_params, norm_g.reshape(1, DA_VD), qr, kr, vr)


def _mem_attn_kernel(q_ref, kv_ref, o_ref):
    q = q_ref[...]
    kv = kv_ref[...]
    scale = MEM_HD ** -0.5
    for h in range(MEM_HEADS):
        qh = q[:, h * MEM_HD:(h + 1) * MEM_HD].astype(BF16)
        kh = kv[:, h * MEM_HD:(h + 1) * MEM_HD].astype(BF16)
        vh = kv[:, BRANCH_W + h * MEM_HD:BRANCH_W + (h + 1) * MEM_HD].astype(BF16)
        s = lax.dot_general(qh, kh, (((1,), (1,)), ((), ())), preferred_element_type=F32) * scale
        e = jnp.exp(s - jnp.max(s, axis=-1, keepdims=True))
        p = e / jnp.sum(e, axis=-1, keepdims=True)
        o_ref[:, h * MEM_HD:(h + 1) * MEM_HD] = jnp.dot(p.astype(BF16), vh, preferred_element_type=F32)


def _memory_attention(z, kv, b, l, tm=512):
    nm = l // tm
    m = kv.shape[1]
    return pl.pallas_call(
        _mem_attn_kernel,
        out_shape=jax.ShapeDtypeStruct((b * l, BRANCH_W), F32),
        grid=(b, nm),
        in_specs=[
            pl.BlockSpec((tm, ME_COLS), lambda bi, i: (bi * nm + i, ME_OFF // ME_COLS)),
            pl.BlockSpec((None, m, 2 * BRANCH_W), lambda bi, i: (bi, 0, 0)),
        ],
        out_specs=pl.BlockSpec((tm, BRANCH_W), lambda bi, i: (bi * nm + i, 0)),
        compiler_params=_cparams(("parallel", "arbitrary")),
        name="memory_attention",
    )(z, kv)


DFT_NA = 128
DFT_NB = 128
HY_LBLK = 4096
HY_KB = 8
HIGHEST = lax.Precision.HIGHEST


def _dft_tables(l):
    assert 2 * l == DFT_NA * DFT_NB
    n = DFT_NA * DFT_NB
    i = jnp.arange(DFT_NA, dtype=jnp.int32)
    ang1 = (2.0 * math.pi / DFT_NA) * ((i[:, None] * i[None, :]) % DFT_NA).astype(F32)
    c1, s1 = jnp.cos(ang1), jnp.sin(ang1)
    half = DFT_NA // 2
    kk = i[:, None, None] + DFT_NA * i[None, :, None]
    ang = (2.0 * math.pi / n) * ((kk * i[None, None, :]) % n).astype(F32)
    gr, gi = jnp.cos(ang), -jnp.sin(ang)
    t = dict(
        f=jnp.concatenate([c1[:, :half], -s1[:, :half]], axis=0),
        gr=gr, gi=gi,
        gg=jnp.concatenate([gr, gi], axis=1),
        hh=jnp.concatenate([gr.transpose(0, 2, 1), gi.transpose(0, 2, 1)], axis=1),
        er=c1.T[:half] / n, ei=s1.T[:half] / n,
    )
    for name in ('f', 'gg', 'hh', 'er', 'ei'):
        t[name + '_bf'] = t[name].astype(BF16)
    return t


def _dft1_kernel(prec, x_ref, f_ref, ar_ref, ai_ref):
    x = x_ref[...]
    f = f_ref[...]
    if prec is None:
        a = jnp.dot(f, x.astype(BF16), preferred_element_type=F32)
    else:
        a = jnp.dot(f, x, precision=prec, preferred_element_type=F32)
    ar_ref[...] = a[:DFT_NA].astype(ar_ref.dtype)
    ai_ref[...] = a[DFT_NA:].astype(ai_ref.dtype)


def _dft_stage1(x, f, out_dtype, prec):
    g, r, w = x.shape
    spec_o = pl.BlockSpec((None, DFT_NA, HY_LBLK), lambda gi, j: (gi, 0, j))
    return pl.pallas_call(
        functools.partial(_dft1_kernel, prec),
        out_shape=[jax.ShapeDtypeStruct((g, DFT_NA, w), out_dtype)] * 2,
        grid=(g, w // HY_LBLK),
        in_specs=[pl.BlockSpec((None, r, HY_LBLK), lambda gi, j: (gi, 0, j)),
                  pl.BlockSpec((2 * DFT_NA, r), lambda gi, j: (0, 0))],
        out_specs=[spec_o, spec_o],
        compiler_params=_cparams(("parallel", "parallel")),
        name="hyena_dft_stage1",
    )(x, f)


def _filter_mlp_kernel(l, feats_ref, w1_ref, b1_ref, w2_ref, b2_ref, w3_ref, b3_ref, fq_ref, dl_ref, h_ref, nrm_ref):
    i = pl.program_id(0)
    tm = feats_ref.shape[0]
    dot = functools.partial(jnp.dot, precision=HIGHEST, preferred_element_type=F32)
    h = jnp.sin(fq_ref[0:1] * (dot(feats_ref[...], w1_ref[...]) + b1_ref[...]))
    h = jnp.sin(fq_ref[1:2] * (dot(h, w2_ref[...]) + b2_ref[...]))
    h = dot(h, w3_ref[...]) + b3_ref[...]
    row = i * tm + lax.broadcasted_iota(jnp.int32, (tm, 1), 0)
    t = row.astype(F32) * (1.0 / (l - 1))
    decay = jnp.exp(-t * dl_ref[...])
    h = h * jnp.concatenate([decay] * (2 * HY_ORDER), axis=1)
    col = lax.broadcasted_iota(jnp.int32, (1, 2 * HY_ORDER * HY_W), 1)
    bwd = jnp.bitwise_and(lax.shift_right_logical(col, HY_W.bit_length() - 1), 1) == 1
    h = jnp.where(jnp.logical_and(row == 0, bwd), 0.0, h)
    h_ref[...] = h

    @pl.when(i == 0)
    def _():
        nrm_ref[...] = jnp.zeros_like(nrm_ref)

    nrm_ref[...] += jnp.sum(jnp.abs(h), axis=0, keepdims=True)


def _filter_mlp(feats, w1, b1, w2, b2, w3, b3, freq, deltas, tm=1024):
    l = feats.shape[0]
    wcols = 2 * HY_ORDER * HY_W
    full = lambda a: pl.BlockSpec(a.shape, lambda i: (0,) * a.ndim)
    args = (feats, w1, b1.reshape(1, -1), w2, b2.reshape(1, -1), w3, b3.reshape(1, -1), freq, deltas.reshape(1, -1))
    return pl.pallas_call(
        functools.partial(_filter_mlp_kernel, l),
        out_shape=[jax.ShapeDtypeStruct((l, wcols), F32), jax.ShapeDtypeStruct((1, wcols), F32)],
        grid=(l // tm,),
        in_specs=[pl.BlockSpec((tm, feats.shape[1]), lambda i: (i, 0))] + [full(a) for a in args[1:]],
        out_specs=[pl.BlockSpec((tm, wcols), lambda i: (i, 0)), pl.BlockSpec((1, wcols), lambda i: (0, 0))],
        compiler_params=_cparams(("arbitrary",)),
        name="hyena_filter_mlp",
    )(*args)


def _filter_spec_kernel(afr_ref, afi_ref, abr_ref, abi_ref, gr_ref, gi_ref, nf_ref, nb_ref, kr_ref, ki_ref):
    dot = functools.partial(jnp.dot, precision=HIGHEST, preferred_element_type=F32)
    inv = 1.0 / (nf_ref[...] + nb_ref[...])
    for j in range(gr_ref.shape[0]):
        gr, gi = gr_ref[j], gi_ref[j]
        sr = afr_ref[j] + abr_ref[j]
        si = afi_ref[j] + abi_ref[j]
        dr = afr_ref[j] - abr_ref[j]
        di = afi_ref[j] - abi_ref[j]
        kr_ref[j] = (dot(gr, sr) - dot(gi, si)) * inv
        ki_ref[j] = (dot(gr, di) + dot(gi, dr)) * inv


def _filter_spectrum(ar, ai, gr, gi, nrm, kb=4, cw=256):
    ncw = HY_W // cw

    def a_spec(d):
        return pl.BlockSpec((kb, DFT_NB, cw), lambda o, ki, ci: (ki, 0, (2 * o + d) * ncw + ci))

    def n_spec(d):
        return pl.BlockSpec((1, cw), lambda o, ki, ci: (0, (2 * o + d) * ncw + ci))

    g_spec = pl.BlockSpec((kb, DFT_NB, DFT_NB), lambda o, ki, ci: (ki, 0, 0))
    o_spec = pl.BlockSpec((None, kb, DFT_NB, cw), lambda o, ki, ci: (o, ki, 0, ci))
    return pl.pallas_call(
        _filter_spec_kernel,
        out_shape=[jax.ShapeDtypeStruct((HY_ORDER, DFT_NA, DFT_NB, HY_W), F32)] * 2,
        grid=(HY_ORDER, DFT_NA // kb, ncw),
        in_specs=[a_spec(0), a_spec(0), a_spec(1), a_spec(1), g_spec, g_spec, n_spec(0), n_spec(1)],
        out_specs=[o_spec, o_spec],
        compiler_params=_cparams(("parallel", "parallel", "parallel")),
        name="hyena_filter_spectrum",
    )(ar, ai, ar, ai, gr, gi, nrm, nrm)


def _hy_mid_kernel(ar_ref, ai_ref, gg_ref, hh_ref, kr_ref, ki_ref, br_ref, bi_ref):
    for j in range(HY_KB):
        gg = gg_ref[j]
        p = jnp.dot(gg, ar_ref[j], preferred_element_type=F32)
        q = jnp.dot(gg, ai_ref[j], preferred_element_type=F32)
        xr = p[:DFT_NB] - q[DFT_NB:]
        xi = q[:DFT_NB] + p[DFT_NB:]
        kr, ki = kr_ref[j], ki_ref[j]
        yr = (xr * kr - xi * ki).astype(BF16)
        yi = (xr * ki + xi * kr).astype(BF16)
        hh = hh_ref[j]
        p = jnp.dot(hh, yr, preferred_element_type=F32)
        q = jnp.dot(hh, yi, preferred_element_type=F32)
        br_ref[j] = (p[:DFT_NB] + q[DFT_NB:]).astype(BF16)
        bi_ref[j] = (q[:DFT_NB] - p[DFT_NB:]).astype(BF16)


def _hy_mid(ar, ai, gg, hh, kr, ki):
    b = ar.shape[0]
    a_spec = pl.BlockSpec((None, HY_KB, DFT_NB, HY_W), lambda ki_, bi: (bi, ki_, 0, 0))
    t_spec = pl.BlockSpec((HY_KB, 2 * DFT_NB, DFT_NB), lambda ki_, bi: (ki_, 0, 0))
    k_spec = pl.BlockSpec((HY_KB, DFT_NB, HY_W), lambda ki_, bi: (ki_, 0, 0))
    return pl.pallas_call(
        _hy_mid_kernel,
        out_shape=[jax.ShapeDtypeStruct(ar.shape, BF16)] * 2,
        grid=(DFT_NA // HY_KB, b),
        in_specs=[a_spec, a_spec, t_spec, t_spec, k_spec, k_spec],
        out_specs=[a_spec, a_spec],
        compiler_params=_cparams(("parallel", "arbitrary")),
        name="hyena_dft_mid",
    )(ar, ai, gg, hh, kr, ki)


def _hy_out_kernel(br_ref, bi_ref, er_ref, ei_ref, u_ref, gate_ref, skip_ref, o_ref):
    y = (jnp.dot(er_ref[...], br_ref[...], preferred_element_type=F32)
         - jnp.dot(ei_ref[...], bi_ref[...], preferred_element_type=F32))
    o_ref[...] = gate_ref[...] * (y + u_ref[...] * skip_ref[...])


def _hy_out(br, bi, er, ei, u, gate, skip_row):
    b, r, w = u.shape
    row = pl.BlockSpec((None, r, HY_LBLK), lambda bi_, j: (bi_, 0, j))
    bspec = pl.BlockSpec((None, DFT_NA, HY_LBLK), lambda bi_, j: (bi_, 0, j))
    espec = pl.BlockSpec((r, DFT_NA), lambda bi_, j: (0, 0))
    return pl.pallas_call(
        _hy_out_kernel,
        out_shape=jax.ShapeDtypeStruct(u.shape, F32),
        grid=(b, w // HY_LBLK),
        in_specs=[bspec, bspec, espec, espec, row, row, pl.BlockSpec((1, HY_LBLK), lambda bi_, j: (0, 0))],
        out_specs=row,
        compiler_params=_cparams(("parallel", "parallel")),
        name="hyena_dft_out",
    )(br, bi, er, ei, u, gate, skip_row)


def _short_conv_kernel(z_ref, w_ref, b_ref, o_ref):
    z = z_ref[...]
    l = z.shape[0]
    row = lax.broadcasted_iota(jnp.int32, (l, 1), 0)
    prev = jnp.where(row == 0, 0.0, pltpu.roll(z, 1, 0))
    nxt = jnp.where(row == l - 1, 0.0, pltpu.roll(z, l - 1, 0))
    w = w_ref[...]
    o_ref[...] = prev * w[0:1] + z * w[1:2] + nxt * w[2:3] + b_ref[...]


def _short_conv(z, w, bias, b, l):
    nc = HY_W // LANE
    return pl.pallas_call(
        _short_conv_kernel,
        out_shape=jax.ShapeDtypeStruct((3, b * l, HY_W), F32),
        grid=(b, 3, nc),
        in_specs=[pl.BlockSpec((l, LANE), lambda bi, p, ci: (bi, p * nc + ci)),
                  pl.BlockSpec((3, LANE), lambda bi, p, ci: (0, p * nc + ci)),
                  pl.BlockSpec((1, LANE), lambda bi, p, ci: (0, p * nc + ci))],
        out_specs=pl.BlockSpec((None, l, LANE), lambda bi, p, ci: (p, bi, ci)),
        compiler_params=_cparams(("parallel", "parallel", "parallel")),
        name="hyena_short_conv",
    )(z, w, bias.reshape(1, -1))


def _hyena_filter_spectrum(l, w1, b1, w2, b2, w3, b3, freq, tabs):
    t = jnp.linspace(0.0, 1.0, l, dtype=F32)[:, None]
    w = 2.0 * math.pi * jnp.arange(l, dtype=F32)[:, None] / l
    fr = jnp.linspace(1e-4, HY_BANDS - 1, HY_BANDS, dtype=F32)[None, :]
    feats = jnp.concatenate([t, jnp.cos(fr * w), -jnp.sin(fr * w)], axis=-1)
    deltas = jnp.abs(jnp.linspace(HY_MIN_DECAY, HY_MAX_DECAY, HY_W, dtype=F32))
    h, nrm = _filter_mlp(feats, w1, b1, w2, b2, w3, b3, freq, deltas)
    wcols = h.shape[1]
    ar, ai = _dft_stage1(h.reshape(1, l // DFT_NB, DFT_NB * wcols), tabs['f'], F32, HIGHEST)
    return _filter_spectrum(ar.reshape(DFT_NA, DFT_NB, wcols), ai.reshape(DFT_NA, DFT_NB, wcols),
                            tabs['gr'], tabs['gi'], nrm)


def _hyena(z, short_w, short_b, skip, kr, ki, tabs, b, l):
    u = _short_conv(z, short_w, short_b, b, l)
    rows = l // DFT_NB
    view = lambda a: a.reshape(b, rows, DFT_NB * HY_W)
    v, gates = view(u[0]), (view(u[1]), view(u[2]))
    for o in range(HY_ORDER):
        ar, ai = _dft_stage1(v, tabs['f_bf'], BF16, None)
        shp4 = (b, DFT_NA, DFT_NB, HY_W)
        br, bi = _hy_mid(ar.reshape(shp4), ai.reshape(shp4), tabs['gg_bf'], tabs['hh_bf'], kr[o], ki[o])
        shp3 = (b, DFT_NA, DFT_NB * HY_W)
        skip_row = jnp.tile(skip[o], HY_LBLK // HY_W).reshape(1, HY_LBLK)
        v = _hy_out(br.reshape(shp3), bi.reshape(shp3), tabs['er_bf'], tabs['ei_bf'], v, gates[o], skip_row)
    return v.reshape(b * l, HY_W)


HG_REC_CHUNK = 64
HG_SUB = 16
HG_TILE = 512
HG_HEADS_PER_STEP = 2
HG_UNROLL = 1


def _gate_terms(z, lb):
    a = jnp.log(lb)
    b = jnp.log1p(-lb) + (jnp.minimum(z, 0.0) - jnp.log1p(jnp.exp(-jnp.abs(z))))
    log_f = jnp.maximum(a, b) + jnp.log1p(jnp.exp(-jnp.abs(a - b)))
    return log_f, (1.0 - lb) * jax.nn.sigmoid(-z)


SUBLANES = 8


def _running_sum_rows(x, fwd):
    r = x.shape[0]
    row = jnp.bitwise_and(lax.broadcasted_iota(jnp.int32, (r, 1), 0), SUBLANES - 1)
    s = 1
    while s < SUBLANES:
        if fwd:
            x = x + jnp.where(row >= s, pltpu.roll(x, s, 0), 0.0)
        else:
            x = x + jnp.where(row < SUBLANES - s, pltpu.roll(x, r - s, 0), 0.0)
        s *= 2
    groups = [x[i:i + SUBLANES] for i in range(0, r, SUBLANES)]
    order = range(1, len(groups)) if fwd else range(len(groups) - 2, -1, -1)
    for i in order:
        prev = groups[i - 1][SUBLANES - 1:SUBLANES] if fwd else groups[i + 1][0:1]
        groups[i] = groups[i] + prev
    return jnp.concatenate(groups, axis=0)


def _gla_chunk(q, k, v, g, st, fwd):
    ch = q.shape[0]
    b = _running_sum_rows(g, fwd)
    nsb = ch // HG_SUB
    parts = [None] * nsb

    def add(i, val):
        parts[i] = val if parts[i] is None else parts[i] + val

    vb = v.astype(BF16)
    li = lax.broadcasted_iota(jnp.int32, (ch, HG_SUB), 1)
    si = jnp.bitwise_and(lax.broadcasted_iota(jnp.int32, (ch, HG_SUB), 0), HG_SUB - 1)
    keep = (li <= si) if fwd else (li >= si)
    a = jnp.zeros((ch, HG_SUB), F32)
    for s in range(HG_SUB):
        src = [i * HG_SUB + s for i in range(nsb)]
        bs = jnp.concatenate([jnp.broadcast_to(b[r:r + 1], (HG_SUB, LANE)) for r in src], axis=0)
        ks = jnp.concatenate([jnp.broadcast_to(k[r:r + 1], (HG_SUB, LANE)) for r in src], axis=0)
        e = jnp.exp(jnp.minimum(b - bs, 0.0))
        a = jnp.where(li == s, jnp.sum(q * e * ks, axis=-1, keepdims=True), a)
    a = jnp.where(keep, a, 0.0).astype(BF16)
    for i in range(nsb):
        sl = slice(i * HG_SUB, (i + 1) * HG_SUB)
        add(i, jnp.dot(a[sl], vb[sl], preferred_element_type=F32))
    h = HG_SUB
    while h < ch:
        for j in range(ch // (2 * h)):
            lo = slice(2 * h * j, 2 * h * j + h)
            hi = slice(2 * h * j + h, 2 * h * j + 2 * h)
            if fwd:
                rows, cols, ref = hi, lo, b[2 * h * j + h - 1:2 * h * j + h]
            else:
                rows, cols, ref = lo, hi, b[2 * h * j + h:2 * h * j + h + 1]
            qt = (q[rows] * jnp.exp(b[rows] - ref)).astype(BF16)
            kt = (k[cols] * jnp.exp(ref - b[cols])).astype(BF16)
            a = lax.dot_general(qt, kt, (((1,), (1,)), ((), ())), preferred_element_type=F32)
            ov = jnp.dot(a.astype(BF16), vb[cols], preferred_element_type=F32)
            for ii in range(h // HG_SUB):
                add(rows.start // HG_SUB + ii, ov[ii * HG_SUB:(ii + 1) * HG_SUB])
        h *= 2
    o = jnp.concatenate(parts, axis=0)
    qh = (q * jnp.exp(b)).astype(BF16)
    o = o + lax.dot_general(qh, st.astype(BF16), (((1,), (1,)), ((), ())), preferred_element_type=F32)
    btot = b[ch - 1:ch] if fwd else b[0:1]
    kh = (k * jnp.exp(btot - b)).astype(BF16)
    st_new = st * jnp.exp(btot) + lax.dot_general(vb, kh, (((0,), (0,)), ((), ())), preferred_element_type=F32)
    return o, st_new


def _hgrn2_kernel(lbf_ref, lbb_ref, ng_ref, zqf_ref, zif_ref, zff_ref, zgf_ref, zqb_ref, zib_ref, zbb_ref, zgb_ref,
                  o_ref, st_ref):
    c = pl.program_id(2)
    n = pl.num_programs(2)
    nch = HG_TILE // HG_REC_CHUNK

    @pl.when(c == 0)
    def _():
        st_ref[...] = jnp.zeros_like(st_ref)

    def body(second_pass, i, carry):
        off_f = pl.multiple_of(i * HG_REC_CHUNK, HG_REC_CHUNK)
        off_b = pl.multiple_of((nch - 1 - i) * HG_REC_CHUNK, HG_REC_CHUNK)
        for fwd, off, zq_ref, zi_ref, zd_ref, zg_ref, lb_ref, tile in (
                (True, off_f, zqf_ref, zif_ref, zff_ref, zgf_ref, lbf_ref, c),
                (False, off_b, zqb_ref, zib_ref, zbb_ref, zgb_ref, lbb_ref, n - 1 - c)):
            sl = pl.ds(off, HG_REC_CHUNK)
            rows = pl.ds(pl.multiple_of(tile * HG_TILE + off, HG_REC_CHUNK), HG_REC_CHUNK)
            d = 0 if fwd else 1
            for hh in range(HG_HEADS_PER_STEP):
                ln = slice(hh * LANE, (hh + 1) * LANE)
                g, k = _gate_terms(zd_ref[sl, ln], lb_ref[:, ln])
                o, st_new = _gla_chunk(jax.nn.silu(zq_ref[sl, ln]), k, zi_ref[sl, ln], g, st_ref[d, hh], fwd)
                st_ref[d, hh] = st_new
                if second_pass:
                    tot = o_ref[rows, ln] + o
                    tot = tot * lax.rsqrt(jnp.mean(tot * tot, axis=-1, keepdims=True) + 1e-6) * ng_ref[...]
                    o_ref[rows, ln] = tot * jax.nn.silu(zg_ref[sl, ln])
                else:
                    o_ref[rows, ln] = o
        return carry

    @pl.when(c < n // 2)
    def _():
        lax.fori_loop(0, nch, functools.partial(body, False), 0, unroll=HG_UNROLL)

    @pl.when(c >= n // 2)
    def _():
        lax.fori_loop(0, nch, functools.partial(body, True), 0, unroll=HG_UNROLL)


def _hgrn2(z, lb_fwd, lb_bwd, norm_g, b, l):
    n = l // HG_TILE
    assert n % 2 == 0
    w = HG_HEADS_PER_STEP * LANE
    base = HY_COLS // w
    nh = BRANCH_W // w

    def zspec(part, rev):
        if rev:
            return pl.BlockSpec((HG_TILE, w), lambda bi, hi, ci: (bi * n + n - 1 - ci, base + part * nh + hi))
        return pl.BlockSpec((HG_TILE, w), lambda bi, hi, ci: (bi * n + ci, base + part * nh + hi))

    lbspec = pl.BlockSpec((1, w), lambda bi, hi, ci: (0, hi))
    return pl.pallas_call(
        _hgrn2_kernel,
        out_shape=jax.ShapeDtypeStruct((b * l, BRANCH_W), F32),
        grid=(b, nh, n),
        in_specs=[lbspec, lbspec, pl.BlockSpec((1, LANE), lambda bi, hi, ci: (0, 0)),
                  zspec(0, False), zspec(1, False), zspec(2, False), zspec(4, False),
                  zspec(0, True), zspec(1, True), zspec(3, True), zspec(4, True)],
        out_specs=pl.BlockSpec((l, w), lambda bi, hi, ci: (bi, hi)),
        scratch_shapes=[pltpu.VMEM((2, HG_HEADS_PER_STEP, HG_DV, HG_DK), F32)],
        compiler_params=_cparams(("parallel", "parallel", "arbitrary")),
        name="hgrn2_scan",
    )(lb_fwd.reshape(1, BRANCH_W), lb_bwd.reshape(1, BRANCH_W), norm_g.reshape(1, HG_DV),
      z, z, z, z, z, z, z, z)


def _merge_kernel(x_ref, o0_ref, o1_ref, o2_ref, o3_ref, wg_ref, bg_ref, wbr_ref, wout_ref, g_ref, b_ref,
                  out_ref, acc_ref, xb_ref, ob_ref):
    j = pl.program_id(1)

    @pl.when(j == 0)
    def _():
        xb_ref[...] = x_ref[...].astype(BF16)
        for bi, o_ref in enumerate((o0_ref, o1_ref, o2_ref, o3_ref)):
            ob_ref[bi] = o_ref[...].astype(BF16)
        acc_ref[...] = jnp.zeros_like(acc_ref)

    xb = xb_ref[...]
    merged = None
    for bi in range(N_BRANCH):
        gate = jax.nn.sigmoid(jnp.dot(xb, wg_ref[bi], preferred_element_type=F32) + bg_ref[bi])
        term = gate * jnp.dot(ob_ref[bi], wbr_ref[bi], preferred_element_type=F32)
        merged = term if merged is None else merged + term
    acc_ref[...] += jnp.dot(merged.astype(BF16), wout_ref[...], preferred_element_type=F32)

    @pl.when(j == pl.num_programs(1) - 1)
    def _():
        out_ref[...] = _ln_rows(DN_ALPHA * x_ref[...] + acc_ref[...], g_ref[...], b_ref[...])


def _merge(x, branches, w_gate, b_gate, w_br, w_out, ln_g, ln_b, tm=512, tn=256):
    t, d = x.shape
    row = pl.BlockSpec((tm, d), lambda i, j: (i, 0))
    brow = pl.BlockSpec((tm, BRANCH_W), lambda i, j: (i, 0))
    vec = pl.BlockSpec((1, d), lambda i, j: (0, 0))
    return pl.pallas_call(
        _merge_kernel,
        out_shape=jax.ShapeDtypeStruct((t, d), F32),
        grid=(t // tm, d // tn),
        in_specs=[
            row, brow, brow, brow, brow,
            pl.BlockSpec((N_BRANCH, d, tn), lambda i, j: (0, 0, j)),
            pl.BlockSpec((N_BRANCH, 1, tn), lambda i, j: (0, 0, j)),
            pl.BlockSpec((N_BRANCH, BRANCH_W, tn), lambda i, j: (0, 0, j)),
            pl.BlockSpec((tn, d), lambda i, j: (j, 0)),
            vec, vec,
        ],
        out_specs=row,
        scratch_shapes=[pltpu.VMEM((tm, d), F32), pltpu.VMEM((tm, d), BF16),
                        pltpu.VMEM((N_BRANCH, tm, BRANCH_W), BF16)],
        compiler_params=_cparams(("parallel", "arbitrary")),
        name="gated_merge",
    )(x, *branches, w_gate, b_gate.reshape(N_BRANCH, 1, d), w_br, w_out, ln_g.reshape(1, d), ln_b.reshape(1, d))


def _expert_kernel(x_ref, gate_ref, w1_ref, w3_ref, w2_ref, o_ref):
    f = pl.program_id(2)
    x = x_ref[...]
    h1 = jnp.dot(x, w1_ref[...], preferred_element_type=F32)
    h3 = jnp.dot(x, w3_ref[...], preferred_element_type=F32)
    h = (jax.nn.silu(h1) * h3).astype(BF16)
    part = jnp.dot(h, w2_ref[...], preferred_element_type=F32)

    @pl.when(f == 0)
    def _():
        o_ref[...] = part

    @pl.when(f != 0)
    def _():
        o_ref[...] += part

    @pl.when(f == pl.num_programs(2) - 1)
    def _():
        o_ref[...] = o_ref[...] * gate_ref[...]


def _expert_ffn(xe, gate, w1, w3, w2, tm=512, tf=1024):
    e, c, d = xe.shape
    ff = w1.shape[2]
    tm = min(tm, c)
    return pl.pallas_call(
        _expert_kernel,
        out_shape=jax.ShapeDtypeStruct((e, c, d), F32),
        grid=(e, c // tm, ff // tf),
        in_specs=[
            pl.BlockSpec((None, tm, d), lambda ei, i, f: (ei, i, 0)),
            pl.BlockSpec((None, tm, 1), lambda ei, i, f: (ei, i, 0)),
            pl.BlockSpec((None, d, tf), lambda ei, i, f: (ei, 0, f)),
            pl.BlockSpec((None, d, tf), lambda ei, i, f: (ei, 0, f)),
            pl.BlockSpec((None, tf, d), lambda ei, i, f: (ei, f, 0)),
        ],
        out_specs=pl.BlockSpec((None, tm, d), lambda ei, i, f: (ei, i, 0)),
        compiler_params=_cparams(("parallel", "parallel", "arbitrary")),
        name="expert_ffn",
    )(xe, gate, w1, w3, w2)


RT_TM = 1024
RT_WIN = 32
RT_EAGER = 2
RT_CHUNKS = 4


def _router_kernel(x_ref, w_ref, o_ref):
    logits = lax.dot_general(w_ref[...], x_ref[...].astype(BF16), (((1,), (1,)), ((), ())),
                             preferred_element_type=F32)
    e = jnp.exp(logits - jnp.max(logits, axis=0, keepdims=True))
    aff = e / jnp.sum(e, axis=0, keepdims=True)
    for c in range(x_ref.shape[0] // LANE):
        o_ref[:, c, :] = aff[:, c * LANE:(c + 1) * LANE]


def _router(x, w_t):
    t, d = x.shape
    return pl.pallas_call(
        _router_kernel,
        out_shape=jax.ShapeDtypeStruct((N_EXPERTS, t // LANE, LANE), F32),
        grid=(t // RT_TM,),
        in_specs=[pl.BlockSpec((RT_TM, d), lambda i: (i, 0)), pl.BlockSpec((N_EXPERTS, d), lambda i: (0, 0))],
        out_specs=pl.BlockSpec((N_EXPERTS, RT_TM // LANE, LANE), lambda i: (0, i, 0)),
        compiler_params=_cparams(("parallel",)),
        name="moe_router",
    )(x, w_t)


def _prefix_count(mask):
    nch = mask.shape[0]
    m = jnp.where(mask, 1.0, 0.0)
    li = lax.broadcasted_iota(jnp.int32, (LANE, LANE), 0)
    lj = lax.broadcasted_iota(jnp.int32, (LANE, LANE), 1)
    incl = jnp.dot(m.astype(BF16), jnp.where(li <= lj, 1.0, 0.0).astype(BF16), preferred_element_type=F32)
    tot = jnp.broadcast_to(incl[:, LANE - 1:LANE], (nch, LANE))
    ri = lax.broadcasted_iota(jnp.int32, (nch, nch), 0)
    ci = lax.broadcasted_iota(jnp.int32, (nch, nch), 1)
    start = jnp.dot(jnp.where(ci < ri, 1.0, 0.0).astype(BF16), tot.astype(BF16), preferred_element_type=F32)
    return incl - m + start, start


def _route_select_kernel(cap, a_ref, pos_ref, start_ref):
    bits = pltpu.bitcast(a_ref[...], jnp.int32)

    def step(i, tau):
        cand = jnp.bitwise_or(tau, lax.shift_left(jnp.int32(1), 30 - i))
        cnt = jnp.sum(jnp.where(bits >= cand, 1.0, 0.0))
        return jnp.where(cnt >= cap, cand, tau)

    tau = lax.fori_loop(0, 31, step, jnp.int32(0))
    gt = bits > tau
    eq = bits == tau
    need = cap - jnp.sum(jnp.where(gt, 1.0, 0.0))
    eq_rank, _ = _prefix_count(eq)
    sel = jnp.logical_or(gt, jnp.logical_and(eq, eq_rank < need))
    pos, start = _prefix_count(sel)
    pos_ref[...] = jnp.where(sel, pos, -1.0)
    start_ref[...] = start


def _route_select(aff3, cap):
    e, nch, _ = aff3.shape
    spec = pl.BlockSpec((None, nch, LANE), lambda ei: (ei, 0, 0))
    return pl.pallas_call(
        functools.partial(_route_select_kernel, float(cap)),
        out_shape=[jax.ShapeDtypeStruct(aff3.shape, F32)] * 2,
        grid=(e,),
        in_specs=[spec],
        out_specs=[spec, spec],
        compiler_params=_cparams(("parallel",)),
        name="moe_route_select",
    )(aff3)


def _route_compact_kernel(cap, start_sm, pos_ref, a_ref, idx_ref, gate_ref, il_ref, gl_ref):
    ei = pl.program_id(0)
    nch = pos_ref.shape[0]
    il_ref[...] = jnp.zeros_like(il_ref)
    gl_ref[...] = jnp.zeros_like(gl_ref)
    s_iota = lax.broadcasted_iota(jnp.int32, (RT_WIN, LANE), 0).astype(F32)
    lane = lax.broadcasted_iota(jnp.int32, (1, LANE), 1).astype(F32)
    log_win = RT_WIN.bit_length() - 1

    def body(i, carry):
        rest = []
        for u in range(RT_CHUNKS):
            j = i * RT_CHUNKS + u
            first = start_sm[ei, j]
            nxt = jnp.where(j + 1 < nch, start_sm[ei, jnp.minimum(j + 1, nch - 1)], cap)
            base = lax.shift_left(lax.shift_right_logical(first, log_win), log_win)
            nwin = lax.shift_right_logical(nxt - base + (RT_WIN - 1), log_win)
            prow = pos_ref[pl.ds(j, 1), :]
            arow = a_ref[pl.ds(j, 1), :]
            tok = lane + jnp.asarray(j * LANE, F32)

            def window(w, c2, base=base, prow=prow, arow=arow, tok=tok):
                lo = pl.multiple_of(base + w * RT_WIN, RT_WIN)
                hit = (prow - jnp.asarray(lo, F32)) == s_iota
                il_ref[pl.ds(lo, RT_WIN), :] += jnp.sum(jnp.where(hit, tok, 0.0), axis=1, keepdims=True)
                gl_ref[pl.ds(lo, RT_WIN), :] += jnp.sum(jnp.where(hit, arow, 0.0), axis=1, keepdims=True)
                return c2

            for w in range(RT_EAGER):
                window(w, 0)
            rest.append((nwin, window))
        for nwin, window in rest:
            lax.fori_loop(RT_EAGER, nwin, window, 0)
        return carry

    lax.fori_loop(0, nch // RT_CHUNKS, body, 0)
    diag = (lax.broadcasted_iota(jnp.int32, (LANE, LANE), 0) == lax.broadcasted_iota(jnp.int32, (LANE, LANE), 1))
    for r in range(idx_ref.shape[0]):
        rows = slice(r * LANE, (r + 1) * LANE)
        idx_ref[r:r + 1, :] = jnp.sum(jnp.where(diag, il_ref[rows, :], 0.0), axis=0, keepdims=True).astype(jnp.int32)
        gate_ref[r:r + 1, :] = jnp.sum(jnp.where(diag, gl_ref[rows, :], 0.0), axis=0, keepdims=True)


def _route_compact(start, pos3, aff3, cap):
    e, nch, _ = pos3.shape
    spec = pl.BlockSpec((None, nch, LANE), lambda ei, s: (ei, 0, 0))
    ospec = pl.BlockSpec((None, cap // LANE, LANE), lambda ei, s: (ei, 0, 0))
    return pl.pallas_call(
        functools.partial(_route_compact_kernel, cap),
        out_shape=[jax.ShapeDtypeStruct((e, cap // LANE, LANE), jnp.int32),
                   jax.ShapeDtypeStruct((e, cap // LANE, LANE), F32)],
        grid_spec=pltpu.PrefetchScalarGridSpec(
            num_scalar_prefetch=1, grid=(e,), in_specs=[spec, spec], out_specs=[ospec, ospec],
            scratch_shapes=[pltpu.VMEM((cap + 2 * LANE, LANE), F32), pltpu.VMEM((cap + 2 * LANE, LANE), F32)]),
        compiler_params=_cparams(("parallel",)),
        name="moe_route_compact",
    )(start, pos3, aff3)


def _expert_choice(x, w_t, cap):
    aff3 = _router(x, w_t)
    pos3, start3 = _route_select(aff3, cap)
    idx3, gate3 = _route_compact(start3[:, :, 0].astype(jnp.int32), pos3, aff3, cap)
    return idx3.reshape(N_EXPERTS, cap), gate3.reshape(N_EXPERTS, cap)


CB_TS = 256
CB_SLOTS = 3


def _combine_kernel(nper, idx_ref, idxn_ref, ye_ref, y_in, y_ref, buf, sem_r, sem_w):
    del y_in
    g = pl.program_id(0)
    ng = pl.num_programs(0)
    i = lax.rem(g, nper)
    slot = lax.rem(g, CB_SLOTS)
    nslot = lax.rem(g + 1, CB_SLOTS)
    pslot = lax.rem(g + CB_SLOTS - 1, CB_SLOTS)
    first = i == 0

    def row_read(row, s, j):
        return pltpu.make_async_copy(y_ref.at[pl.ds(row, 1)], buf.at[s, pl.ds(j, 1)], sem_r.at[s])

    def row_write(row, s, j):
        return pltpu.make_async_copy(buf.at[s, pl.ds(j, 1)], y_ref.at[pl.ds(row, 1)], sem_w.at[s])

    def for_rows(fn):
        lax.fori_loop(0, CB_TS, lambda j, c: (fn(j), c)[1], 0, unroll=8)

    def start_reads(ids_ref, s):
        for_rows(lambda j: row_read(ids_ref[0, j], s, j).start())

    def wait_reads(s):
        for_rows(lambda j: row_read(0, s, 0).wait())

    def wait_writes(s):
        for_rows(lambda j: row_write(0, s, 0).wait())

    @pl.when(jnp.logical_and(g >= 2, i != 1))
    def _():
        wait_writes(nslot)

    @pl.when(jnp.logical_and(first, g >= 1))
    def _():
        wait_writes(pslot)

    @pl.when(first)
    def _():
        start_reads(idx_ref, slot)

    @pl.when(i != nper - 1)
    def _():
        start_reads(idxn_ref, nslot)

    wait_reads(slot)
    buf[slot] = buf[slot] + ye_ref[...]
    for_rows(lambda j: row_write(idx_ref[0, j], slot, j).start())

    @pl.when(g == ng - 1)
    def _():
        wait_writes(pslot)
        wait_writes(slot)


def _moe_combine(idx, ye, t):
    e, cap, d = ye.shape
    nper = cap // CB_TS
    assert nper >= 2
    ng = e * nper
    idx3 = idx.reshape(ng, 1, CB_TS)
    sm_spec = lambda off: pl.BlockSpec((None, 1, CB_TS), lambda g: (jnp.minimum(g + off, ng - 1), 0, 0),
                                       memory_space=pltpu.SMEM)
    return pl.pallas_call(
        functools.partial(_combine_kernel, nper),
        out_shape=jax.ShapeDtypeStruct((t, d), F32),
        grid=(ng,),
        in_specs=[sm_spec(0), sm_spec(1), pl.BlockSpec((CB_TS, d), lambda g: (g, 0)),
                  pl.BlockSpec(memory_space=pl.ANY)],
        out_specs=pl.BlockSpec(memory_space=pl.ANY),
        scratch_shapes=[pltpu.VMEM((CB_SLOTS, CB_TS, d), F32), pltpu.SemaphoreType.DMA((CB_SLOTS,)),
                        pltpu.SemaphoreType.DMA((CB_SLOTS,))],
        input_output_aliases={3: 0},
        compiler_params=_cparams(("arbitrary",)),
        name="moe_combine",
    )(idx3, idx3, ye.reshape(e * cap, d), jnp.zeros((t, d), F32))


def _gather_kernel(idx_ref, idxn_ref, x_ref, o_ref, buf, sem):
    g = pl.program_id(0)
    ng = pl.num_programs(0)
    slot = lax.rem(g, 2)

    def row_read(row, s, j):
        return pltpu.make_async_copy(x_ref.at[pl.ds(row, 1)], buf.at[s, pl.ds(j, 1)], sem.at[s])

    def start_reads(ids_ref, s):
        lax.fori_loop(0, CB_TS, lambda j, c: (row_read(ids_ref[0, j], s, j).start(), c)[1], 0, unroll=8)

    @pl.when(g == 0)
    def _():
        start_reads(idx_ref, slot)

    @pl.when(g + 1 < ng)
    def _():
        start_reads(idxn_ref, 1 - slot)

    lax.fori_loop(0, CB_TS, lambda j, c: (row_read(0, slot, 0).wait(), c)[1], 0, unroll=8)
    o_ref[...] = buf[slot].astype(BF16)


def _moe_gather(x, idx):
    e, cap = idx.shape
    d = x.shape[1]
    ng = e * cap // CB_TS
    idx3 = idx.reshape(ng, 1, CB_TS)
    sm_spec = lambda off: pl.BlockSpec((None, 1, CB_TS), lambda g: (jnp.minimum(g + off, ng - 1), 0, 0),
                                       memory_space=pltpu.SMEM)
    out = pl.pallas_call(
        _gather_kernel,
        out_shape=jax.ShapeDtypeStruct((e * cap, d), BF16),
        grid=(ng,),
        in_specs=[sm_spec(0), sm_spec(1), pl.BlockSpec(memory_space=pl.ANY)],
        out_specs=pl.BlockSpec((CB_TS, d), lambda g: (g, 0)),
        scratch_shapes=[pltpu.VMEM((2, CB_TS, d), F32), pltpu.SemaphoreType.DMA((2,))],
        compiler_params=_cparams(("arbitrary",)),
        name="moe_gather",
    )(idx3, idx3, x)
    return out.reshape(e, cap, d)


def _attn_prep_kernel(zq_ref, zk_ref, zv_ref, cos_ref, sin_ref, q_ref, k_ref, v_ref):
    lane = lax.broadcasted_iota(jnp.int32, (1, LANE), 1)
    first = jnp.bitwise_and(lane, DA_HD - 1) < ROPE_DIM // 2
    cos, sin = cos_ref[...], sin_ref[...]

    def rope(x):
        partner = jnp.where(first, pltpu.roll(x, LANE - ROPE_DIM // 2, 1), pltpu.roll(x, ROPE_DIM // 2, 1))
        return x * cos + partner * sin

    q_ref[...] = (rope(zq_ref[...]) * (DA_HD ** -0.5)).astype(BF16)
    k_ref[...] = rope(zk_ref[...]).astype(BF16)
    v_ref[...] = zv_ref[...].astype(BF16)


def _rope_lane_tables(l):
    inv = ROPE_THETA ** (-jnp.arange(0, ROPE_DIM, 2, dtype=F32) / ROPE_DIM)
    ang = jnp.arange(l, dtype=F32)[:, None] * inv[None, :]
    c, s = jnp.cos(ang), jnp.sin(ang)
    rest = DA_HD - ROPE_DIM
    cos_h = jnp.concatenate([c, c, jnp.ones((l, rest), F32)], axis=1)
    sin_h = jnp.concatenate([-s, s, jnp.zeros((l, rest), F32)], axis=1)
    return jnp.concatenate([cos_h, cos_h], axis=1), jnp.concatenate([sin_h, sin_h], axis=1)


def _attn_prep(z, cos, sin, b, l, tm=1024):
    n = l // tm
    base = DA_OFF // LANE

    def zspec(part):
        return pl.BlockSpec((tm, LANE), lambda bi, hi, i: (bi * n + i, base + part * DA_HEADS + hi))

    tspec = pl.BlockSpec((tm, LANE), lambda bi, hi, i: (i, 0))
    ospec = pl.BlockSpec((None, None, tm, LANE), lambda bi, hi, i: (bi, hi, i, 0))
    return pl.pallas_call(
        _attn_prep_kernel,
        out_shape=[jax.ShapeDtypeStruct((b, DA_HEADS, l, DA_VD), BF16)] * 3,
        grid=(b, DA_HEADS, n),
        in_specs=[zspec(0), zspec(1), zspec(2), tspec, tspec],
        out_specs=[ospec, ospec, ospec],
        compiler_params=_cparams(("parallel", "parallel", "parallel")),
        name="attention_prep",
    )(z, z, z, cos, sin)


def _trunk(x, mem, p):
    B, L, D = x.shape
    T = B * L
    cos, sin = _rope_lane_tables(L)
    lb_all = jnp.cumsum(jax.nn.softmax(p['hg_lb_raw'], axis=1), axis=1)
    lb_all = lb_all - lb_all[:, :1]
    xf = _layer_norm(x.reshape(T, D), p['ln_in_g'], p['ln_in_b'])
    memf = mem.reshape(B * mem.shape[1], D)
    cap = EC_FACTOR * T // N_EXPERTS
    for l in range(DEPTH):
        z = _matmul(xf, p['w_in'][l], 1024, 1024)
        hy_kr, hy_ki = p['hy_spectrum'][L][l]
        o_hy = _hyena(z, p['hy_short_w'][l], p['hy_short_b'][l], p['hy_skip'][l], hy_kr, hy_ki,
                      p['dft_tables'][L], B, L)
        o_hg = _hgrn2(z, lb_all[0, l], lb_all[1, l], p['hg_norm_g'][l], B, L)
        qr, kr, vr = _attn_prep(z, cos, sin, B, L)
        lam_init = 0.8 - 0.6 * math.exp(-0.3 * l)
        o_da = _diff_attention(qr, kr, vr, p['da_lambda'][l], p['da_norm_g'][l], lam_init)
        kv = _matmul(memf, p['w_mem_kv'][l], min(512, memf.shape[0]), 512).reshape(B, -1, 2 * BRANCH_W)
        o_me = _memory_attention(z, kv, B, L)
        xf = _merge(xf, (o_hy, o_hg, o_da, o_me), p['w_gate'][l], p['b_gate'][l], p['w_br'][l], p['w_out'][l],
                    p['ln1_g'][l], p['ln1_b'][l])
        idx, gate = _expert_choice(xf, p['w_router_t'][l], cap)
        xe = _moe_gather(xf, idx)
        ye = _expert_ffn(xe, gate[..., None], p['w_e1'][l], p['w_e3'][l], p['w_e2'][l])
        y = _moe_combine(idx, ye, T)
        xf = _layer_norm(xf, p['ln2_g'][l], p['ln2_b'][l], resid=y, alpha=DN_ALPHA)
    return xf.reshape(B, L, D)


def kernel(x_prompt, x_sample, mem_prompt, mem_sample, ln_in_g, ln_in_b, w_in, hy_short_w, hy_short_b, hy_ffn_w1, hy_ffn_b1, hy_ffn_w2, hy_ffn_b2, hy_ffn_w3, hy_ffn_b3, hy_freq, hy_skip, hg_lb_raw, hg_norm_g, da_lambda, da_norm_g, w_mem_kv, w_gate, b_gate, w_br, w_out, ln1_g, ln1_b, w_router, w_e1, w_e3, w_e2, ln2_g, ln2_b):
    p = dict(ln_in_g=ln_in_g, ln_in_b=ln_in_b, hy_short_w=hy_short_w, hy_short_b=hy_short_b,
             hy_ffn_w1=hy_ffn_w1, hy_ffn_b1=hy_ffn_b1, hy_ffn_w2=hy_ffn_w2, hy_ffn_b2=hy_ffn_b2,
             hy_ffn_w3=hy_ffn_w3, hy_ffn_b3=hy_ffn_b3, hy_freq=hy_freq, hy_skip=hy_skip, hg_lb_raw=hg_lb_raw,
             hg_norm_g=hg_norm_g, da_lambda=da_lambda, da_norm_g=da_norm_g, b_gate=b_gate,
             ln1_g=ln1_g, ln1_b=ln1_b, ln2_g=ln2_g, ln2_b=ln2_b)
    for name, w in (('w_in', w_in), ('w_mem_kv', w_mem_kv), ('w_gate', w_gate), ('w_br', w_br), ('w_out', w_out),
                    ('w_e1', w_e1), ('w_e3', w_e3), ('w_e2', w_e2)):
        p[name] = w.astype(BF16)
    p['w_router_t'] = jnp.swapaxes(w_router, 1, 2).astype(BF16)
    p['dft_tables'], p['hy_spectrum'] = {}, {}
    for seq_len in sorted({x_prompt.shape[1], x_sample.shape[1]}):
        tabs = _dft_tables(seq_len)
        p['dft_tables'][seq_len] = tabs
        p['hy_spectrum'][seq_len] = [
            _hyena_filter_spectrum(seq_len, hy_ffn_w1[l], hy_ffn_b1[l], hy_ffn_w2[l], hy_ffn_b2[l], hy_ffn_w3[l],
                                   hy_ffn_b3[l], hy_freq[l], tabs) for l in range(DEPTH)]
    y_prompt = _trunk(x_prompt, mem_prompt, p)
    y_sample = _trunk(x_sample, mem_sample, p)
    return (y_prompt, y_sample)
```

```python
import functools
import math

import jax
import jax.numpy as jnp
from jax import lax
from jax.experimental import pallas as pl
from jax.experimental.pallas import tpu as pltpu

F32 = jnp.float32
BF16 = jnp.bfloat16

D_MODEL = 2048
DEPTH = 4
N_BRANCH = 4
BRANCH_W = D_MODEL // 4
HY_W = BRANCH_W
HY_ORDER = 2
HY_EMB = 33
HY_BANDS = (HY_EMB - 1) // 2
HY_MIN_DECAY = math.log(1e-2) / 1.5
HY_MAX_DECAY = math.log(1e-2) / 0.3
HG_HEADS = 4
HG_DK = BRANCH_W // HG_HEADS
HG_DV = BRANCH_W // HG_HEADS
HG_CHUNK = 64
DA_HEADS = 4
DA_HD = BRANCH_W // (2 * DA_HEADS)
DA_VD = 2 * DA_HD
ROPE_DIM = DA_HD // 4
ROPE_THETA = 500000.0
MEM_HEADS = 4
MEM_HD = BRANCH_W // MEM_HEADS
N_EXPERTS = 16
EC_FACTOR = 2
EXPERT_FF = 2048
DN_ALPHA = (2 * DEPTH) ** 0.25
LN_EPS = 1e-5
HY_COLS = 3 * HY_W
HG_COLS = 5 * BRANCH_W
DA_COLS = 3 * BRANCH_W
ME_COLS = BRANCH_W
IN_W = HY_COLS + HG_COLS + DA_COLS + ME_COLS
DA_OFF = HY_COLS + HG_COLS
ME_OFF = DA_OFF + DA_COLS

V7X_VMEM_LIMIT_BYTES = 56 * 1024 * 1024
LANE = 128


def _cparams(sem):
    return pltpu.CompilerParams(dimension_semantics=sem, vmem_limit_bytes=V7X_VMEM_LIMIT_BYTES)


def _mm_kernel(a_ref, w_ref, o_ref):
    o_ref[...] = jnp.dot(a_ref[...].astype(BF16), w_ref[...], preferred_element_type=F32)


def _matmul(a, w, tm, tn):
    m, k = a.shape
    n = w.shape[1]
    return pl.pallas_call(
        _mm_kernel,
        out_shape=jax.ShapeDtypeStruct((m, n), F32),
        grid=(m // tm, n // tn),
        in_specs=[pl.BlockSpec((tm, k), lambda i, j: (i, 0)), pl.BlockSpec((k, tn), lambda i, j: (0, j))],
        out_specs=pl.BlockSpec((tm, tn), lambda i, j: (i, j)),
        compiler_params=_cparams(("parallel", "arbitrary")),
        name="dense_matmul",
    )(a, w)


def _ln_rows(y, g, b):
    mu = jnp.mean(y, axis=-1, keepdims=True)
    d = y - mu
    var = jnp.mean(d * d, axis=-1, keepdims=True)
    return d * lax.rsqrt(var + LN_EPS) * g + b


def _ln_kernel(alpha, x_ref, r_ref, g_ref, b_ref, o_ref):
    y = x_ref[...]
    if r_ref is not None:
        y = alpha * y + r_ref[...]
    o_ref[...] = _ln_rows(y, g_ref[...], b_ref[...])


def _layer_norm(x, g, b, resid=None, alpha=1.0, tm=512):
    t, d = x.shape
    row = pl.BlockSpec((tm, d), lambda i: (i, 0))
    vec = pl.BlockSpec((1, d), lambda i: (0, 0))
    if resid is None:
        kern = lambda x_ref, g_ref, b_ref, o_ref: _ln_kernel(alpha, x_ref, None, g_ref, b_ref, o_ref)
        args, specs = (x, g.reshape(1, d), b.reshape(1, d)), [row, vec, vec]
    else:
        kern = functools.partial(_ln_kernel, alpha)
        args, specs = (x, resid, g.reshape(1, d), b.reshape(1, d)), [row, row, vec, vec]
    return pl.pallas_call(
        kern,
        out_shape=jax.ShapeDtypeStruct((t, d), F32),
        grid=(t // tm,),
        in_specs=specs,
        out_specs=row,
        compiler_params=_cparams(("parallel",)),
        name="layer_norm",
    )(*args)


def _da_kernel(lam_init, tk, lp_ref, g_ref, q_ref, k_ref, v_ref, o_ref):
    tq = q_ref.shape[0]
    nk = k_ref.shape[0] // tk
    q = q_ref[...]
    qs = (q[:, :DA_HD], q[:, DA_HD:])

    def body(i, carry):
        off = pl.multiple_of(i * tk, tk)
        k = k_ref[pl.ds(off, tk), :]
        v = v_ref[pl.ds(off, tk), :]
        out = []
        for c in range(2):
            m_prev, l_prev, acc_prev = carry[c]
            s = lax.dot_general(qs[c], k[:, c * DA_HD:(c + 1) * DA_HD], (((1,), (1,)), ((), ())),
                                preferred_element_type=F32)
            m_new = jnp.maximum(m_prev, jnp.max(s, axis=-1, keepdims=True))
            a = jnp.exp(m_prev - m_new)
            p = jnp.exp(s - m_new)
            l_new = a * l_prev + jnp.sum(p, axis=-1, keepdims=True)
            acc_new = a * acc_prev + jnp.dot(p.astype(BF16), v, preferred_element_type=F32)
            out.append((m_new, l_new, acc_new))
        return tuple(out)

    init = tuple((jnp.full((tq, 1), -jnp.inf, F32), jnp.zeros((tq, 1), F32), jnp.zeros((tq, DA_VD), F32))
                 for _ in range(2))
    (_, l0, a0), (_, l1, a1) = lax.fori_loop(0, nk, body, init)
    lp = lp_ref[...]
    lam = (jnp.exp(jnp.sum(lp[0:1] * lp[1:2], axis=-1, keepdims=True))
           - jnp.exp(jnp.sum(lp[2:3] * lp[3:4], axis=-1, keepdims=True)) + lam_init)
    o = a0 / l0 - lam * (a1 / l1)
    o = o * lax.rsqrt(jnp.mean(o * o, axis=-1, keepdims=True) + 1e-6) * g_ref[...]
    o_ref[...] = o * (1.0 - lam_init)


def _diff_attention(qr, kr, vr, lam_params, norm_g, lam_init, tq=512, tk=8192):
    b, h, l, _ = qr.shape
    nq = l // tq
    tk = min(tk, l)
    return pl.pallas_call(
        functools.partial(_da_kernel, lam_init, tk),
        out_shape=jax.ShapeDtypeStruct((b * l, BRANCH_W), F32),
        grid=(b, h, nq),
        in_specs=[
            pl.BlockSpec((4, DA_HD), lambda bi, hi, qi: (0, 0)),
            pl.BlockSpec((1, DA_VD), lambda bi, hi, qi: (0, 0)),
            pl.BlockSpec((None, None, tq, DA_VD), lambda bi, hi, qi: (bi, hi, qi, 0)),
            pl.BlockSpec((None, None, l, DA_VD), lambda bi, hi, qi: (bi, hi, 0, 0)),
            pl.BlockSpec((None, None, l, DA_VD), lambda bi, hi, qi: (bi, hi, 0, 0)),
        ],
        out_specs=pl.BlockSpec((tq, DA_VD), lambda bi, hi, qi: (bi * nq + qi, hi)),
        compiler_params=_cparams(("parallel", "parallel", "arbitrary")),
        name="diff_attention",
    )(lam_params, norm_g.reshape(1, DA_VD), qr, kr, vr)


def _mem_attn_kernel(q_ref, kv_ref, o_ref):
    q = q_ref[...]
    kv = kv_ref[...]
    scale = MEM_HD ** -0.5
    for h in range(MEM_HEADS):
        qh = q[:, h * MEM_HD:(h + 1) * MEM_HD].astype(BF16)
        kh = kv[:, h * MEM_HD:(h + 1) * MEM_HD].astype(BF16)
        vh = kv[:, BRANCH_W + h * MEM_HD:BRANCH_W + (h + 1) * MEM_HD].astype(BF16)
        s = lax.dot_general(qh, kh, (((1,), (1,)), ((), ())), preferred_element_type=F32) * scale
        e = jnp.exp(s - jnp.max(s, axis=-1, keepdims=True))
        p = e / jnp.sum(e, axis=-1, keepdims=True)
        o_ref[:, h * MEM_HD:(h + 1) * MEM_HD] = jnp.dot(p.astype(BF16), vh, preferred_element_type=F32)


def _memory_attention(z, kv, b, l, tm=512):
    nm = l // tm
    m = kv.shape[1]
    return pl.pallas_call(
        _mem_attn_kernel,
        out_shape=jax.ShapeDtypeStruct((b * l, BRANCH_W), F32),
        grid=(b, nm),
        in_specs=[
            pl.BlockSpec((tm, ME_COLS), lambda bi, i: (bi * nm + i, ME_OFF // ME_COLS)),
            pl.BlockSpec((None, m, 2 * BRANCH_W), lambda bi, i: (bi, 0, 0)),
        ],
        out_specs=pl.BlockSpec((tm, BRANCH_W), lambda bi, i: (bi * nm + i, 0)),
        compiler_params=_cparams(("parallel", "arbitrary")),
        name="memory_attention",
    )(z, kv)


DFT_NA = 128
DFT_NB = 128
HY_LBLK = 4096
HY_KB = 8
HIGHEST = lax.Precision.HIGHEST


def _dft_tables(l):
    assert 2 * l == DFT_NA * DFT_NB
    n = DFT_NA * DFT_NB
    i = jnp.arange(DFT_NA, dtype=jnp.int32)
    ang1 = (2.0 * math.pi / DFT_NA) * ((i[:, None] * i[None, :]) % DFT_NA).astype(F32)
    c1, s1 = jnp.cos(ang1), jnp.sin(ang1)
    half = DFT_NA // 2
    kk = i[:, None, None] + DFT_NA * i[None, :, None]
    ang = (2.0 * math.pi / n) * ((kk * i[None, None, :]) % n).astype(F32)
    gr, gi = jnp.cos(ang), -jnp.sin(ang)
    t = dict(
        f=jnp.concatenate([c1[:, :half], -s1[:, :half]], axis=0),
        gr=gr, gi=gi,
        gg=jnp.concatenate([gr, gi], axis=1),
        hh=jnp.concatenate([gr.transpose(0, 2, 1), gi.transpose(0, 2, 1)], axis=1),
        er=c1.T[:half] / n, ei=s1.T[:half] / n,
    )
    for name in ('f', 'gg', 'hh', 'er', 'ei'):
        t[name + '_bf'] = t[name].astype(BF16)
    return t


def _dft1_kernel(prec, x_ref, f_ref, ar_ref, ai_ref):
    x = x_ref[...]
    f = f_ref[...]
    if prec is None:
        a = jnp.dot(f, x.astype(BF16), preferred_element_type=F32)
    else:
        a = jnp.dot(f, x, precision=prec, preferred_element_type=F32)
    ar_ref[...] = a[:DFT_NA].astype(ar_ref.dtype)
    ai_ref[...] = a[DFT_NA:].astype(ai_ref.dtype)


def _dft_stage1(x, part, f, out_dtype, prec):
    _, g, r, w = x.shape
    spec_o = pl.BlockSpec((None, DFT_NA, HY_LBLK), lambda gi, j: (gi, 0, j))
    return pl.pallas_call(
        functools.partial(_dft1_kernel, prec),
        out_shape=[jax.ShapeDtypeStruct((g, DFT_NA, w), out_dtype)] * 2,
        grid=(g, w // HY_LBLK),
        in_specs=[pl.BlockSpec((None, None, r, HY_LBLK), lambda gi, j: (part, gi, 0, j)),
                  pl.BlockSpec((2 * DFT_NA, r), lambda gi, j: (0, 0))],
        out_specs=[spec_o, spec_o],
        compiler_params=_cparams(("parallel", "parallel")),
        name="hyena_dft_stage1",
    )(x, f)


def _filter_mlp_kernel(l, feats_ref, w1_ref, b1_ref, w2_ref, b2_ref, w3_ref, b3_ref, fq_ref, dl_ref, h_ref, nrm_ref):
    i = pl.program_id(0)
    tm = feats_ref.shape[0]
    dot = functools.partial(jnp.dot, precision=HIGHEST, preferred_element_type=F32)
    h = jnp.sin(fq_ref[0:1] * (dot(feats_ref[...], w1_ref[...]) + b1_ref[...]))
    h = jnp.sin(fq_ref[1:2] * (dot(h, w2_ref[...]) + b2_ref[...]))
    h = dot(h, w3_ref[...]) + b3_ref[...]
    row = i * tm + lax.broadcasted_iota(jnp.int32, (tm, 1), 0)
    t = row.astype(F32) * (1.0 / (l - 1))
    decay = jnp.exp(-t * dl_ref[...])
    h = h * jnp.concatenate([decay] * (2 * HY_ORDER), axis=1)
    col = lax.broadcasted_iota(jnp.int32, (1, 2 * HY_ORDER * HY_W), 1)
    bwd = jnp.bitwise_and(lax.shift_right_logical(col, HY_W.bit_length() - 1), 1) == 1
    h = jnp.where(jnp.logical_and(row == 0, bwd), 0.0, h)
    h_ref[...] = h

    @pl.when(i == 0)
    def _():
        nrm_ref[...] = jnp.zeros_like(nrm_ref)

    nrm_ref[...] += jnp.sum(jnp.abs(h), axis=0, keepdims=True)


def _filter_mlp(feats, w1, b1, w2, b2, w3, b3, freq, deltas, tm=1024):
    l = feats.shape[0]
    wcols = 2 * HY_ORDER * HY_W
    full = lambda a: pl.BlockSpec(a.shape, lambda i: (0,) * a.ndim)
    args = (feats, w1, b1.reshape(1, -1), w2, b2.reshape(1, -1), w3, b3.reshape(1, -1), freq, deltas.reshape(1, -1))
    return pl.pallas_call(
        functools.partial(_filter_mlp_kernel, l),
        out_shape=[jax.ShapeDtypeStruct((l, wcols), F32), jax.ShapeDtypeStruct((1, wcols), F32)],
        grid=(l // tm,),
        in_specs=[pl.BlockSpec((tm, feats.shape[1]), lambda i: (i, 0))] + [full(a) for a in args[1:]],
        out_specs=[pl.BlockSpec((tm, wcols), lambda i: (i, 0)), pl.BlockSpec((1, wcols), lambda i: (0, 0))],
        compiler_params=_cparams(("arbitrary",)),
        name="hyena_filter_mlp",
    )(*args)


def _filter_spec_kernel(afr_ref, afi_ref, abr_ref, abi_ref, gr_ref, gi_ref, nf_ref, nb_ref, kr_ref, ki_ref):
    dot = functools.partial(jnp.dot, precision=HIGHEST, preferred_element_type=F32)
    inv = 1.0 / (nf_ref[...] + nb_ref[...])
    for j in range(gr_ref.shape[0]):
        gr, gi = gr_ref[j], gi_ref[j]
        sr = afr_ref[j] + abr_ref[j]
        si = afi_ref[j] + abi_ref[j]
        dr = afr_ref[j] - abr_ref[j]
        di = afi_ref[j] - abi_ref[j]
        kr_ref[j] = (dot(gr, sr) - dot(gi, si)) * inv
        ki_ref[j] = (dot(gr, di) + dot(gi, dr)) * inv


def _filter_spectrum(ar, ai, gr, gi, nrm, kb=4, cw=256):
    ncw = HY_W // cw

    def a_spec(d):
        return pl.BlockSpec((kb, DFT_NB, cw), lambda o, ki, ci: (ki, 0, (2 * o + d) * ncw + ci))

    def n_spec(d):
        return pl.BlockSpec((1, cw), lambda o, ki, ci: (0, (2 * o + d) * ncw + ci))

    g_spec = pl.BlockSpec((kb, DFT_NB, DFT_NB), lambda o, ki, ci: (ki, 0, 0))
    o_spec = pl.BlockSpec((None, kb, DFT_NB, cw), lambda o, ki, ci: (o, ki, 0, ci))
    return pl.pallas_call(
        _filter_spec_kernel,
        out_shape=[jax.ShapeDtypeStruct((HY_ORDER, DFT_NA, DFT_NB, HY_W), F32)] * 2,
        grid=(HY_ORDER, DFT_NA // kb, ncw),
        in_specs=[a_spec(0), a_spec(0), a_spec(1), a_spec(1), g_spec, g_spec, n_spec(0), n_spec(1)],
        out_specs=[o_spec, o_spec],
        compiler_params=_cparams(("parallel", "parallel", "parallel")),
        name="hyena_filter_spectrum",
    )(ar, ai, ar, ai, gr, gi, nrm, nrm)


def _hy_mid_kernel(ar_ref, ai_ref, gg_ref, hh_ref, kr_ref, ki_ref, br_ref, bi_ref):
    for j in range(HY_KB):
        gg = gg_ref[j]
        p = jnp.dot(gg, ar_ref[j], preferred_element_type=F32)
        q = jnp.dot(gg, ai_ref[j], preferred_element_type=F32)
        xr = p[:DFT_NB] - q[DFT_NB:]
        xi = q[:DFT_NB] + p[DFT_NB:]
        kr, ki = kr_ref[j], ki_ref[j]
        yr = (xr * kr - xi * ki).astype(BF16)
        yi = (xr * ki + xi * kr).astype(BF16)
        hh = hh_ref[j]
        p = jnp.dot(hh, yr, preferred_element_type=F32)
        q = jnp.dot(hh, yi, preferred_element_type=F32)
        br_ref[j] = (p[:DFT_NB] + q[DFT_NB:]).astype(BF16)
        bi_ref[j] = (q[:DFT_NB] - p[DFT_NB:]).astype(BF16)


def _hy_mid(ar, ai, gg, hh, kr, ki):
    b = ar.shape[0]
    a_spec = pl.BlockSpec((None, HY_KB, DFT_NB, HY_W), lambda ki_, bi: (bi, ki_, 0, 0))
    t_spec = pl.BlockSpec((HY_KB, 2 * DFT_NB, DFT_NB), lambda ki_, bi: (ki_, 0, 0))
    k_spec = pl.BlockSpec((HY_KB, DFT_NB, HY_W), lambda ki_, bi: (ki_, 0, 0))
    return pl.pallas_call(
        _hy_mid_kernel,
        out_shape=[jax.ShapeDtypeStruct(ar.shape, BF16)] * 2,
        grid=(DFT_NA // HY_KB, b),
        in_specs=[a_spec, a_spec, t_spec, t_spec, k_spec, k_spec],
        out_specs=[a_spec, a_spec],
        compiler_params=_cparams(("parallel", "arbitrary")),
        name="hyena_dft_mid",
    )(ar, ai, gg, hh, kr, ki)


def _hy_out_kernel(br_ref, bi_ref, er_ref, ei_ref, u_ref, gate_ref, skip_ref, o_ref):
    y = (jnp.dot(er_ref[...], br_ref[...], preferred_element_type=F32)
         - jnp.dot(ei_ref[...], bi_ref[...], preferred_element_type=F32))
    o_ref[...] = gate_ref[...] * (y + u_ref[...] * skip_ref[...])


def _hy_out(br, bi, er, ei, u, upart, gate, gpart, skip_row):
    _, b, r, w = u.shape
    row = pl.BlockSpec((None, r, HY_LBLK), lambda bi_, j: (bi_, 0, j))
    prow = lambda part: pl.BlockSpec((None, None, r, HY_LBLK), lambda bi_, j: (part, bi_, 0, j))
    bspec = pl.BlockSpec((None, DFT_NA, HY_LBLK), lambda bi_, j: (bi_, 0, j))
    espec = pl.BlockSpec((r, DFT_NA), lambda bi_, j: (0, 0))
    return pl.pallas_call(
        _hy_out_kernel,
        out_shape=jax.ShapeDtypeStruct((b, r, w), F32),
        grid=(b, w // HY_LBLK),
        in_specs=[bspec, bspec, espec, espec, prow(upart), prow(gpart),
                  pl.BlockSpec((1, HY_LBLK), lambda bi_, j: (0, 0))],
        out_specs=row,
        compiler_params=_cparams(("parallel", "parallel")),
        name="hyena_dft_out",
    )(br, bi, er, ei, u, gate, skip_row)


def _short_conv_kernel(z_ref, w_ref, b_ref, o_ref):
    z = z_ref[...]
    l = z.shape[0]
    row = lax.broadcasted_iota(jnp.int32, (l, 1), 0)
    prev = jnp.where(row == 0, 0.0, pltpu.roll(z, 1, 0))
    nxt = jnp.where(row == l - 1, 0.0, pltpu.roll(z, l - 1, 0))
    w = w_ref[...]
    o_ref[...] = prev * w[0:1] + z * w[1:2] + nxt * w[2:3] + b_ref[...]


def _short_conv(z, w, bias, b, l):
    nc = HY_W // LANE
    return pl.pallas_call(
        _short_conv_kernel,
        out_shape=jax.ShapeDtypeStruct((3, b * l, HY_W), F32),
        grid=(b, 3, nc),
        in_specs=[pl.BlockSpec((l, LANE), lambda bi, p, ci: (bi, p * nc + ci)),
                  pl.BlockSpec((3, LANE), lambda bi, p, ci: (0, p * nc + ci)),
                  pl.BlockSpec((1, LANE), lambda bi, p, ci: (0, p * nc + ci))],
        out_specs=pl.BlockSpec((None, l, LANE), lambda bi, p, ci: (p, bi, ci)),
        compiler_params=_cparams(("parallel", "parallel", "parallel")),
        name="hyena_short_conv",
    )(z, w, bias.reshape(1, -1))


def _hyena_filter_spectrum(l, w1, b1, w2, b2, w3, b3, freq, tabs):
    t = jnp.linspace(0.0, 1.0, l, dtype=F32)[:, None]
    w = 2.0 * math.pi * jnp.arange(l, dtype=F32)[:, None] / l
    fr = jnp.linspace(1e-4, HY_BANDS - 1, HY_BANDS, dtype=F32)[None, :]
    feats = jnp.concatenate([t, jnp.cos(fr * w), -jnp.sin(fr * w)], axis=-1)
    deltas = jnp.abs(jnp.linspace(HY_MIN_DECAY, HY_MAX_DECAY, HY_W, dtype=F32))
    h, nrm = _filter_mlp(feats, w1, b1, w2, b2, w3, b3, freq, deltas)
    wcols = h.shape[1]
    ar, ai = _dft_stage1(h.reshape(1, 1, l // DFT_NB, DFT_NB * wcols), 0, tabs['f'], F32, HIGHEST)
    return _filter_spectrum(ar.reshape(DFT_NA, DFT_NB, wcols), ai.reshape(DFT_NA, DFT_NB, wcols),
                            tabs['gr'], tabs['gi'], nrm)


def _hyena(z, short_w, short_b, skip, kr, ki, tabs, b, l):
    rows = l // DFT_NB
    u = _short_conv(z, short_w, short_b, b, l).reshape(3, b, rows, DFT_NB * HY_W)
    v, vpart = u, 0
    for o in range(HY_ORDER):
        ar, ai = _dft_stage1(v, vpart, tabs['f_bf'], BF16, None)
        shp4 = (b, DFT_NA, DFT_NB, HY_W)
        br, bi = _hy_mid(ar.reshape(shp4), ai.reshape(shp4), tabs['gg_bf'], tabs['hh_bf'], kr[o], ki[o])
        shp3 = (b, DFT_NA, DFT_NB * HY_W)
        skip_row = jnp.tile(skip[o], HY_LBLK // HY_W).reshape(1, HY_LBLK)
        out = _hy_out(br.reshape(shp3), bi.reshape(shp3), tabs['er_bf'], tabs['ei_bf'], v, vpart, u, o + 1, skip_row)
        v, vpart = out[None], 0
    return out.reshape(b * l, HY_W)


HG_REC_CHUNK = 64
HG_SUB = 16
HG_TILE = 512
HG_HEADS_PER_STEP = 2
HG_UNROLL = 1


def _gate_terms(z, lb):
    a = jnp.log(lb)
    b = jnp.log1p(-lb) + (jnp.minimum(z, 0.0) - jnp.log1p(jnp.exp(-jnp.abs(z))))
    log_f = jnp.maximum(a, b) + jnp.log1p(jnp.exp(-jnp.abs(a - b)))
    return log_f, (1.0 - lb) * jax.nn.sigmoid(-z)


SUBLANES = 8


def _running_sum_rows(x, fwd):
    r = x.shape[0]
    row = jnp.bitwise_and(lax.broadcasted_iota(jnp.int32, (r, 1), 0), SUBLANES - 1)
    s = 1
    while s < SUBLANES:
        if fwd:
            x = x + jnp.where(row >= s, pltpu.roll(x, s, 0), 0.0)
        else:
            x = x + jnp.where(row < SUBLANES - s, pltpu.roll(x, r - s, 0), 0.0)
        s *= 2
    groups = [x[i:i + SUBLANES] for i in range(0, r, SUBLANES)]
    order = range(1, len(groups)) if fwd else range(len(groups) - 2, -1, -1)
    for i in order:
        prev = groups[i - 1][SUBLANES - 1:SUBLANES] if fwd else groups[i + 1][0:1]
        groups[i] = groups[i] + prev
    return jnp.concatenate(groups, axis=0)


def _gla_chunk(q, k, v, g, st, fwd):
    ch = q.shape[0]
    b = _running_sum_rows(g, fwd)
    nsb = ch // HG_SUB
    parts = [None] * nsb

    def add(i, val):
        parts[i] = val if parts[i] is None else parts[i] + val

    vb = v.astype(BF16)
    half = HG_SUB // 2
    b2 = b * math.log2(math.e)

    def halves(x, hi):
        return jnp.concatenate([x[i * HG_SUB + hi * half:i * HG_SUB + (hi + 1) * half] for i in range(nsb)], axis=0)

    qh, bh = [halves(q, 0), halves(q, 1)], [halves(b2, 0), halves(b2, 1)]
    nr = nsb * half
    li = lax.broadcasted_iota(jnp.int32, (nr, HG_SUB), 1)
    ti = jnp.bitwise_and(lax.broadcasted_iota(jnp.int32, (nr, HG_SUB), 0), half - 1)
    acc = [jnp.zeros((nr, HG_SUB), F32), jnp.zeros((nr, HG_SUB), F32)]
    for s in range(HG_SUB):
        src = [i * HG_SUB + s for i in range(nsb)]
        bs = jnp.concatenate([jnp.broadcast_to(b2[r:r + 1], (half, LANE)) for r in src], axis=0)
        ks = jnp.concatenate([jnp.broadcast_to(k[r:r + 1], (half, LANE)) for r in src], axis=0)
        for hi in range(2):
            lo_t, hi_t = hi * half, (hi + 1) * half - 1
            if (fwd and hi_t < s) or (not fwd and lo_t > s):
                continue
            e = jnp.exp2(bh[hi] - bs)
            acc[hi] = jnp.where(li == s, jnp.sum(qh[hi] * e * ks, axis=-1, keepdims=True), acc[hi])
    for hi in range(2):
        t_in = ti + hi * half
        keep = (li <= t_in) if fwd else (li >= t_in)
        acc[hi] = jnp.where(keep, acc[hi], 0.0)
    for i in range(nsb):
        sl = slice(i * HG_SUB, (i + 1) * HG_SUB)
        hs = slice(i * half, (i + 1) * half)
        a = jnp.concatenate([acc[0][hs], acc[1][hs]], axis=0).astype(BF16)
        add(i, jnp.dot(a, vb[sl], preferred_element_type=F32))
    h = HG_SUB
    while h < ch:
        for j in range(ch // (2 * h)):
            lo = slice(2 * h * j, 2 * h * j + h)
            hi = slice(2 * h * j + h, 2 * h * j + 2 * h)
            if fwd:
                rows, cols, ref = hi, lo, b[2 * h * j + h - 1:2 * h * j + h]
            else:
                rows, cols, ref = lo, hi, b[2 * h * j + h:2 * h * j + h + 1]
            qt = (q[rows] * jnp.exp(b[rows] - ref)).astype(BF16)
            kt = (k[cols] * jnp.exp(ref - b[cols])).astype(BF16)
            a = lax.dot_general(qt, kt, (((1,), (1,)), ((), ())), preferred_element_type=F32)
            ov = jnp.dot(a.astype(BF16), vb[cols], preferred_element_type=F32)
            for ii in range(h // HG_SUB):
                add(rows.start // HG_SUB + ii, ov[ii * HG_SUB:(ii + 1) * HG_SUB])
        h *= 2
    o = jnp.concatenate(parts, axis=0)
    qh = (q * jnp.exp(b)).astype(BF16)
    o = o + lax.dot_general(qh, st.astype(BF16), (((1,), (1,)), ((), ())), preferred_element_type=F32)
    btot = b[ch - 1:ch] if fwd else b[0:1]
    kh = (k * jnp.exp(btot - b)).astype(BF16)
    st_new = st * jnp.exp(btot) + lax.dot_general(vb, kh, (((0,), (0,)), ((), ())), preferred_element_type=F32)
    return o, st_new


def _hgrn2_kernel(lbf_ref, lbb_ref, ng_ref, zqf_ref, zif_ref, zff_ref, zgf_ref, zqb_ref, zib_ref, zbb_ref, zgb_ref,
                  o_ref, st_ref):
    c = pl.program_id(2)
    n = pl.num_programs(2)
    nch = HG_TILE // HG_REC_CHUNK

    @pl.when(c == 0)
    def _():
        st_ref[...] = jnp.zeros_like(st_ref)

    def body(second_pass, i, carry):
        off_f = pl.multiple_of(i * HG_REC_CHUNK, HG_REC_CHUNK)
        off_b = pl.multiple_of((nch - 1 - i) * HG_REC_CHUNK, HG_REC_CHUNK)
        for fwd, off, zq_ref, zi_ref, zd_ref, zg_ref, lb_ref, tile in (
                (True, off_f, zqf_ref, zif_ref, zff_ref, zgf_ref, lbf_ref, c),
                (False, off_b, zqb_ref, zib_ref, zbb_ref, zgb_ref, lbb_ref, n - 1 - c)):
            sl = pl.ds(off, HG_REC_CHUNK)
            rows = pl.ds(pl.multiple_of(tile * HG_TILE + off, HG_REC_CHUNK), HG_REC_CHUNK)
            d = 0 if fwd else 1
            for hh in range(HG_HEADS_PER_STEP):
                ln = slice(hh * LANE, (hh + 1) * LANE)
                g, k = _gate_terms(zd_ref[sl, ln], lb_ref[:, ln])
                o, st_new = _gla_chunk(jax.nn.silu(zq_ref[sl, ln]), k, zi_ref[sl, ln], g, st_ref[d, hh], fwd)
                st_ref[d, hh] = st_new
                if second_pass:
                    tot = o_ref[rows, ln] + o
                    tot = tot * lax.rsqrt(jnp.mean(tot * tot, axis=-1, keepdims=True) + 1e-6) * ng_ref[...]
                    o_ref[rows, ln] = tot * jax.nn.silu(zg_ref[sl, ln])
                else:
                    o_ref[rows, ln] = o
        return carry

    @pl.when(c < n // 2)
    def _():
        lax.fori_loop(0, nch, functools.partial(body, False), 0, unroll=HG_UNROLL)

    @pl.when(c >= n // 2)
    def _():
        lax.fori_loop(0, nch, functools.partial(body, True), 0, unroll=HG_UNROLL)


def _hgrn2(z, lb_fwd, lb_bwd, norm_g, b, l):
    n = l // HG_TILE
    assert n % 2 == 0
    w = HG_HEADS_PER_STEP * LANE
    base = HY_COLS // w
    nh = BRANCH_W // w

    def zspec(part, rev):
        if rev:
            return pl.BlockSpec((HG_TILE, w), lambda bi, hi, ci: (bi * n + n - 1 - ci, base + part * nh + hi))
        return pl.BlockSpec((HG_TILE, w), lambda bi, hi, ci: (bi * n + ci, base + part * nh + hi))

    lbspec = pl.BlockSpec((1, w), lambda bi, hi, ci: (0, hi))
    return pl.pallas_call(
        _hgrn2_kernel,
        out_shape=jax.ShapeDtypeStruct((b * l, BRANCH_W), F32),
        grid=(b, nh, n),
        in_specs=[lbspec, lbspec, pl.BlockSpec((1, LANE), lambda bi, hi, ci: (0, 0)),
                  zspec(0, False), zspec(1, False), zspec(2, False), zspec(4, False),
                  zspec(0, True), zspec(1, True), zspec(3, True), zspec(4, True)],
        out_specs=pl.BlockSpec((l, w), lambda bi, hi, ci: (bi, hi)),
        scratch_shapes=[pltpu.VMEM((2, HG_HEADS_PER_STEP, HG_DV, HG_DK), F32)],
        compiler_params=_cparams(("parallel", "parallel", "arbitrary")),
        name="hgrn2_scan",
    )(lb_fwd.reshape(1, BRANCH_W), lb_bwd.reshape(1, BRANCH_W), norm_g.reshape(1, HG_DV),
      z, z, z, z, z, z, z, z)


def _merge_kernel(x_ref, o0_ref, o1_ref, o2_ref, o3_ref, wg_ref, bg_ref, wbr_ref, wout_ref, g_ref, b_ref,
                  out_ref, acc_ref, xb_ref, ob_ref):
    j = pl.program_id(1)

    @pl.when(j == 0)
    def _():
        xb_ref[...] = x_ref[...].astype(BF16)
        for bi, o_ref in enumerate((o0_ref, o1_ref, o2_ref, o3_ref)):
            ob_ref[bi] = o_ref[...].astype(BF16)
        acc_ref[...] = jnp.zeros_like(acc_ref)

    xb = xb_ref[...]
    merged = None
    for bi in range(N_BRANCH):
        gate = jax.nn.sigmoid(jnp.dot(xb, wg_ref[bi], preferred_element_type=F32) + bg_ref[bi])
        term = gate * jnp.dot(ob_ref[bi], wbr_ref[bi], preferred_element_type=F32)
        merged = term if merged is None else merged + term
    acc_ref[...] += jnp.dot(merged.astype(BF16), wout_ref[...], preferred_element_type=F32)

    @pl.when(j == pl.num_programs(1) - 1)
    def _():
        out_ref[...] = _ln_rows(DN_ALPHA * x_ref[...] + acc_ref[...], g_ref[...], b_ref[...])


def _merge(x, branches, w_gate, b_gate, w_br, w_out, ln_g, ln_b, tm=512, tn=256):
    t, d = x.shape
    row = pl.BlockSpec((tm, d), lambda i, j: (i, 0))
    brow = pl.BlockSpec((tm, BRANCH_W), lambda i, j: (i, 0))
    vec = pl.BlockSpec((1, d), lambda i, j: (0, 0))
    return pl.pallas_call(
        _merge_kernel,
        out_shape=jax.ShapeDtypeStruct((t, d), F32),
        grid=(t // tm, d // tn),
        in_specs=[
            row, brow, brow, brow, brow,
            pl.BlockSpec((N_BRANCH, d, tn), lambda i, j: (0, 0, j)),
            pl.BlockSpec((N_BRANCH, 1, tn), lambda i, j: (0, 0, j)),
            pl.BlockSpec((N_BRANCH, BRANCH_W, tn), lambda i, j: (0, 0, j)),
            pl.BlockSpec((tn, d), lambda i, j: (j, 0)),
            vec, vec,
        ],
        out_specs=row,
        scratch_shapes=[pltpu.VMEM((tm, d), F32), pltpu.VMEM((tm, d), BF16),
                        pltpu.VMEM((N_BRANCH, tm, BRANCH_W), BF16)],
        compiler_params=_cparams(("parallel", "arbitrary")),
        name="gated_merge",
    )(x, *branches, w_gate, b_gate.reshape(N_BRANCH, 1, d), w_br, w_out, ln_g.reshape(1, d), ln_b.reshape(1, d))


def _expert_kernel(x_ref, gate_ref, w1_ref, w3_ref, w2_ref, o_ref):
    f = pl.program_id(2)
    x = x_ref[...]
    h1 = jnp.dot(x, w1_ref[...], preferred_element_type=F32)
    h3 = jnp.dot(x, w3_ref[...], preferred_element_type=F32)
    h = (jax.nn.silu(h1) * h3).astype(BF16)
    part = jnp.dot(h, w2_ref[...], preferred_element_type=F32)

    @pl.when(f == 0)
    def _():
        o_ref[...] = part

    @pl.when(f != 0)
    def _():
        o_ref[...] += part

    @pl.when(f == pl.num_programs(2) - 1)
    def _():
        o_ref[...] = o_ref[...] * gate_ref[...]


def _expert_ffn(xe, gate, w1, w3, w2, tm=512, tf=1024):
    e, c, d = xe.shape
    ff = w1.shape[2]
    tm = min(tm, c)
    return pl.pallas_call(
        _expert_kernel,
        out_shape=jax.ShapeDtypeStruct((e, c, d), F32),
        grid=(e, c // tm, ff // tf),
        in_specs=[
            pl.BlockSpec((None, tm, d), lambda ei, i, f: (ei, i, 0)),
            pl.BlockSpec((None, tm, 1), lambda ei, i, f: (ei, i, 0)),
            pl.BlockSpec((None, d, tf), lambda ei, i, f: (ei, 0, f)),
            pl.BlockSpec((None, d, tf), lambda ei, i, f: (ei, 0, f)),
            pl.BlockSpec((None, tf, d), lambda ei, i, f: (ei, f, 0)),
        ],
        out_specs=pl.BlockSpec((None, tm, d), lambda ei, i, f: (ei, i, 0)),
        compiler_params=_cparams(("parallel", "parallel", "arbitrary")),
        name="expert_ffn",
    )(xe, gate, w1, w3, w2)


RT_TM = 1024
RT_WIN = 32
RT_EAGER = 2
RT_CHUNKS = 4


def _router_kernel(x_ref, w_ref, o_ref):
    logits = lax.dot_general(w_ref[...], x_ref[...].astype(BF16), (((1,), (1,)), ((), ())),
                             preferred_element_type=F32)
    e = jnp.exp(logits - jnp.max(logits, axis=0, keepdims=True))
    aff = e / jnp.sum(e, axis=0, keepdims=True)
    for c in range(x_ref.shape[0] // LANE):
        o_ref[:, c, :] = aff[:, c * LANE:(c + 1) * LANE]


def _router(x, w_t):
    t, d = x.shape
    return pl.pallas_call(
        _router_kernel,
        out_shape=jax.ShapeDtypeStruct((N_EXPERTS, t // LANE, LANE), F32),
        grid=(t // RT_TM,),
        in_specs=[pl.BlockSpec((RT_TM, d), lambda i: (i, 0)), pl.BlockSpec((N_EXPERTS, d), lambda i: (0, 0))],
        out_specs=pl.BlockSpec((N_EXPERTS, RT_TM // LANE, LANE), lambda i: (0, i, 0)),
        compiler_params=_cparams(("parallel",)),
        name="moe_router",
    )(x, w_t)


def _prefix_count(mask):
    nch = mask.shape[0]
    m = jnp.where(mask, 1.0, 0.0)
    li = lax.broadcasted_iota(jnp.int32, (LANE, LANE), 0)
    lj = lax.broadcasted_iota(jnp.int32, (LANE, LANE), 1)
    incl = jnp.dot(m.astype(BF16), jnp.where(li <= lj, 1.0, 0.0).astype(BF16), preferred_element_type=F32)
    tot = jnp.broadcast_to(incl[:, LANE - 1:LANE], (nch, LANE))
    ri = lax.broadcasted_iota(jnp.int32, (nch, nch), 0)
    ci = lax.broadcasted_iota(jnp.int32, (nch, nch), 1)
    start = jnp.dot(jnp.where(ci < ri, 1.0, 0.0).astype(BF16), tot.astype(BF16), preferred_element_type=F32)
    return incl - m + start, start


def _route_select_kernel(cap, a_ref, pos_ref, start_ref):
    bits = pltpu.bitcast(a_ref[...], jnp.int32)

    def step(i, tau):
        cand = jnp.bitwise_or(tau, lax.shift_left(jnp.int32(1), 30 - i))
        cnt = jnp.sum(jnp.where(bits >= cand, 1.0, 0.0))
        return jnp.where(cnt >= cap, cand, tau)

    tau = lax.fori_loop(0, 31, step, jnp.int32(0))
    gt = bits > tau
    eq = bits == tau
    need = cap - jnp.sum(jnp.where(gt, 1.0, 0.0))
    eq_rank, _ = _prefix_count(eq)
    sel = jnp.logical_or(gt, jnp.logical_and(eq, eq_rank < need))
    pos, start = _prefix_count(sel)
    pos_ref[...] = jnp.where(sel, pos, -1.0)
    start_ref[...] = start


def _route_select(aff3, cap):
    e, nch, _ = aff3.shape
    spec = pl.BlockSpec((None, nch, LANE), lambda ei: (ei, 0, 0))
    return pl.pallas_call(
        functools.partial(_route_select_kernel, float(cap)),
        out_shape=[jax.ShapeDtypeStruct(aff3.shape, F32)] * 2,
        grid=(e,),
        in_specs=[spec],
        out_specs=[spec, spec],
        compiler_params=_cparams(("parallel",)),
        name="moe_route_select",
    )(aff3)


def _route_compact_kernel(cap, start_sm, pos_ref, a_ref, idx_ref, gate_ref, il_ref, gl_ref):
    ei = pl.program_id(0)
    nch = pos_ref.shape[0]
    il_ref[...] = jnp.zeros_like(il_ref)
    gl_ref[...] = jnp.zeros_like(gl_ref)
    s_iota = lax.broadcasted_iota(jnp.int32, (RT_WIN, LANE), 0).astype(F32)
    lane = lax.broadcasted_iota(jnp.int32, (1, LANE), 1).astype(F32)
    log_win = RT_WIN.bit_length() - 1

    def body(i, carry):
        rest = []
        for u in range(RT_CHUNKS):
            j = i * RT_CHUNKS + u
            first = start_sm[ei, j]
            nxt = jnp.where(j + 1 < nch, start_sm[ei, jnp.minimum(j + 1, nch - 1)], cap)
            base = lax.shift_left(lax.shift_right_logical(first, log_win), log_win)
            nwin = lax.shift_right_logical(nxt - base + (RT_WIN - 1), log_win)
            prow = pos_ref[pl.ds(j, 1), :]
            arow = a_ref[pl.ds(j, 1), :]
            tok = lane + jnp.asarray(j * LANE, F32)

            def window(w, c2, base=base, prow=prow, arow=arow, tok=tok):
                lo = pl.multiple_of(base + w * RT_WIN, RT_WIN)
                hit = (prow - jnp.asarray(lo, F32)) == s_iota
                il_ref[pl.ds(lo, RT_WIN), :] += jnp.sum(jnp.where(hit, tok, 0.0), axis=1, keepdims=True)
                gl_ref[pl.ds(lo, RT_WIN), :] += jnp.sum(jnp.where(hit, arow, 0.0), axis=1, keepdims=True)
                return c2

            for w in range(RT_EAGER):
                window(w, 0)
            rest.append((nwin, window))
        for nwin, window in rest:
            lax.fori_loop(RT_EAGER, nwin, window, 0)
        return carry

    lax.fori_loop(0, nch // RT_CHUNKS, body, 0)
    diag = (lax.broadcasted_iota(jnp.int32, (LANE, LANE), 0) == lax.broadcasted_iota(jnp.int32, (LANE, LANE), 1))
    for r in range(idx_ref.shape[0]):
        rows = slice(r * LANE, (r + 1) * LANE)
        idx_ref[r:r + 1, :] = jnp.sum(jnp.where(diag, il_ref[rows, :], 0.0), axis=0, keepdims=True).astype(jnp.int32)
        gate_ref[r:r + 1, :] = jnp.sum(jnp.where(diag, gl_ref[rows, :], 0.0), axis=0, keepdims=True)


def _route_compact(start, pos3, aff3, cap):
    e, nch, _ = pos3.shape
    spec = pl.BlockSpec((None, nch, LANE), lambda ei, s: (ei, 0, 0))
    ospec = pl.BlockSpec((None, cap // LANE, LANE), lambda ei, s: (ei, 0, 0))
    return pl.pallas_call(
        functools.partial(_route_compact_kernel, cap),
        out_shape=[jax.ShapeDtypeStruct((e, cap // LANE, LANE), jnp.int32),
                   jax.ShapeDtypeStruct((e, cap // LANE, LANE), F32)],
        grid_spec=pltpu.PrefetchScalarGridSpec(
            num_scalar_prefetch=1, grid=(e,), in_specs=[spec, spec], out_specs=[ospec, ospec],
            scratch_shapes=[pltpu.VMEM((cap + 2 * LANE, LANE), F32), pltpu.VMEM((cap + 2 * LANE, LANE), F32)]),
        compiler_params=_cparams(("parallel",)),
        name="moe_route_compact",
    )(start, pos3, aff3)


def _expert_choice(x, w_t, cap):
    aff3 = _router(x, w_t)
    pos3, start3 = _route_select(aff3, cap)
    idx3, gate3 = _route_compact(start3[:, :, 0].astype(jnp.int32), pos3, aff3, cap)
    return idx3.reshape(N_EXPERTS, cap), gate3.reshape(N_EXPERTS, cap)


CB_TS = 256
CB_SLOTS = 3


def _combine_kernel(nper, idx_ref, idxn_ref, ye_ref, y_in, y_ref, buf, sem_r, sem_w):
    del y_in
    g = pl.program_id(0)
    ng = pl.num_programs(0)
    i = lax.rem(g, nper)
    slot = lax.rem(g, CB_SLOTS)
    nslot = lax.rem(g + 1, CB_SLOTS)
    pslot = lax.rem(g + CB_SLOTS - 1, CB_SLOTS)
    first = i == 0

    def row_read(row, s, j):
        return pltpu.make_async_copy(y_ref.at[pl.ds(row, 1)], buf.at[s, pl.ds(j, 1)], sem_r.at[s])

    def row_write(row, s, j):
        return pltpu.make_async_copy(buf.at[s, pl.ds(j, 1)], y_ref.at[pl.ds(row, 1)], sem_w.at[s])

    def for_rows(fn):
        lax.fori_loop(0, CB_TS, lambda j, c: (fn(j), c)[1], 0, unroll=8)

    def start_reads(ids_ref, s):
        for_rows(lambda j: row_read(ids_ref[0, j], s, j).start())

    def wait_reads(s):
        for_rows(lambda j: row_read(0, s, 0).wait())

    def wait_writes(s):
        for_rows(lambda j: row_write(0, s, 0).wait())

    @pl.when(jnp.logical_and(g >= 2, i != 1))
    def _():
        wait_writes(nslot)

    @pl.when(jnp.logical_and(first, g >= 1))
    def _():
        wait_writes(pslot)

    @pl.when(first)
    def _():
        start_reads(idx_ref, slot)

    @pl.when(i != nper - 1)
    def _():
        start_reads(idxn_ref, nslot)

    wait_reads(slot)
    buf[slot] = buf[slot] + ye_ref[...]
    for_rows(lambda j: row_write(idx_ref[0, j], slot, j).start())

    @pl.when(g == ng - 1)
    def _():
        wait_writes(pslot)
        wait_writes(slot)


def _moe_combine(idx, ye, t):
    e, cap, d = ye.shape
    nper = cap // CB_TS
    assert nper >= 2
    ng = e * nper
    idx3 = idx.reshape(ng, 1, CB_TS)
    sm_spec = lambda off: pl.BlockSpec((None, 1, CB_TS), lambda g: (jnp.minimum(g + off, ng - 1), 0, 0),
                                       memory_space=pltpu.SMEM)
    return pl.pallas_call(
        functools.partial(_combine_kernel, nper),
        out_shape=jax.ShapeDtypeStruct((t, d), F32),
        grid=(ng,),
        in_specs=[sm_spec(0), sm_spec(1), pl.BlockSpec((CB_TS, d), lambda g: (g, 0)),
                  pl.BlockSpec(memory_space=pl.ANY)],
        out_specs=pl.BlockSpec(memory_space=pl.ANY),
        scratch_shapes=[pltpu.VMEM((CB_SLOTS, CB_TS, d), F32), pltpu.SemaphoreType.DMA((CB_SLOTS,)),
                        pltpu.SemaphoreType.DMA((CB_SLOTS,))],
        input_output_aliases={3: 0},
        compiler_params=_cparams(("arbitrary",)),
        name="moe_combine",
    )(idx3, idx3, ye.reshape(e * cap, d), jnp.zeros((t, d), F32))


def _gather_kernel(idx_ref, idxn_ref, x_ref, o_ref, buf, sem):
    g = pl.program_id(0)
    ng = pl.num_programs(0)
    slot = lax.rem(g, 2)

    def row_read(row, s, j):
        return pltpu.make_async_copy(x_ref.at[pl.ds(row, 1)], buf.at[s, pl.ds(j, 1)], sem.at[s])

    def start_reads(ids_ref, s):
        lax.fori_loop(0, CB_TS, lambda j, c: (row_read(ids_ref[0, j], s, j).start(), c)[1], 0, unroll=8)

    @pl.when(g == 0)
    def _():
        start_reads(idx_ref, slot)

    @pl.when(g + 1 < ng)
    def _():
        start_reads(idxn_ref, 1 - slot)

    lax.fori_loop(0, CB_TS, lambda j, c: (row_read(0, slot, 0).wait(), c)[1], 0, unroll=8)
    o_ref[...] = buf[slot].astype(BF16)


def _moe_gather(x, idx):
    e, cap = idx.shape
    d = x.shape[1]
    ng = e * cap // CB_TS
    idx3 = idx.reshape(ng, 1, CB_TS)
    sm_spec = lambda off: pl.BlockSpec((None, 1, CB_TS), lambda g: (jnp.minimum(g + off, ng - 1), 0, 0),
                                       memory_space=pltpu.SMEM)
    out = pl.pallas_call(
        _gather_kernel,
        out_shape=jax.ShapeDtypeStruct((e * cap, d), BF16),
        grid=(ng,),
        in_specs=[sm_spec(0), sm_spec(1), pl.BlockSpec(memory_space=pl.ANY)],
        out_specs=pl.BlockSpec((CB_TS, d), lambda g: (g, 0)),
        scratch_shapes=[pltpu.VMEM((2, CB_TS, d), F32), pltpu.SemaphoreType.DMA((2,))],
        compiler_params=_cparams(("arbitrary",)),
        name="moe_gather",
    )(idx3, idx3, x)
    return out.reshape(e, cap, d)


def _attn_prep_kernel(zq_ref, zk_ref, zv_ref, cos_ref, sin_ref, q_ref, k_ref, v_ref):
    lane = lax.broadcasted_iota(jnp.int32, (1, LANE), 1)
    first = jnp.bitwise_and(lane, DA_HD - 1) < ROPE_DIM // 2
    cos, sin = cos_ref[...], sin_ref[...]

    def rope(x):
        partner = jnp.where(first, pltpu.roll(x, LANE - ROPE_DIM // 2, 1), pltpu.roll(x, ROPE_DIM // 2, 1))
        return x * cos + partner * sin

    q_ref[...] = (rope(zq_ref[...]) * (DA_HD ** -0.5)).astype(BF16)
    k_ref[...] = rope(zk_ref[...]).astype(BF16)
    v_ref[...] = zv_ref[...].astype(BF16)


def _rope_lane_tables(l):
    inv = ROPE_THETA ** (-jnp.arange(0, ROPE_DIM, 2, dtype=F32) / ROPE_DIM)
    ang = jnp.arange(l, dtype=F32)[:, None] * inv[None, :]
    c, s = jnp.cos(ang), jnp.sin(ang)
    rest = DA_HD - ROPE_DIM
    cos_h = jnp.concatenate([c, c, jnp.ones((l, rest), F32)], axis=1)
    sin_h = jnp.concatenate([-s, s, jnp.zeros((l, rest), F32)], axis=1)
    return jnp.concatenate([cos_h, cos_h], axis=1), jnp.concatenate([sin_h, sin_h], axis=1)


def _attn_prep(z, cos, sin, b, l, tm=1024):
    n = l // tm
    base = DA_OFF // LANE

    def zspec(part):
        return pl.BlockSpec((tm, LANE), lambda bi, hi, i: (bi * n + i, base + part * DA_HEADS + hi))

    tspec = pl.BlockSpec((tm, LANE), lambda bi, hi, i: (i, 0))
    ospec = pl.BlockSpec((None, None, tm, LANE), lambda bi, hi, i: (bi, hi, i, 0))
    return pl.pallas_call(
        _attn_prep_kernel,
        out_shape=[jax.ShapeDtypeStruct((b, DA_HEADS, l, DA_VD), BF16)] * 3,
        grid=(b, DA_HEADS, n),
        in_specs=[zspec(0), zspec(1), zspec(2), tspec, tspec],
        out_specs=[ospec, ospec, ospec],
        compiler_params=_cparams(("parallel", "parallel", "parallel")),
        name="attention_prep",
    )(z, z, z, cos, sin)


def _trunk(x, mem, p):
    B, L, D = x.shape
    T = B * L
    cos, sin = _rope_lane_tables(L)
    lb_all = jnp.cumsum(jax.nn.softmax(p['hg_lb_raw'], axis=1), axis=1)
    lb_all = lb_all - lb_all[:, :1]
    xf = _layer_norm(x.reshape(T, D), p['ln_in_g'], p['ln_in_b'])
    memf = mem.reshape(B * mem.shape[1], D)
    cap = EC_FACTOR * T // N_EXPERTS
    for l in range(DEPTH):
        z = _matmul(xf, p['w_in'][l], 1024, 1024)
        hy_kr, hy_ki = p['hy_spectrum'][L][l]
        o_hy = _hyena(z, p['hy_short_w'][l], p['hy_short_b'][l], p['hy_skip'][l], hy_kr, hy_ki,
                      p['dft_tables'][L], B, L)
        o_hg = _hgrn2(z, lb_all[0, l], lb_all[1, l], p['hg_norm_g'][l], B, L)
        qr, kr, vr = _attn_prep(z, cos, sin, B, L)
        lam_init = 0.8 - 0.6 * math.exp(-0.3 * l)
        o_da = _diff_attention(qr, kr, vr, p['da_lambda'][l], p['da_norm_g'][l], lam_init)
        kv = _matmul(memf, p['w_mem_kv'][l], min(512, memf.shape[0]), 512).reshape(B, -1, 2 * BRANCH_W)
        o_me = _memory_attention(z, kv, B, L)
        xf = _merge(xf, (o_hy, o_hg, o_da, o_me), p['w_gate'][l], p['b_gate'][l], p['w_br'][l], p['w_out'][l],
                    p['ln1_g'][l], p['ln1_b'][l])
        idx, gate = _expert_choice(xf, p['w_router_t'][l], cap)
        xe = _moe_gather(xf, idx)
        ye = _expert_ffn(xe, gate[..., None], p['w_e1'][l], p['w_e3'][l], p['w_e2'][l])
        y = _moe_combine(idx, ye, T)
        xf = _layer_norm(xf, p['ln2_g'][l], p['ln2_b'][l], resid=y, alpha=DN_ALPHA)
    return xf.reshape(B, L, D)


def kernel(x_prompt, x_sample, mem_prompt, mem_sample, ln_in_g, ln_in_b, w_in, hy_short_w, hy_short_b, hy_ffn_w1, hy_ffn_b1, hy_ffn_w2, hy_ffn_b2, hy_ffn_w3, hy_ffn_b3, hy_freq, hy_skip, hg_lb_raw, hg_norm_g, da_lambda, da_norm_g, w_mem_kv, w_gate, b_gate, w_br, w_out, ln1_g, ln1_b, w_router, w_e1, w_e3, w_e2, ln2_g, ln2_b):
    p = dict(ln_in_g=ln_in_g, ln_in_b=ln_in_b, hy_short_w=hy_short_w, hy_short_b=hy_short_b,
             hy_ffn_w1=hy_ffn_w1, hy_ffn_b1=hy_ffn_b1, hy_ffn_w2=hy_ffn_w2, hy_ffn_b2=hy_ffn_b2,
             hy_ffn_w3=hy_ffn_w3, hy_ffn_b3=hy_ffn_b3, hy_freq=hy_freq, hy_skip=hy_skip, hg_lb_raw=hg_lb_raw,
             hg_norm_g=hg_norm_g, da_lambda=da_lambda, da_norm_g=da_norm_g, b_gate=b_gate,
             ln1_g=ln1_g, ln1_b=ln1_b, ln2_g=ln2_g, ln2_b=ln2_b)
    for name, w in (('w_in', w_in), ('w_mem_kv', w_mem_kv), ('w_gate', w_gate), ('w_br', w_br), ('w_out', w_out),
                    ('w_e1', w_e1), ('w_e3', w_e3), ('w_e2', w_e2)):
        p[name] = w.astype(BF16)
    p['w_router_t'] = jnp.swapaxes(w_router, 1, 2).astype(BF16)
    p['dft_tables'], p['hy_spectrum'] = {}, {}
    for seq_len in sorted({x_prompt.shape[1], x_sample.shape[1]}):
        tabs = _dft_tables(seq_len)
        p['dft_tables'][seq_len] = tabs
        p['hy_spectrum'][seq_len] = [
            _hyena_filter_spectrum(seq_len, hy_ffn_w1[l], hy_ffn_b1[l], hy_ffn_w2[l], hy_ffn_b2[l], hy_ffn_w3[l],
                                   hy_ffn_b3[l], hy_freq[l], tabs) for l in range(DEPTH)]
    y_prompt = _trunk(x_prompt, mem_prompt, p)
    y_sample = _trunk(x_sample, mem_sample, p)
    return (y_prompt, y_sample)
```

```python
import functools
import math

import jax
import jax.numpy as jnp
from jax import lax
from jax.experimental import pallas as pl
from jax.experimental.pallas import tpu as pltpu

F32 = jnp.float32
BF16 = jnp.bfloat16

D_MODEL = 2048
DEPTH = 4
N_BRANCH = 4
BRANCH_W = D_MODEL // 4
HY_W = BRANCH_W
HY_ORDER = 2
HY_EMB = 33
HY_BANDS = (HY_EMB - 1) // 2
HY_MIN_DECAY = math.log(1e-2) / 1.5
HY_MAX_DECAY = math.log(1e-2) / 0.3
HG_HEADS = 4
HG_DK = BRANCH_W // HG_HEADS
HG_DV = BRANCH_W // HG_HEADS
HG_CHUNK = 64
DA_HEADS = 4
DA_HD = BRANCH_W // (2 * DA_HEADS)
DA_VD = 2 * DA_HD
ROPE_DIM = DA_HD // 4
ROPE_THETA = 500000.0
MEM_HEADS = 4
MEM_HD = BRANCH_W // MEM_HEADS
N_EXPERTS = 16
EC_FACTOR = 2
EXPERT_FF = 2048
DN_ALPHA = (2 * DEPTH) ** 0.25
LN_EPS = 1e-5
HY_COLS = 3 * HY_W
HG_COLS = 5 * BRANCH_W
DA_COLS = 3 * BRANCH_W
ME_COLS = BRANCH_W
IN_W = HY_COLS + HG_COLS + DA_COLS + ME_COLS
DA_OFF = HY_COLS + HG_COLS
ME_OFF = DA_OFF + DA_COLS

V7X_VMEM_LIMIT_BYTES = 56 * 1024 * 1024
LANE = 128


def _cparams(sem):
    return pltpu.CompilerParams(dimension_semantics=sem, vmem_limit_bytes=V7X_VMEM_LIMIT_BYTES)


def _mm_kernel(a_ref, w_ref, o_ref):
    o_ref[...] = jnp.dot(a_ref[...].astype(BF16), w_ref[...], preferred_element_type=F32)


def _matmul(a, w, tm, tn):
    m, k = a.shape
    n = w.shape[1]
    return pl.pallas_call(
        _mm_kernel,
        out_shape=jax.ShapeDtypeStruct((m, n), F32),
        grid=(m // tm, n // tn),
        in_specs=[pl.BlockSpec((tm, k), lambda i, j: (i, 0)), pl.BlockSpec((k, tn), lambda i, j: (0, j))],
        out_specs=pl.BlockSpec((tm, tn), lambda i, j: (i, j)),
        compiler_params=_cparams(("parallel", "arbitrary")),
        name="dense_matmul",
    )(a, w)


def _ln_rows(y, g, b):
    mu = jnp.mean(y, axis=-1, keepdims=True)
    d = y - mu
    var = jnp.mean(d * d, axis=-1, keepdims=True)
    return d * lax.rsqrt(var + LN_EPS) * g + b


def _ln_kernel(alpha, x_ref, r_ref, g_ref, b_ref, o_ref):
    y = x_ref[...]
    if r_ref is not None:
        y = alpha * y + r_ref[...]
    o_ref[...] = _ln_rows(y, g_ref[...], b_ref[...])


def _layer_norm(x, g, b, resid=None, alpha=1.0, tm=512):
    t, d = x.shape
    row = pl.BlockSpec((tm, d), lambda i: (i, 0))
    vec = pl.BlockSpec((1, d), lambda i: (0, 0))
    if resid is None:
        kern = lambda x_ref, g_ref, b_ref, o_ref: _ln_kernel(alpha, x_ref, None, g_ref, b_ref, o_ref)
        args, specs = (x, g.reshape(1, d), b.reshape(1, d)), [row, vec, vec]
    else:
        kern = functools.partial(_ln_kernel, alpha)
        args, specs = (x, resid, g.reshape(1, d), b.reshape(1, d)), [row, row, vec, vec]
    return pl.pallas_call(
        kern,
        out_shape=jax.ShapeDtypeStruct((t, d), F32),
        grid=(t // tm,),
        in_specs=specs,
        out_specs=row,
        compiler_params=_cparams(("parallel",)),
        name="layer_norm",
    )(*args)


def _da_kernel(lam_init, tk, lp_ref, g_ref, q_ref, k_ref, v_ref, o_ref):
    tq = q_ref.shape[0]
    nk = k_ref.shape[0] // tk
    q = q_ref[...]
    qs = (q[:, :DA_HD], q[:, DA_HD:])

    def body(i, carry):
        off = pl.multiple_of(i * tk, tk)
        k = k_ref[pl.ds(off, tk), :]
        v = v_ref[pl.ds(off, tk), :]
        out = []
        for c in range(2):
            m_prev, l_prev, acc_prev = carry[c]
            s = lax.dot_general(qs[c], k[:, c * DA_HD:(c + 1) * DA_HD], (((1,), (1,)), ((), ())),
                                preferred_element_type=F32)
            m_new = jnp.maximum(m_prev, jnp.max(s, axis=-1, keepdims=True))
            a = jnp.exp(m_prev - m_new)
            p = jnp.exp(s - m_new)
            l_new = a * l_prev + jnp.sum(p, axis=-1, keepdims=True)
            acc_new = a * acc_prev + jnp.dot(p.astype(BF16), v, preferred_element_type=F32)
            out.append((m_new, l_new, acc_new))
        return tuple(out)

    init = tuple((jnp.full((tq, 1), -jnp.inf, F32), jnp.zeros((tq, 1), F32), jnp.zeros((tq, DA_VD), F32))
                 for _ in range(2))
    (_, l0, a0), (_, l1, a1) = lax.fori_loop(0, nk, body, init)
    lp = lp_ref[...]
    lam = (jnp.exp(jnp.sum(lp[0:1] * lp[1:2], axis=-1, keepdims=True))
           - jnp.exp(jnp.sum(lp[2:3] * lp[3:4], axis=-1, keepdims=True)) + lam_init)
    o = a0 / l0 - lam * (a1 / l1)
    o = o * lax.rsqrt(jnp.mean(o * o, axis=-1, keepdims=True) + 1e-6) * g_ref[...]
    o_ref[...] = o * (1.0 - lam_init)


def _diff_attention(qr, kr, vr, lam_params, norm_g, lam_init, tq=1024, tk=8192):
    b, h, l, _ = qr.shape
    nq = l // tq
    tk = min(tk, l)
    return pl.pallas_call(
        functools.partial(_da_kernel, lam_init, tk),
        out_shape=jax.ShapeDtypeStruct((b * l, BRANCH_W), F32),
        grid=(b, h, nq),
        in_specs=[
            pl.BlockSpec((4, DA_HD), lambda bi, hi, qi: (0, 0)),
            pl.BlockSpec((1, DA_VD), lambda bi, hi, qi: (0, 0)),
            pl.BlockSpec((None, None, tq, DA_VD), lambda bi, hi, qi: (bi, hi, qi, 0)),
            pl.BlockSpec((None, None, l, DA_VD), lambda bi, hi, qi: (bi, hi, 0, 0)),
            pl.BlockSpec((None, None, l, DA_VD), lambda bi, hi, qi: (bi, hi, 0, 0)),
        ],
        out_specs=pl.BlockSpec((tq, DA_VD), lambda bi, hi, qi: (bi * nq + qi, hi)),
        compiler_params=_cparams(("parallel", "parallel", "arbitrary")),
        name="diff_attention",
    )(lam_params, norm_g.reshape(1, DA_VD), qr, kr, vr)


def _mem_attn_kernel(q_ref, kv_ref, o_ref):
    q = q_ref[...]
    kv = kv_ref[...]
    scale = MEM_HD ** -0.5
    for h in range(MEM_HEADS):
        qh = q[:, h * MEM_HD:(h + 1) * MEM_HD].astype(BF16)
        kh = kv[:, h * MEM_HD:(h + 1) * MEM_HD].astype(BF16)
        vh = kv[:, BRANCH_W + h * MEM_HD:BRANCH_W + (h + 1) * MEM_HD].astype(BF16)
        s = lax.dot_general(qh, kh, (((1,), (1,)), ((), ())), preferred_element_type=F32) * scale
        e = jnp.exp(s - jnp.max(s, axis=-1, keepdims=True))
        p = e / jnp.sum(e, axis=-1, keepdims=True)
        o_ref[:, h * MEM_HD:(h + 1) * MEM_HD] = jnp.dot(p.astype(BF16), vh, preferred_element_type=F32)


def _memory_attention(z, kv, b, l, tm=512):
    nm = l // tm
    m = kv.shape[1]
    return pl.pallas_call(
        _mem_attn_kernel,
        out_shape=jax.ShapeDtypeStruct((b * l, BRANCH_W), F32),
        grid=(b, nm),
        in_specs=[
            pl.BlockSpec((tm, ME_COLS), lambda bi, i: (bi * nm + i, ME_OFF // ME_COLS)),
            pl.BlockSpec((None, m, 2 * BRANCH_W), lambda bi, i: (bi, 0, 0)),
        ],
        out_specs=pl.BlockSpec((tm, BRANCH_W), lambda bi, i: (bi * nm + i, 0)),
        compiler_params=_cparams(("parallel", "arbitrary")),
        name="memory_attention",
    )(z, kv)


DFT_NA = 128
DFT_NB = 128
HY_LBLK = 4096
HY_KB = 8
HIGHEST = lax.Precision.HIGHEST


def _dft_tables(l):
    assert 2 * l == DFT_NA * DFT_NB
    n = DFT_NA * DFT_NB
    i = jnp.arange(DFT_NA, dtype=jnp.int32)
    ang1 = (2.0 * math.pi / DFT_NA) * ((i[:, None] * i[None, :]) % DFT_NA).astype(F32)
    c1, s1 = jnp.cos(ang1), jnp.sin(ang1)
    half = DFT_NA // 2
    kk = i[:, None, None] + DFT_NA * i[None, :, None]
    ang = (2.0 * math.pi / n) * ((kk * i[None, None, :]) % n).astype(F32)
    gr, gi = jnp.cos(ang), -jnp.sin(ang)
    t = dict(
        f=jnp.concatenate([c1[:, :half], -s1[:, :half]], axis=0),
        gr=gr, gi=gi,
        gg=jnp.concatenate([gr, gi], axis=1),
        hh=jnp.concatenate([gr.transpose(0, 2, 1), gi.transpose(0, 2, 1)], axis=1),
        er=c1.T[:half] / n, ei=s1.T[:half] / n,
    )
    for name in ('f', 'gg', 'hh', 'er', 'ei'):
        t[name + '_bf'] = t[name].astype(BF16)
    return t


def _dft1_kernel(prec, x_ref, f_ref, ar_ref, ai_ref):
    x = x_ref[...]
    f = f_ref[...]
    if prec is None:
        a = jnp.dot(f, x.astype(BF16), preferred_element_type=F32)
    else:
        a = jnp.dot(f, x, precision=prec, preferred_element_type=F32)
    ar_ref[...] = a[:DFT_NA].astype(ar_ref.dtype)
    ai_ref[...] = a[DFT_NA:].astype(ai_ref.dtype)


def _dft_stage1(x, part, f, out_dtype, prec):
    _, g, r, w = x.shape
    spec_o = pl.BlockSpec((None, DFT_NA, HY_LBLK), lambda gi, j: (gi, 0, j))
    return pl.pallas_call(
        functools.partial(_dft1_kernel, prec),
        out_shape=[jax.ShapeDtypeStruct((g, DFT_NA, w), out_dtype)] * 2,
        grid=(g, w // HY_LBLK),
        in_specs=[pl.BlockSpec((None, None, r, HY_LBLK), lambda gi, j: (part, gi, 0, j)),
                  pl.BlockSpec((2 * DFT_NA, r), lambda gi, j: (0, 0))],
        out_specs=[spec_o, spec_o],
        compiler_params=_cparams(("parallel", "parallel")),
        name="hyena_dft_stage1",
    )(x, f)


def _filter_mlp_kernel(l, feats_ref, w1_ref, b1_ref, w2_ref, b2_ref, w3_ref, b3_ref, fq_ref, dl_ref, h_ref, nrm_ref):
    i = pl.program_id(0)
    tm = feats_ref.shape[0]
    dot = functools.partial(jnp.dot, precision=HIGHEST, preferred_element_type=F32)
    h = jnp.sin(fq_ref[0:1] * (dot(feats_ref[...], w1_ref[...]) + b1_ref[...]))
    h = jnp.sin(fq_ref[1:2] * (dot(h, w2_ref[...]) + b2_ref[...]))
    h = dot(h, w3_ref[...]) + b3_ref[...]
    row = i * tm + lax.broadcasted_iota(jnp.int32, (tm, 1), 0)
    t = row.astype(F32) * (1.0 / (l - 1))
    decay = jnp.exp(-t * dl_ref[...])
    h = h * jnp.concatenate([decay] * (2 * HY_ORDER), axis=1)
    col = lax.broadcasted_iota(jnp.int32, (1, 2 * HY_ORDER * HY_W), 1)
    bwd = jnp.bitwise_and(lax.shift_right_logical(col, HY_W.bit_length() - 1), 1) == 1
    h = jnp.where(jnp.logical_and(row == 0, bwd), 0.0, h)
    h_ref[...] = h

    @pl.when(i == 0)
    def _():
        nrm_ref[...] = jnp.zeros_like(nrm_ref)

    nrm_ref[...] += jnp.sum(jnp.abs(h), axis=0, keepdims=True)


def _filter_mlp(feats, w1, b1, w2, b2, w3, b3, freq, deltas, tm=1024):
    l = feats.shape[0]
    wcols = 2 * HY_ORDER * HY_W
    full = lambda a: pl.BlockSpec(a.shape, lambda i: (0,) * a.ndim)
    args = (feats, w1, b1.reshape(1, -1), w2, b2.reshape(1, -1), w3, b3.reshape(1, -1), freq, deltas.reshape(1, -1))
    return pl.pallas_call(
        functools.partial(_filter_mlp_kernel, l),
        out_shape=[jax.ShapeDtypeStruct((l, wcols), F32), jax.ShapeDtypeStruct((1, wcols), F32)],
        grid=(l // tm,),
        in_specs=[pl.BlockSpec((tm, feats.shape[1]), lambda i: (i, 0))] + [full(a) for a in args[1:]],
        out_specs=[pl.BlockSpec((tm, wcols), lambda i: (i, 0)), pl.BlockSpec((1, wcols), lambda i: (0, 0))],
        compiler_params=_cparams(("arbitrary",)),
        name="hyena_filter_mlp",
    )(*args)


def _filter_spec_kernel(afr_ref, afi_ref, abr_ref, abi_ref, gr_ref, gi_ref, nf_ref, nb_ref, kr_ref, ki_ref):
    dot = functools.partial(jnp.dot, precision=HIGHEST, preferred_element_type=F32)
    inv = 1.0 / (nf_ref[...] + nb_ref[...])
    for j in range(gr_ref.shape[0]):
        gr, gi = gr_ref[j], gi_ref[j]
        sr = afr_ref[j] + abr_ref[j]
        si = afi_ref[j] + abi_ref[j]
        dr = afr_ref[j] - abr_ref[j]
        di = afi_ref[j] - abi_ref[j]
        kr_ref[j] = (dot(gr, sr) - dot(gi, si)) * inv
        ki_ref[j] = (dot(gr, di) + dot(gi, dr)) * inv


def _filter_spectrum(ar, ai, gr, gi, nrm, kb=4, cw=256):
    ncw = HY_W // cw

    def a_spec(d):
        return pl.BlockSpec((kb, DFT_NB, cw), lambda o, ki, ci: (ki, 0, (2 * o + d) * ncw + ci))

    def n_spec(d):
        return pl.BlockSpec((1, cw), lambda o, ki, ci: (0, (2 * o + d) * ncw + ci))

    g_spec = pl.BlockSpec((kb, DFT_NB, DFT_NB), lambda o, ki, ci: (ki, 0, 0))
    o_spec = pl.BlockSpec((None, kb, DFT_NB, cw), lambda o, ki, ci: (o, ki, 0, ci))
    return pl.pallas_call(
        _filter_spec_kernel,
        out_shape=[jax.ShapeDtypeStruct((HY_ORDER, DFT_NA, DFT_NB, HY_W), F32)] * 2,
        grid=(HY_ORDER, DFT_NA // kb, ncw),
        in_specs=[a_spec(0), a_spec(0), a_spec(1), a_spec(1), g_spec, g_spec, n_spec(0), n_spec(1)],
        out_specs=[o_spec, o_spec],
        compiler_params=_cparams(("parallel", "parallel", "parallel")),
        name="hyena_filter_spectrum",
    )(ar, ai, ar, ai, gr, gi, nrm, nrm)


def _hy_mid_kernel(ar_ref, ai_ref, gg_ref, hh_ref, kr_ref, ki_ref, br_ref, bi_ref):
    for j in range(HY_KB):
        gg = gg_ref[j]
        p = jnp.dot(gg, ar_ref[j], preferred_element_type=F32)
        q = jnp.dot(gg, ai_ref[j], preferred_element_type=F32)
        xr = p[:DFT_NB] - q[DFT_NB:]
        xi = q[:DFT_NB] + p[DFT_NB:]
        kr, ki = kr_ref[j], ki_ref[j]
        yr = (xr * kr - xi * ki).astype(BF16)
        yi = (xr * ki + xi * kr).astype(BF16)
        hh = hh_ref[j]
        p = jnp.dot(hh, yr, preferred_element_type=F32)
        q = jnp.dot(hh, yi, preferred_element_type=F32)
        br_ref[j] = (p[:DFT_NB] + q[DFT_NB:]).astype(BF16)
        bi_ref[j] = (q[:DFT_NB] - p[DFT_NB:]).astype(BF16)


def _hy_mid(ar, ai, gg, hh, kr, ki):
    b = ar.shape[0]
    a_spec = pl.BlockSpec((None, HY_KB, DFT_NB, HY_W), lambda ki_, bi: (bi, ki_, 0, 0))
    t_spec = pl.BlockSpec((HY_KB, 2 * DFT_NB, DFT_NB), lambda ki_, bi: (ki_, 0, 0))
    k_spec = pl.BlockSpec((HY_KB, DFT_NB, HY_W), lambda ki_, bi: (ki_, 0, 0))
    return pl.pallas_call(
        _hy_mid_kernel,
        out_shape=[jax.ShapeDtypeStruct(ar.shape, BF16)] * 2,
        grid=(DFT_NA // HY_KB, b),
        in_specs=[a_spec, a_spec, t_spec, t_spec, k_spec, k_spec],
        out_specs=[a_spec, a_spec],
        compiler_params=_cparams(("parallel", "arbitrary")),
        name="hyena_dft_mid",
    )(ar, ai, gg, hh, kr, ki)


def _hy_out_kernel(br_ref, bi_ref, er_ref, ei_ref, u_ref, gate_ref, skip_ref, o_ref):
    y = (jnp.dot(er_ref[...], br_ref[...], preferred_element_type=F32)
         - jnp.dot(ei_ref[...], bi_ref[...], preferred_element_type=F32))
    o_ref[...] = gate_ref[...] * (y + u_ref[...] * skip_ref[...])


def _hy_out(br, bi, er, ei, u, upart, gate, gpart, skip_row):
    _, b, r, w = u.shape
    row = pl.BlockSpec((None, r, HY_LBLK), lambda bi_, j: (bi_, 0, j))
    prow = lambda part: pl.BlockSpec((None, None, r, HY_LBLK), lambda bi_, j: (part, bi_, 0, j))
    bspec = pl.BlockSpec((None, DFT_NA, HY_LBLK), lambda bi_, j: (bi_, 0, j))
    espec = pl.BlockSpec((r, DFT_NA), lambda bi_, j: (0, 0))
    return pl.pallas_call(
        _hy_out_kernel,
        out_shape=jax.ShapeDtypeStruct((b, r, w), F32),
        grid=(b, w // HY_LBLK),
        in_specs=[bspec, bspec, espec, espec, prow(upart), prow(gpart),
                  pl.BlockSpec((1, HY_LBLK), lambda bi_, j: (0, 0))],
        out_specs=row,
        compiler_params=_cparams(("parallel", "parallel")),
        name="hyena_dft_out",
    )(br, bi, er, ei, u, gate, skip_row)


def _short_conv_kernel(z_ref, w_ref, b_ref, o_ref):
    z = z_ref[...]
    l = z.shape[0]
    row = lax.broadcasted_iota(jnp.int32, (l, 1), 0)
    prev = jnp.where(row == 0, 0.0, pltpu.roll(z, 1, 0))
    nxt = jnp.where(row == l - 1, 0.0, pltpu.roll(z, l - 1, 0))
    w = w_ref[...]
    o_ref[...] = prev * w[0:1] + z * w[1:2] + nxt * w[2:3] + b_ref[...]


def _short_conv(z, w, bias, b, l):
    nc = HY_W // LANE
    return pl.pallas_call(
        _short_conv_kernel,
        out_shape=jax.ShapeDtypeStruct((3, b * l, HY_W), F32),
        grid=(b, 3, nc),
        in_specs=[pl.BlockSpec((l, LANE), lambda bi, p, ci: (bi, p * nc + ci)),
                  pl.BlockSpec((3, LANE), lambda bi, p, ci: (0, p * nc + ci)),
                  pl.BlockSpec((1, LANE), lambda bi, p, ci: (0, p * nc + ci))],
        out_specs=pl.BlockSpec((None, l, LANE), lambda bi, p, ci: (p, bi, ci)),
        compiler_params=_cparams(("parallel", "parallel", "parallel")),
        name="hyena_short_conv",
    )(z, w, bias.reshape(1, -1))


def _hyena_filter_spectrum(l, w1, b1, w2, b2, w3, b3, freq, tabs):
    t = jnp.linspace(0.0, 1.0, l, dtype=F32)[:, None]
    w = 2.0 * math.pi * jnp.arange(l, dtype=F32)[:, None] / l
    fr = jnp.linspace(1e-4, HY_BANDS - 1, HY_BANDS, dtype=F32)[None, :]
    feats = jnp.concatenate([t, jnp.cos(fr * w), -jnp.sin(fr * w)], axis=-1)
    deltas = jnp.abs(jnp.linspace(HY_MIN_DECAY, HY_MAX_DECAY, HY_W, dtype=F32))
    h, nrm = _filter_mlp(feats, w1, b1, w2, b2, w3, b3, freq, deltas)
    wcols = h.shape[1]
    ar, ai = _dft_stage1(h.reshape(1, 1, l // DFT_NB, DFT_NB * wcols), 0, tabs['f'], F32, HIGHEST)
    return _filter_spectrum(ar.reshape(DFT_NA, DFT_NB, wcols), ai.reshape(DFT_NA, DFT_NB, wcols),
                            tabs['gr'], tabs['gi'], nrm)


def _hyena(z, short_w, short_b, skip, kr, ki, tabs, b, l):
    rows = l // DFT_NB
    u = _short_conv(z, short_w, short_b, b, l).reshape(3, b, rows, DFT_NB * HY_W)
    v, vpart = u, 0
    for o in range(HY_ORDER):
        ar, ai = _dft_stage1(v, vpart, tabs['f_bf'], BF16, None)
        shp4 = (b, DFT_NA, DFT_NB, HY_W)
        br, bi = _hy_mid(ar.reshape(shp4), ai.reshape(shp4), tabs['gg_bf'], tabs['hh_bf'], kr[o], ki[o])
        shp3 = (b, DFT_NA, DFT_NB * HY_W)
        skip_row = jnp.tile(skip[o], HY_LBLK // HY_W).reshape(1, HY_LBLK)
        out = _hy_out(br.reshape(shp3), bi.reshape(shp3), tabs['er_bf'], tabs['ei_bf'], v, vpart, u, o + 1, skip_row)
        v, vpart = out[None], 0
    return out.reshape(b * l, HY_W)


HG_REC_CHUNK = 64
HG_SUB = 16
HG_TILE = 512
HG_HEADS_PER_STEP = 2
HG_UNROLL = 1


def _gate_terms(z, lb):
    a = jnp.log(lb)
    b = jnp.log1p(-lb) + (jnp.minimum(z, 0.0) - jnp.log1p(jnp.exp(-jnp.abs(z))))
    log_f = jnp.maximum(a, b) + jnp.log1p(jnp.exp(-jnp.abs(a - b)))
    return log_f, (1.0 - lb) * jax.nn.sigmoid(-z)


SUBLANES = 8


def _running_sum_rows(x, fwd):
    r = x.shape[0]
    row = jnp.bitwise_and(lax.broadcasted_iota(jnp.int32, (r, 1), 0), SUBLANES - 1)
    s = 1
    while s < SUBLANES:
        if fwd:
            x = x + jnp.where(row >= s, pltpu.roll(x, s, 0), 0.0)
        else:
            x = x + jnp.where(row < SUBLANES - s, pltpu.roll(x, r - s, 0), 0.0)
        s *= 2
    groups = [x[i:i + SUBLANES] for i in range(0, r, SUBLANES)]
    order = range(1, len(groups)) if fwd else range(len(groups) - 2, -1, -1)
    for i in order:
        prev = groups[i - 1][SUBLANES - 1:SUBLANES] if fwd else groups[i + 1][0:1]
        groups[i] = groups[i] + prev
    return jnp.concatenate(groups, axis=0)


def _gla_chunk(q, k, v, g, st, fwd):
    ch = q.shape[0]
    b = _running_sum_rows(g, fwd)
    nsb = ch // HG_SUB
    parts = [None] * nsb

    def add(i, val):
        parts[i] = val if parts[i] is None else parts[i] + val

    vb = v.astype(BF16)
    half = HG_SUB // 2
    b2 = b * math.log2(math.e)

    def halves(x, hi):
        return jnp.concatenate([x[i * HG_SUB + hi * half:i * HG_SUB + (hi + 1) * half] for i in range(nsb)], axis=0)

    qh, bh = [halves(q, 0), halves(q, 1)], [halves(b2, 0), halves(b2, 1)]
    nr = nsb * half
    li = lax.broadcasted_iota(jnp.int32, (nr, HG_SUB), 1)
    ti = jnp.bitwise_and(lax.broadcasted_iota(jnp.int32, (nr, HG_SUB), 0), half - 1)
    acc = [jnp.zeros((nr, HG_SUB), F32), jnp.zeros((nr, HG_SUB), F32)]
    for s in range(HG_SUB):
        src = [i * HG_SUB + s for i in range(nsb)]
        bs = jnp.concatenate([jnp.broadcast_to(b2[r:r + 1], (half, LANE)) for r in src], axis=0)
        ks = jnp.concatenate([jnp.broadcast_to(k[r:r + 1], (half, LANE)) for r in src], axis=0)
        for hi in range(2):
            lo_t, hi_t = hi * half, (hi + 1) * half - 1
            if (fwd and hi_t < s) or (not fwd and lo_t > s):
                continue
            e = jnp.exp2(bh[hi] - bs)
            acc[hi] = jnp.where(li == s, jnp.sum(qh[hi] * e * ks, axis=-1, keepdims=True), acc[hi])
    for hi in range(2):
        t_in = ti + hi * half
        keep = (li <= t_in) if fwd else (li >= t_in)
        acc[hi] = jnp.where(keep, acc[hi], 0.0)
    for i in range(nsb):
        sl = slice(i * HG_SUB, (i + 1) * HG_SUB)
        hs = slice(i * half, (i + 1) * half)
        a = jnp.concatenate([acc[0][hs], acc[1][hs]], axis=0).astype(BF16)
        add(i, jnp.dot(a, vb[sl], preferred_element_type=F32))
    h = HG_SUB
    while h < ch:
        for j in range(ch // (2 * h)):
            lo = slice(2 * h * j, 2 * h * j + h)
            hi = slice(2 * h * j + h, 2 * h * j + 2 * h)
            if fwd:
                rows, cols, ref = hi, lo, b[2 * h * j + h - 1:2 * h * j + h]
            else:
                rows, cols, ref = lo, hi, b[2 * h * j + h:2 * h * j + h + 1]
            qt = (q[rows] * jnp.exp(b[rows] - ref)).astype(BF16)
            kt = (k[cols] * jnp.exp(ref - b[cols])).astype(BF16)
            a = lax.dot_general(qt, kt, (((1,), (1,)), ((), ())), preferred_element_type=F32)
            ov = jnp.dot(a.astype(BF16), vb[cols], preferred_element_type=F32)
            for ii in range(h // HG_SUB):
                add(rows.start // HG_SUB + ii, ov[ii * HG_SUB:(ii + 1) * HG_SUB])
        h *= 2
    o = jnp.concatenate(parts, axis=0)
    qh = (q * jnp.exp(b)).astype(BF16)
    o = o + lax.dot_general(qh, st.astype(BF16), (((1,), (1,)), ((), ())), preferred_element_type=F32)
    btot = b[ch - 1:ch] if fwd else b[0:1]
    kh = (k * jnp.exp(btot - b)).astype(BF16)
    st_new = st * jnp.exp(btot) + lax.dot_general(vb, kh, (((0,), (0,)), ((), ())), preferred_element_type=F32)
    return o, st_new


def _hgrn2_kernel(lbf_ref, lbb_ref, ng_ref, zqf_ref, zif_ref, zff_ref, zgf_ref, zqb_ref, zib_ref, zbb_ref, zgb_ref,
                  o_ref, st_ref):
    c = pl.program_id(2)
    n = pl.num_programs(2)
    nch = HG_TILE // HG_REC_CHUNK

    @pl.when(c == 0)
    def _():
        st_ref[...] = jnp.zeros_like(st_ref)

    def body(second_pass, i, carry):
        off_f = pl.multiple_of(i * HG_REC_CHUNK, HG_REC_CHUNK)
        off_b = pl.multiple_of((nch - 1 - i) * HG_REC_CHUNK, HG_REC_CHUNK)
        for fwd, off, zq_ref, zi_ref, zd_ref, zg_ref, lb_ref, tile in (
                (True, off_f, zqf_ref, zif_ref, zff_ref, zgf_ref, lbf_ref, c),
                (False, off_b, zqb_ref, zib_ref, zbb_ref, zgb_ref, lbb_ref, n - 1 - c)):
            sl = pl.ds(off, HG_REC_CHUNK)
            rows = pl.ds(pl.multiple_of(tile * HG_TILE + off, HG_REC_CHUNK), HG_REC_CHUNK)
            d = 0 if fwd else 1
            for hh in range(HG_HEADS_PER_STEP):
                ln = slice(hh * LANE, (hh + 1) * LANE)
                g, k = _gate_terms(zd_ref[sl, ln], lb_ref[:, ln])
                o, st_new = _gla_chunk(jax.nn.silu(zq_ref[sl, ln]), k, zi_ref[sl, ln], g, st_ref[d, hh], fwd)
                st_ref[d, hh] = st_new
                if second_pass:
                    tot = o_ref[rows, ln] + o
                    tot = tot * lax.rsqrt(jnp.mean(tot * tot, axis=-1, keepdims=True) + 1e-6) * ng_ref[...]
                    o_ref[rows, ln] = tot * jax.nn.silu(zg_ref[sl, ln])
                else:
                    o_ref[rows, ln] = o
        return carry

    @pl.when(c < n // 2)
    def _():
        lax.fori_loop(0, nch, functools.partial(body, False), 0, unroll=HG_UNROLL)

    @pl.when(c >= n // 2)
    def _():
        lax.fori_loop(0, nch, functools.partial(body, True), 0, unroll=HG_UNROLL)


def _hgrn2(z, lb_fwd, lb_bwd, norm_g, b, l):
    n = l // HG_TILE
    assert n % 2 == 0
    w = HG_HEADS_PER_STEP * LANE
    base = HY_COLS // w
    nh = BRANCH_W // w

    def zspec(part, rev):
        if rev:
            return pl.BlockSpec((HG_TILE, w), lambda bi, hi, ci: (bi * n + n - 1 - ci, base + part * nh + hi))
        return pl.BlockSpec((HG_TILE, w), lambda bi, hi, ci: (bi * n + ci, base + part * nh + hi))

    lbspec = pl.BlockSpec((1, w), lambda bi, hi, ci: (0, hi))
    return pl.pallas_call(
        _hgrn2_kernel,
        out_shape=jax.ShapeDtypeStruct((b * l, BRANCH_W), F32),
        grid=(b, nh, n),
        in_specs=[lbspec, lbspec, pl.BlockSpec((1, LANE), lambda bi, hi, ci: (0, 0)),
                  zspec(0, False), zspec(1, False), zspec(2, False), zspec(4, False),
                  zspec(0, True), zspec(1, True), zspec(3, True), zspec(4, True)],
        out_specs=pl.BlockSpec((l, w), lambda bi, hi, ci: (bi, hi)),
        scratch_shapes=[pltpu.VMEM((2, HG_HEADS_PER_STEP, HG_DV, HG_DK), F32)],
        compiler_params=_cparams(("parallel", "parallel", "arbitrary")),
        name="hgrn2_scan",
    )(lb_fwd.reshape(1, BRANCH_W), lb_bwd.reshape(1, BRANCH_W), norm_g.reshape(1, HG_DV),
      z, z, z, z, z, z, z, z)


def _merge_kernel(x_ref, o0_ref, o1_ref, o2_ref, o3_ref, wg_ref, bg_ref, wbr_ref, wout_ref, g_ref, b_ref,
                  out_ref, acc_ref, xb_ref, ob_ref):
    j = pl.program_id(1)

    @pl.when(j == 0)
    def _():
        xb_ref[...] = x_ref[...].astype(BF16)
        for bi, o_ref in enumerate((o0_ref, o1_ref, o2_ref, o3_ref)):
            ob_ref[bi] = o_ref[...].astype(BF16)
        acc_ref[...] = jnp.zeros_like(acc_ref)

    xb = xb_ref[...]
    merged = None
    for bi in range(N_BRANCH):
        gate = jax.nn.sigmoid(jnp.dot(xb, wg_ref[bi], preferred_element_type=F32) + bg_ref[bi])
        term = gate * jnp.dot(ob_ref[bi], wbr_ref[bi], preferred_element_type=F32)
        merged = term if merged is None else merged + term
    acc_ref[...] += jnp.dot(merged.astype(BF16), wout_ref[...], preferred_element_type=F32)

    @pl.when(j == pl.num_programs(1) - 1)
    def _():
        out_ref[...] = _ln_rows(DN_ALPHA * x_ref[...] + acc_ref[...], g_ref[...], b_ref[...])


def _merge(x, branches, w_gate, b_gate, w_br, w_out, ln_g, ln_b, tm=512, tn=256):
    t, d = x.shape
    row = pl.BlockSpec((tm, d), lambda i, j: (i, 0))
    brow = pl.BlockSpec((tm, BRANCH_W), lambda i, j: (i, 0))
    vec = pl.BlockSpec((1, d), lambda i, j: (0, 0))
    return pl.pallas_call(
        _merge_kernel,
        out_shape=jax.ShapeDtypeStruct((t, d), F32),
        grid=(t // tm, d // tn),
        in_specs=[
            row, brow, brow, brow, brow,
            pl.BlockSpec((N_BRANCH, d, tn), lambda i, j: (0, 0, j)),
            pl.BlockSpec((N_BRANCH, 1, tn), lambda i, j: (0, 0, j)),
            pl.BlockSpec((N_BRANCH, BRANCH_W, tn), lambda i, j: (0, 0, j)),
            pl.BlockSpec((tn, d), lambda i, j: (j, 0)),
            vec, vec,
        ],
        out_specs=row,
        scratch_shapes=[pltpu.VMEM((tm, d), F32), pltpu.VMEM((tm, d), BF16),
                        pltpu.VMEM((N_BRANCH, tm, BRANCH_W), BF16)],
        compiler_params=_cparams(("parallel", "arbitrary")),
        name="gated_merge",
    )(x, *branches, w_gate, b_gate.reshape(N_BRANCH, 1, d), w_br, w_out, ln_g.reshape(1, d), ln_b.reshape(1, d))


def _expert_kernel(x_ref, gate_ref, w1_ref, w3_ref, w2_ref, o_ref):
    f = pl.program_id(2)
    x = x_ref[...]
    h1 = jnp.dot(x, w1_ref[...], preferred_element_type=F32)
    h3 = jnp.dot(x, w3_ref[...], preferred_element_type=F32)
    h = (jax.nn.silu(h1) * h3).astype(BF16)
    part = jnp.dot(h, w2_ref[...], preferred_element_type=F32)

    @pl.when(f == 0)
    def _():
        o_ref[...] = part

    @pl.when(f != 0)
    def _():
        o_ref[...] += part

    @pl.when(f == pl.num_programs(2) - 1)
    def _():
        o_ref[...] = o_ref[...] * gate_ref[...]


def _expert_ffn(xe, gate, w1, w3, w2, tm=512, tf=1024):
    e, c, d = xe.shape
    ff = w1.shape[2]
    tm = min(tm, c)
    return pl.pallas_call(
        _expert_kernel,
        out_shape=jax.ShapeDtypeStruct((e, c, d), F32),
        grid=(e, c // tm, ff // tf),
        in_specs=[
            pl.BlockSpec((None, tm, d), lambda ei, i, f: (ei, i, 0)),
            pl.BlockSpec((None, tm, 1), lambda ei, i, f: (ei, i, 0)),
            pl.BlockSpec((None, d, tf), lambda ei, i, f: (ei, 0, f)),
            pl.BlockSpec((None, d, tf), lambda ei, i, f: (ei, 0, f)),
            pl.BlockSpec((None, tf, d), lambda ei, i, f: (ei, f, 0)),
        ],
        out_specs=pl.BlockSpec((None, tm, d), lambda ei, i, f: (ei, i, 0)),
        compiler_params=_cparams(("parallel", "parallel", "arbitrary")),
        name="expert_ffn",
    )(xe, gate, w1, w3, w2)


RT_TM = 1024
RT_WIN = 32
RT_EAGER = 2
RT_CHUNKS = 4


def _router_kernel(x_ref, w_ref, o_ref):
    logits = lax.dot_general(w_ref[...], x_ref[...].astype(BF16), (((1,), (1,)), ((), ())),
                             preferred_element_type=F32)
    e = jnp.exp(logits - jnp.max(logits, axis=0, keepdims=True))
    aff = e / jnp.sum(e, axis=0, keepdims=True)
    for c in range(x_ref.shape[0] // LANE):
        o_ref[:, c, :] = aff[:, c * LANE:(c + 1) * LANE]


def _router(x, w_t):
    t, d = x.shape
    return pl.pallas_call(
        _router_kernel,
        out_shape=jax.ShapeDtypeStruct((N_EXPERTS, t // LANE, LANE), F32),
        grid=(t // RT_TM,),
        in_specs=[pl.BlockSpec((RT_TM, d), lambda i: (i, 0)), pl.BlockSpec((N_EXPERTS, d), lambda i: (0, 0))],
        out_specs=pl.BlockSpec((N_EXPERTS, RT_TM // LANE, LANE), lambda i: (0, i, 0)),
        compiler_params=_cparams(("parallel",)),
        name="moe_router",
    )(x, w_t)


def _prefix_count(mask):
    nch = mask.shape[0]
    m = jnp.where(mask, 1.0, 0.0)
    li = lax.broadcasted_iota(jnp.int32, (LANE, LANE), 0)
    lj = lax.broadcasted_iota(jnp.int32, (LANE, LANE), 1)
    incl = jnp.dot(m.astype(BF16), jnp.where(li <= lj, 1.0, 0.0).astype(BF16), preferred_element_type=F32)
    tot = jnp.broadcast_to(incl[:, LANE - 1:LANE], (nch, LANE))
    ri = lax.broadcasted_iota(jnp.int32, (nch, nch), 0)
    ci = lax.broadcasted_iota(jnp.int32, (nch, nch), 1)
    start = jnp.dot(jnp.where(ci < ri, 1.0, 0.0).astype(BF16), tot.astype(BF16), preferred_element_type=F32)
    return incl - m + start, start


def _route_select_kernel(cap, a_ref, pos_ref, start_ref):
    bits = pltpu.bitcast(a_ref[...], jnp.int32)

    def step(i, tau):
        cand = jnp.bitwise_or(tau, lax.shift_left(jnp.int32(1), 30 - i))
        cnt = jnp.sum(jnp.where(bits >= cand, 1.0, 0.0))
        return jnp.where(cnt >= cap, cand, tau)

    tau = lax.fori_loop(0, 31, step, jnp.int32(0))
    gt = bits > tau
    eq = bits == tau
    need = cap - jnp.sum(jnp.where(gt, 1.0, 0.0))
    eq_rank, _ = _prefix_count(eq)
    sel = jnp.logical_or(gt, jnp.logical_and(eq, eq_rank < need))
    pos, start = _prefix_count(sel)
    pos_ref[...] = jnp.where(sel, pos, -1.0)
    start_ref[...] = start


def _route_select(aff3, cap):
    e, nch, _ = aff3.shape
    spec = pl.BlockSpec((None, nch, LANE), lambda ei: (ei, 0, 0))
    return pl.pallas_call(
        functools.partial(_route_select_kernel, float(cap)),
        out_shape=[jax.ShapeDtypeStruct(aff3.shape, F32)] * 2,
        grid=(e,),
        in_specs=[spec],
        out_specs=[spec, spec],
        compiler_params=_cparams(("parallel",)),
        name="moe_route_select",
    )(aff3)


def _route_compact_kernel(cap, start_sm, pos_ref, a_ref, idx_ref, gate_ref, il_ref, gl_ref):
    ei = pl.program_id(0)
    nch = pos_ref.shape[0]
    il_ref[...] = jnp.zeros_like(il_ref)
    gl_ref[...] = jnp.zeros_like(gl_ref)
    s_iota = lax.broadcasted_iota(jnp.int32, (RT_WIN, LANE), 0).astype(F32)
    lane = lax.broadcasted_iota(jnp.int32, (1, LANE), 1).astype(F32)
    log_win = RT_WIN.bit_length() - 1

    def body(i, carry):
        rest = []
        for u in range(RT_CHUNKS):
            j = i * RT_CHUNKS + u
            first = start_sm[ei, j]
            nxt = jnp.where(j + 1 < nch, start_sm[ei, jnp.minimum(j + 1, nch - 1)], cap)
            base = lax.shift_left(lax.shift_right_logical(first, log_win), log_win)
            nwin = lax.shift_right_logical(nxt - base + (RT_WIN - 1), log_win)
            prow = pos_ref[pl.ds(j, 1), :]
            arow = a_ref[pl.ds(j, 1), :]
            tok = lane + jnp.asarray(j * LANE, F32)

            def window(w, c2, base=base, prow=prow, arow=arow, tok=tok):
                lo = pl.multiple_of(base + w * RT_WIN, RT_WIN)
                hit = (prow - jnp.asarray(lo, F32)) == s_iota
                il_ref[pl.ds(lo, RT_WIN), :] += jnp.sum(jnp.where(hit, tok, 0.0), axis=1, keepdims=True)
                gl_ref[pl.ds(lo, RT_WIN), :] += jnp.sum(jnp.where(hit, arow, 0.0), axis=1, keepdims=True)
                return c2

            for w in range(RT_EAGER):
                window(w, 0)
            rest.append((nwin, window))
        for nwin, window in rest:
            lax.fori_loop(RT_EAGER, nwin, window, 0)
        return carry

    lax.fori_loop(0, nch // RT_CHUNKS, body, 0)
    diag = (lax.broadcasted_iota(jnp.int32, (LANE, LANE), 0) == lax.broadcasted_iota(jnp.int32, (LANE, LANE), 1))
    for r in range(idx_ref.shape[0]):
        rows = slice(r * LANE, (r + 1) * LANE)
        idx_ref[r:r + 1, :] = jnp.sum(jnp.where(diag, il_ref[rows, :], 0.0), axis=0, keepdims=True).astype(jnp.int32)
        gate_ref[r:r + 1, :] = jnp.sum(jnp.where(diag, gl_ref[rows, :], 0.0), axis=0, keepdims=True)


def _route_compact(start, pos3, aff3, cap):
    e, nch, _ = pos3.shape
    spec = pl.BlockSpec((None, nch, LANE), lambda ei, s: (ei, 0, 0))
    ospec = pl.BlockSpec((None, cap // LANE, LANE), lambda ei, s: (ei, 0, 0))
    return pl.pallas_call(
        functools.partial(_route_compact_kernel, cap),
        out_shape=[jax.ShapeDtypeStruct((e, cap // LANE, LANE), jnp.int32),
                   jax.ShapeDtypeStruct((e, cap // LANE, LANE), F32)],
        grid_spec=pltpu.PrefetchScalarGridSpec(
            num_scalar_prefetch=1, grid=(e,), in_specs=[spec, spec], out_specs=[ospec, ospec],
            scratch_shapes=[pltpu.VMEM((cap + 2 * LANE, LANE), F32), pltpu.VMEM((cap + 2 * LANE, LANE), F32)]),
        compiler_params=_cparams(("parallel",)),
        name="moe_route_compact",
    )(start, pos3, aff3)


def _expert_choice(x, w_t, cap):
    aff3 = _router(x, w_t)
    pos3, start3 = _route_select(aff3, cap)
    idx3, gate3 = _route_compact(start3[:, :, 0].astype(jnp.int32), pos3, aff3, cap)
    return idx3.reshape(N_EXPERTS, cap), gate3.reshape(N_EXPERTS, cap)


CB_TS = 512
CB_SLOTS = 3


def _combine_kernel(nper, idx_ref, idxn_ref, ye_ref, y_in, y_ref, buf, sem_r, sem_w):
    del y_in
    g = pl.program_id(0)
    ng = pl.num_programs(0)
    i = lax.rem(g, nper)
    slot = lax.rem(g, CB_SLOTS)
    nslot = lax.rem(g + 1, CB_SLOTS)
    pslot = lax.rem(g + CB_SLOTS - 1, CB_SLOTS)
    first = i == 0

    def row_read(row, s, j):
        return pltpu.make_async_copy(y_ref.at[pl.ds(row, 1)], buf.at[s, pl.ds(j, 1)], sem_r.at[s])

    def row_write(row, s, j):
        return pltpu.make_async_copy(buf.at[s, pl.ds(j, 1)], y_ref.at[pl.ds(row, 1)], sem_w.at[s])

    def for_rows(fn):
        lax.fori_loop(0, CB_TS, lambda j, c: (fn(j), c)[1], 0, unroll=8)

    def start_reads(ids_ref, s):
        for_rows(lambda j: row_read(ids_ref[0, j], s, j).start())

    def wait_reads(s):
        for_rows(lambda j: row_read(0, s, 0).wait())

    def wait_writes(s):
        for_rows(lambda j: row_write(0, s, 0).wait())

    @pl.when(jnp.logical_and(g >= 2, i != 1))
    def _():
        wait_writes(nslot)

    @pl.when(jnp.logical_and(first, g >= 1))
    def _():
        wait_writes(pslot)

    @pl.when(first)
    def _():
        start_reads(idx_ref, slot)

    @pl.when(i != nper - 1)
    def _():
        start_reads(idxn_ref, nslot)

    wait_reads(slot)
    buf[slot] = buf[slot] + ye_ref[...]
    for_rows(lambda j: row_write(idx_ref[0, j], slot, j).start())

    @pl.when(g == ng - 1)
    def _():
        wait_writes(pslot)
        wait_writes(slot)


def _moe_combine(idx, ye, t):
    e, cap, d = ye.shape
    nper = cap // CB_TS
    assert nper >= 2
    ng = e * nper
    idx3 = idx.reshape(ng, 1, CB_TS)
    sm_spec = lambda off: pl.BlockSpec((None, 1, CB_TS), lambda g: (jnp.minimum(g + off, ng - 1), 0, 0),
                                       memory_space=pltpu.SMEM)
    return pl.pallas_call(
        functools.partial(_combine_kernel, nper),
        out_shape=jax.ShapeDtypeStruct((t, d), F32),
        grid=(ng,),
        in_specs=[sm_spec(0), sm_spec(1), pl.BlockSpec((CB_TS, d), lambda g: (g, 0)),
                  pl.BlockSpec(memory_space=pl.ANY)],
        out_specs=pl.BlockSpec(memory_space=pl.ANY),
        scratch_shapes=[pltpu.VMEM((CB_SLOTS, CB_TS, d), F32), pltpu.SemaphoreType.DMA((CB_SLOTS,)),
                        pltpu.SemaphoreType.DMA((CB_SLOTS,))],
        input_output_aliases={3: 0},
        compiler_params=_cparams(("arbitrary",)),
        name="moe_combine",
    )(idx3, idx3, ye.reshape(e * cap, d), jnp.zeros((t, d), F32))


def _gather_kernel(idx_ref, idxn_ref, x_ref, o_ref, buf, sem):
    g = pl.program_id(0)
    ng = pl.num_programs(0)
    slot = lax.rem(g, 2)

    def row_read(row, s, j):
        return pltpu.make_async_copy(x_ref.at[pl.ds(row, 1)], buf.at[s, pl.ds(j, 1)], sem.at[s])

    def start_reads(ids_ref, s):
        lax.fori_loop(0, CB_TS, lambda j, c: (row_read(ids_ref[0, j], s, j).start(), c)[1], 0, unroll=8)

    @pl.when(g == 0)
    def _():
        start_reads(idx_ref, slot)

    @pl.when(g + 1 < ng)
    def _():
        start_reads(idxn_ref, 1 - slot)

    lax.fori_loop(0, CB_TS, lambda j, c: (row_read(0, slot, 0).wait(), c)[1], 0, unroll=8)
    o_ref[...] = buf[slot].astype(BF16)


def _moe_gather(x, idx):
    e, cap = idx.shape
    d = x.shape[1]
    ng = e * cap // CB_TS
    idx3 = idx.reshape(ng, 1, CB_TS)
    sm_spec = lambda off: pl.BlockSpec((None, 1, CB_TS), lambda g: (jnp.minimum(g + off, ng - 1), 0, 0),
                                       memory_space=pltpu.SMEM)
    out = pl.pallas_call(
        _gather_kernel,
        out_shape=jax.ShapeDtypeStruct((e * cap, d), BF16),
        grid=(ng,),
        in_specs=[sm_spec(0), sm_spec(1), pl.BlockSpec(memory_space=pl.ANY)],
        out_specs=pl.BlockSpec((CB_TS, d), lambda g: (g, 0)),
        scratch_shapes=[pltpu.VMEM((2, CB_TS, d), F32), pltpu.SemaphoreType.DMA((2,))],
        compiler_params=_cparams(("arbitrary",)),
        name="moe_gather",
    )(idx3, idx3, x)
    return out.reshape(e, cap, d)


def _attn_prep_kernel(zq_ref, zk_ref, zv_ref, cos_ref, sin_ref, q_ref, k_ref, v_ref):
    lane = lax.broadcasted_iota(jnp.int32, (1, LANE), 1)
    first = jnp.bitwise_and(lane, DA_HD - 1) < ROPE_DIM // 2
    cos, sin = cos_ref[...], sin_ref[...]

    def rope(x):
        partner = jnp.where(first, pltpu.roll(x, LANE - ROPE_DIM // 2, 1), pltpu.roll(x, ROPE_DIM // 2, 1))
        return x * cos + partner * sin

    q_ref[...] = (rope(zq_ref[...]) * (DA_HD ** -0.5)).astype(BF16)
    k_ref[...] = rope(zk_ref[...]).astype(BF16)
    v_ref[...] = zv_ref[...].astype(BF16)


def _rope_lane_tables(l):
    inv = ROPE_THETA ** (-jnp.arange(0, ROPE_DIM, 2, dtype=F32) / ROPE_DIM)
    ang = jnp.arange(l, dtype=F32)[:, None] * inv[None, :]
    c, s = jnp.cos(ang), jnp.sin(ang)
    rest = DA_HD - ROPE_DIM
    cos_h = jnp.concatenate([c, c, jnp.ones((l, rest), F32)], axis=1)
    sin_h = jnp.concatenate([-s, s, jnp.zeros((l, rest), F32)], axis=1)
    return jnp.concatenate([cos_h, cos_h], axis=1), jnp.concatenate([sin_h, sin_h], axis=1)


def _attn_prep(z, cos, sin, b, l, tm=1024):
    n = l // tm
    base = DA_OFF // LANE

    def zspec(part):
        return pl.BlockSpec((tm, LANE), lambda bi, hi, i: (bi * n + i, base + part * DA_HEADS + hi))

    tspec = pl.BlockSpec((tm, LANE), lambda bi, hi, i: (i, 0))
    ospec = pl.BlockSpec((None, None, tm, LANE), lambda bi, hi, i: (bi, hi, i, 0))
    return pl.pallas_call(
        _attn_prep_kernel,
        out_shape=[jax.ShapeDtypeStruct((b, DA_HEADS, l, DA_VD), BF16)] * 3,
        grid=(b, DA_HEADS, n),
        in_specs=[zspec(0), zspec(1), zspec(2), tspec, tspec],
        out_specs=[ospec, ospec, ospec],
        compiler_params=_cparams(("parallel", "parallel", "parallel")),
        name="attention_prep",
    )(z, z, z, cos, sin)


def _trunk(x, mem, p):
    B, L, D = x.shape
    T = B * L
    cos, sin = _rope_lane_tables(L)
    lb_all = jnp.cumsum(jax.nn.softmax(p['hg_lb_raw'], axis=1), axis=1)
    lb_all = lb_all - lb_all[:, :1]
    xf = _layer_norm(x.reshape(T, D), p['ln_in_g'], p['ln_in_b'])
    memf = mem.reshape(B * mem.shape[1], D)
    cap = EC_FACTOR * T // N_EXPERTS
    for l in range(DEPTH):
        z = _matmul(xf, p['w_in'][l], 1024, 1024)
        hy_kr, hy_ki = p['hy_spectrum'][L][l]
        o_hy = _hyena(z, p['hy_short_w'][l], p['hy_short_b'][l], p['hy_skip'][l], hy_kr, hy_ki,
                      p['dft_tables'][L], B, L)
        o_hg = _hgrn2(z, lb_all[0, l], lb_all[1, l], p['hg_norm_g'][l], B, L)
        qr, kr, vr = _attn_prep(z, cos, sin, B, L)
        lam_init = 0.8 - 0.6 * math.exp(-0.3 * l)
        o_da = _diff_attention(qr, kr, vr, p['da_lambda'][l], p['da_norm_g'][l], lam_init)
        kv = _matmul(memf, p['w_mem_kv'][l], min(512, memf.shape[0]), 512).reshape(B, -1, 2 * BRANCH_W)
        o_me = _memory_attention(z, kv, B, L)
        xf = _merge(xf, (o_hy, o_hg, o_da, o_me), p['w_gate'][l], p['b_gate'][l], p['w_br'][l], p['w_out'][l],
                    p['ln1_g'][l], p['ln1_b'][l])
        idx, gate = _expert_choice(xf, p['w_router_t'][l], cap)
        xe = _moe_gather(xf, idx)
        ye = _expert_ffn(xe, gate[..., None], p['w_e1'][l], p['w_e3'][l], p['w_e2'][l])
        y = _moe_combine(idx, ye, T)
        xf = _layer_norm(xf, p['ln2_g'][l], p['ln2_b'][l], resid=y, alpha=DN_ALPHA)
    return xf.reshape(B, L, D)


def kernel(x_prompt, x_sample, mem_prompt, mem_sample, ln_in_g, ln_in_b, w_in, hy_short_w, hy_short_b, hy_ffn_w1, hy_ffn_b1, hy_ffn_w2, hy_ffn_b2, hy_ffn_w3, hy_ffn_b3, hy_freq, hy_skip, hg_lb_raw, hg_norm_g, da_lambda, da_norm_g, w_mem_kv, w_gate, b_gate, w_br, w_out, ln1_g, ln1_b, w_router, w_e1, w_e3, w_e2, ln2_g, ln2_b):
    p = dict(ln_in_g=ln_in_g, ln_in_b=ln_in_b, hy_short_w=hy_short_w, hy_short_b=hy_short_b,
             hy_ffn_w1=hy_ffn_w1, hy_ffn_b1=hy_ffn_b1, hy_ffn_w2=hy_ffn_w2, hy_ffn_b2=hy_ffn_b2,
             hy_ffn_w3=hy_ffn_w3, hy_ffn_b3=hy_ffn_b3, hy_freq=hy_freq, hy_skip=hy_skip, hg_lb_raw=hg_lb_raw,
             hg_norm_g=hg_norm_g, da_lambda=da_lambda, da_norm_g=da_norm_g, b_gate=b_gate,
             ln1_g=ln1_g, ln1_b=ln1_b, ln2_g=ln2_g, ln2_b=ln2_b)
    for name, w in (('w_in', w_in), ('w_mem_kv', w_mem_kv), ('w_gate', w_gate), ('w_br', w_br), ('w_out', w_out),
                    ('w_e1', w_e1), ('w_e3', w_e3), ('w_e2', w_e2)):
        p[name] = w.astype(BF16)
    p['w_router_t'] = jnp.swapaxes(w_router, 1, 2).astype(BF16)
    p['dft_tables'], p['hy_spectrum'] = {}, {}
    for seq_len in sorted({x_prompt.shape[1], x_sample.shape[1]}):
        tabs = _dft_tables(seq_len)
        p['dft_tables'][seq_len] = tabs
        p['hy_spectrum'][seq_len] = [
            _hyena_filter_spectrum(seq_len, hy_ffn_w1[l], hy_ffn_b1[l], hy_ffn_w2[l], hy_ffn_b2[l], hy_ffn_w3[l],
                                   hy_ffn_b3[l], hy_freq[l], tabs) for l in range(DEPTH)]
    y_prompt = _trunk(x_prompt, mem_prompt, p)
    y_sample = _trunk(x_sample, mem_sample, p)
    return (y_prompt, y_sample)
```

```python
import functools
import math

import jax
import jax.numpy as jnp
from jax import lax
from jax.experimental import pallas as pl
from jax.experimental.pallas import tpu as pltpu

F32 = jnp.float32
BF16 = jnp.bfloat16

D_MODEL = 2048
DEPTH = 4
N_BRANCH = 4
BRANCH_W = D_MODEL // 4
HY_W = BRANCH_W
HY_ORDER = 2
HY_EMB = 33
HY_BANDS = (HY_EMB - 1) // 2
HY_MIN_DECAY = math.log(1e-2) / 1.5
HY_MAX_DECAY = math.log(1e-2) / 0.3
HG_HEADS = 4
HG_DK = BRANCH_W // HG_HEADS
HG_DV = BRANCH_W // HG_HEADS
HG_CHUNK = 64
DA_HEADS = 4
DA_HD = BRANCH_W // (2 * DA_HEADS)
DA_VD = 2 * DA_HD
ROPE_DIM = DA_HD // 4
ROPE_THETA = 500000.0
MEM_HEADS = 4
MEM_HD = BRANCH_W // MEM_HEADS
N_EXPERTS = 16
EC_FACTOR = 2
EXPERT_FF = 2048
DN_ALPHA = (2 * DEPTH) ** 0.25
LN_EPS = 1e-5
HY_COLS = 3 * HY_W
HG_COLS = 5 * BRANCH_W
DA_COLS = 3 * BRANCH_W
ME_COLS = BRANCH_W
IN_W = HY_COLS + HG_COLS + DA_COLS + ME_COLS
DA_OFF = HY_COLS + HG_COLS
ME_OFF = DA_OFF + DA_COLS

V7X_VMEM_LIMIT_BYTES = 56 * 1024 * 1024
LANE = 128


def _cparams(sem):
    return pltpu.CompilerParams(dimension_semantics=sem, vmem_limit_bytes=V7X_VMEM_LIMIT_BYTES)


def _mm_kernel(a_ref, w_ref, o_ref):
    o_ref[...] = jnp.dot(a_ref[...].astype(BF16), w_ref[...], preferred_element_type=F32)


def _matmul(a, w, tm, tn):
    m, k = a.shape
    n = w.shape[1]
    return pl.pallas_call(
        _mm_kernel,
        out_shape=jax.ShapeDtypeStruct((m, n), F32),
        grid=(m // tm, n // tn),
        in_specs=[pl.BlockSpec((tm, k), lambda i, j: (i, 0)), pl.BlockSpec((k, tn), lambda i, j: (0, j))],
        out_specs=pl.BlockSpec((tm, tn), lambda i, j: (i, j)),
        compiler_params=_cparams(("parallel", "arbitrary")),
        name="dense_matmul",
    )(a, w)


def _ln_rows(y, g, b):
    mu = jnp.mean(y, axis=-1, keepdims=True)
    d = y - mu
    var = jnp.mean(d * d, axis=-1, keepdims=True)
    return d * lax.rsqrt(var + LN_EPS) * g + b


def _ln_kernel(alpha, x_ref, r_ref, g_ref, b_ref, o_ref):
    y = x_ref[...]
    if r_ref is not None:
        y = alpha * y + r_ref[...]
    o_ref[...] = _ln_rows(y, g_ref[...], b_ref[...])


def _layer_norm(x, g, b, resid=None, alpha=1.0, tm=512):
    t, d = x.shape
    row = pl.BlockSpec((tm, d), lambda i: (i, 0))
    vec = pl.BlockSpec((1, d), lambda i: (0, 0))
    if resid is None:
        kern = lambda x_ref, g_ref, b_ref, o_ref: _ln_kernel(alpha, x_ref, None, g_ref, b_ref, o_ref)
        args, specs = (x, g.reshape(1, d), b.reshape(1, d)), [row, vec, vec]
    else:
        kern = functools.partial(_ln_kernel, alpha)
        args, specs = (x, resid, g.reshape(1, d), b.reshape(1, d)), [row, row, vec, vec]
    return pl.pallas_call(
        kern,
        out_shape=jax.ShapeDtypeStruct((t, d), F32),
        grid=(t // tm,),
        in_specs=specs,
        out_specs=row,
        compiler_params=_cparams(("parallel",)),
        name="layer_norm",
    )(*args)


def _da_kernel(lam_init, tk, lp_ref, g_ref, q_ref, k_ref, v_ref, o_ref):
    tq = q_ref.shape[0]
    nk = k_ref.shape[0] // tk
    q = q_ref[...]
    qs = (q[:, :DA_HD], q[:, DA_HD:])

    def body(i, carry):
        off = pl.multiple_of(i * tk, tk)
        k = k_ref[pl.ds(off, tk), :]
        v = v_ref[pl.ds(off, tk), :]
        out = []
        for c in range(2):
            m_prev, l_prev, acc_prev = carry[c]
            s = lax.dot_general(qs[c], k[:, c * DA_HD:(c + 1) * DA_HD], (((1,), (1,)), ((), ())),
                                preferred_element_type=F32)
            m_new = jnp.maximum(m_prev, jnp.max(s, axis=-1, keepdims=True))
            a = jnp.exp(m_prev - m_new)
            p = jnp.exp(s - m_new)
            l_new = a * l_prev + jnp.sum(p, axis=-1, keepdims=True)
            acc_new = a * acc_prev + jnp.dot(p.astype(BF16), v, preferred_element_type=F32)
            out.append((m_new, l_new, acc_new))
        return tuple(out)

    init = tuple((jnp.full((tq, 1), -jnp.inf, F32), jnp.zeros((tq, 1), F32), jnp.zeros((tq, DA_VD), F32))
                 for _ in range(2))
    (_, l0, a0), (_, l1, a1) = lax.fori_loop(0, nk, body, init)
    lp = lp_ref[...]
    lam = (jnp.exp(jnp.sum(lp[0:1] * lp[1:2], axis=-1, keepdims=True))
           - jnp.exp(jnp.sum(lp[2:3] * lp[3:4], axis=-1, keepdims=True)) + lam_init)
    o = a0 / l0 - lam * (a1 / l1)
    o = o * lax.rsqrt(jnp.mean(o * o, axis=-1, keepdims=True) + 1e-6) * g_ref[...]
    o_ref[...] = o * (1.0 - lam_init)


def _diff_attention(qr, kr, vr, lam_params, norm_g, lam_init, tq=1024, tk=8192):
    b, h, l, _ = qr.shape
    nq = l // tq
    tk = min(tk, l)
    return pl.pallas_call(
        functools.partial(_da_kernel, lam_init, tk),
        out_shape=jax.ShapeDtypeStruct((b * l, BRANCH_W), F32),
        grid=(b, h, nq),
        in_specs=[
            pl.BlockSpec((4, DA_HD), lambda bi, hi, qi: (0, 0)),
            pl.BlockSpec((1, DA_VD), lambda bi, hi, qi: (0, 0)),
            pl.BlockSpec((None, None, tq, DA_VD), lambda bi, hi, qi: (bi, hi, qi, 0)),
            pl.BlockSpec((None, None, l, DA_VD), lambda bi, hi, qi: (bi, hi, 0, 0)),
            pl.BlockSpec((None, None, l, DA_VD), lambda bi, hi, qi: (bi, hi, 0, 0)),
        ],
        out_specs=pl.BlockSpec((tq, DA_VD), lambda bi, hi, qi: (bi * nq + qi, hi)),
        compiler_params=_cparams(("parallel", "parallel", "arbitrary")),
        name="diff_attention",
    )(lam_params, norm_g.reshape(1, DA_VD), qr, kr, vr)


def _mem_attn_kernel(q_ref, kv_ref, o_ref):
    q = q_ref[...]
    kv = kv_ref[...]
    scale = MEM_HD ** -0.5
    for h in range(MEM_HEADS):
        qh = q[:, h * MEM_HD:(h + 1) * MEM_HD].astype(BF16)
        kh = kv[:, h * MEM_HD:(h + 1) * MEM_HD].astype(BF16)
        vh = kv[:, BRANCH_W + h * MEM_HD:BRANCH_W + (h + 1) * MEM_HD].astype(BF16)
        s = lax.dot_general(qh, kh, (((1,), (1,)), ((), ())), preferred_element_type=F32) * scale
        e = jnp.exp(s - jnp.max(s, axis=-1, keepdims=True))
        p = e / jnp.sum(e, axis=-1, keepdims=True)
        o_ref[:, h * MEM_HD:(h + 1) * MEM_HD] = jnp.dot(p.astype(BF16), vh, preferred_element_type=F32)


def _memory_attention(z, kv, b, l, tm=512):
    nm = l // tm
    m = kv.shape[1]
    return pl.pallas_call(
        _mem_attn_kernel,
        out_shape=jax.ShapeDtypeStruct((b * l, BRANCH_W), F32),
        grid=(b, nm),
        in_specs=[
            pl.BlockSpec((tm, ME_COLS), lambda bi, i: (bi * nm + i, ME_OFF // ME_COLS)),
            pl.BlockSpec((None, m, 2 * BRANCH_W), lambda bi, i: (bi, 0, 0)),
        ],
        out_specs=pl.BlockSpec((tm, BRANCH_W), lambda bi, i: (bi * nm + i, 0)),
        compiler_params=_cparams(("parallel", "arbitrary")),
        name="memory_attention",
    )(z, kv)


DFT_NA = 128
DFT_NB = 128
HY_LBLK = 4096
HY_KB = 8
HIGHEST = lax.Precision.HIGHEST


def _dft_tables(l):
    assert 2 * l == DFT_NA * DFT_NB
    n = DFT_NA * DFT_NB
    i = jnp.arange(DFT_NA, dtype=jnp.int32)
    ang1 = (2.0 * math.pi / DFT_NA) * ((i[:, None] * i[None, :]) % DFT_NA).astype(F32)
    c1, s1 = jnp.cos(ang1), jnp.sin(ang1)
    half = DFT_NA // 2
    kk = i[:, None, None] + DFT_NA * i[None, :, None]
    ang = (2.0 * math.pi / n) * ((kk * i[None, None, :]) % n).astype(F32)
    gr, gi = jnp.cos(ang), -jnp.sin(ang)
    t = dict(
        f=jnp.concatenate([c1[:, :half], -s1[:, :half]], axis=0),
        gr=gr, gi=gi,
        gg=jnp.concatenate([gr, gi], axis=1),
        hh=jnp.concatenate([gr.transpose(0, 2, 1), gi.transpose(0, 2, 1)], axis=1),
        er=c1.T[:half] / n, ei=s1.T[:half] / n,
    )
    for name in ('f', 'gg', 'hh', 'er', 'ei'):
        t[name + '_bf'] = t[name].astype(BF16)
    return t


def _dft1_kernel(prec, x_ref, f_ref, ar_ref, ai_ref):
    x = x_ref[...]
    f = f_ref[...]
    if prec is None:
        a = jnp.dot(f, x.astype(BF16), preferred_element_type=F32)
    else:
        a = jnp.dot(f, x, precision=prec, preferred_element_type=F32)
    ar_ref[...] = a[:DFT_NA].astype(ar_ref.dtype)
    ai_ref[...] = a[DFT_NA:].astype(ai_ref.dtype)


def _dft_stage1(x, part, f, out_dtype, prec):
    _, g, r, w = x.shape
    spec_o = pl.BlockSpec((None, DFT_NA, HY_LBLK), lambda gi, j: (gi, 0, j))
    return pl.pallas_call(
        functools.partial(_dft1_kernel, prec),
        out_shape=[jax.ShapeDtypeStruct((g, DFT_NA, w), out_dtype)] * 2,
        grid=(g, w // HY_LBLK),
        in_specs=[pl.BlockSpec((None, None, r, HY_LBLK), lambda gi, j: (part, gi, 0, j)),
                  pl.BlockSpec((2 * DFT_NA, r), lambda gi, j: (0, 0))],
        out_specs=[spec_o, spec_o],
        compiler_params=_cparams(("parallel", "parallel")),
        name="hyena_dft_stage1",
    )(x, f)


def _filter_mlp_kernel(l, feats_ref, w1_ref, b1_ref, w2_ref, b2_ref, w3_ref, b3_ref, fq_ref, dl_ref, h_ref, nrm_ref):
    i = pl.program_id(0)
    tm = feats_ref.shape[0]
    dot = functools.partial(jnp.dot, precision=HIGHEST, preferred_element_type=F32)
    h = jnp.sin(fq_ref[0:1] * (dot(feats_ref[...], w1_ref[...]) + b1_ref[...]))
    h = jnp.sin(fq_ref[1:2] * (dot(h, w2_ref[...]) + b2_ref[...]))
    h = dot(h, w3_ref[...]) + b3_ref[...]
    row = i * tm + lax.broadcasted_iota(jnp.int32, (tm, 1), 0)
    t = row.astype(F32) * (1.0 / (l - 1))
    decay = jnp.exp(-t * dl_ref[...])
    h = h * jnp.concatenate([decay] * (2 * HY_ORDER), axis=1)
    col = lax.broadcasted_iota(jnp.int32, (1, 2 * HY_ORDER * HY_W), 1)
    bwd = jnp.bitwise_and(lax.shift_right_logical(col, HY_W.bit_length() - 1), 1) == 1
    h = jnp.where(jnp.logical_and(row == 0, bwd), 0.0, h)
    h_ref[...] = h

    @pl.when(i == 0)
    def _():
        nrm_ref[...] = jnp.zeros_like(nrm_ref)

    nrm_ref[...] += jnp.sum(jnp.abs(h), axis=0, keepdims=True)


def _filter_mlp(feats, w1, b1, w2, b2, w3, b3, freq, deltas, tm=1024):
    l = feats.shape[0]
    wcols = 2 * HY_ORDER * HY_W
    full = lambda a: pl.BlockSpec(a.shape, lambda i: (0,) * a.ndim)
    args = (feats, w1, b1.reshape(1, -1), w2, b2.reshape(1, -1), w3, b3.reshape(1, -1), freq, deltas.reshape(1, -1))
    return pl.pallas_call(
        functools.partial(_filter_mlp_kernel, l),
        out_shape=[jax.ShapeDtypeStruct((l, wcols), F32), jax.ShapeDtypeStruct((1, wcols), F32)],
        grid=(l // tm,),
        in_specs=[pl.BlockSpec((tm, feats.shape[1]), lambda i: (i, 0))] + [full(a) for a in args[1:]],
        out_specs=[pl.BlockSpec((tm, wcols), lambda i: (i, 0)), pl.BlockSpec((1, wcols), lambda i: (0, 0))],
        compiler_params=_cparams(("arbitrary",)),
        name="hyena_filter_mlp",
    )(*args)


def _filter_spec_kernel(afr_ref, afi_ref, abr_ref, abi_ref, gr_ref, gi_ref, nf_ref, nb_ref, kr_ref, ki_ref):
    dot = functools.partial(jnp.dot, precision=HIGHEST, preferred_element_type=F32)
    inv = 1.0 / (nf_ref[...] + nb_ref[...])
    for j in range(gr_ref.shape[0]):
        gr, gi = gr_ref[j], gi_ref[j]
        sr = afr_ref[j] + abr_ref[j]
        si = afi_ref[j] + abi_ref[j]
        dr = afr_ref[j] - abr_ref[j]
        di = afi_ref[j] - abi_ref[j]
        kr_ref[j] = (dot(gr, sr) - dot(gi, si)) * inv
        ki_ref[j] = (dot(gr, di) + dot(gi, dr)) * inv


def _filter_spectrum(ar, ai, gr, gi, nrm, kb=4, cw=256):
    ncw = HY_W // cw

    def a_spec(d):
        return pl.BlockSpec((kb, DFT_NB, cw), lambda o, ki, ci: (ki, 0, (2 * o + d) * ncw + ci))

    def n_spec(d):
        return pl.BlockSpec((1, cw), lambda o, ki, ci: (0, (2 * o + d) * ncw + ci))

    g_spec = pl.BlockSpec((kb, DFT_NB, DFT_NB), lambda o, ki, ci: (ki, 0, 0))
    o_spec = pl.BlockSpec((None, kb, DFT_NB, cw), lambda o, ki, ci: (o, ki, 0, ci))
    return pl.pallas_call(
        _filter_spec_kernel,
        out_shape=[jax.ShapeDtypeStruct((HY_ORDER, DFT_NA, DFT_NB, HY_W), F32)] * 2,
        grid=(HY_ORDER, DFT_NA // kb, ncw),
        in_specs=[a_spec(0), a_spec(0), a_spec(1), a_spec(1), g_spec, g_spec, n_spec(0), n_spec(1)],
        out_specs=[o_spec, o_spec],
        compiler_params=_cparams(("parallel", "parallel", "parallel")),
        name="hyena_filter_spectrum",
    )(ar, ai, ar, ai, gr, gi, nrm, nrm)


def _hy_mid_kernel(ar_ref, ai_ref, gg_ref, hh_ref, kr_ref, ki_ref, br_ref, bi_ref):
    for j in range(HY_KB):
        gg = gg_ref[j]
        p = jnp.dot(gg, ar_ref[j], preferred_element_type=F32)
        q = jnp.dot(gg, ai_ref[j], preferred_element_type=F32)
        xr = p[:DFT_NB] - q[DFT_NB:]
        xi = q[:DFT_NB] + p[DFT_NB:]
        kr, ki = kr_ref[j], ki_ref[j]
        yr = (xr * kr - xi * ki).astype(BF16)
        yi = (xr * ki + xi * kr).astype(BF16)
        hh = hh_ref[j]
        p = jnp.dot(hh, yr, preferred_element_type=F32)
        q = jnp.dot(hh, yi, preferred_element_type=F32)
        br_ref[j] = (p[:DFT_NB] + q[DFT_NB:]).astype(BF16)
        bi_ref[j] = (q[:DFT_NB] - p[DFT_NB:]).astype(BF16)


def _hy_mid(ar, ai, gg, hh, kr, ki):
    b = ar.shape[0]
    a_spec = pl.BlockSpec((None, HY_KB, DFT_NB, HY_W), lambda ki_, bi: (bi, ki_, 0, 0))
    t_spec = pl.BlockSpec((HY_KB, 2 * DFT_NB, DFT_NB), lambda ki_, bi: (ki_, 0, 0))
    k_spec = pl.BlockSpec((HY_KB, DFT_NB, HY_W), lambda ki_, bi: (ki_, 0, 0))
    return pl.pallas_call(
        _hy_mid_kernel,
        out_shape=[jax.ShapeDtypeStruct(ar.shape, BF16)] * 2,
        grid=(DFT_NA // HY_KB, b),
        in_specs=[a_spec, a_spec, t_spec, t_spec, k_spec, k_spec],
        out_specs=[a_spec, a_spec],
        compiler_params=_cparams(("parallel", "arbitrary")),
        name="hyena_dft_mid",
    )(ar, ai, gg, hh, kr, ki)


def _hy_out_kernel(br_ref, bi_ref, er_ref, ei_ref, u_ref, gate_ref, skip_ref, o_ref):
    y = (jnp.dot(er_ref[...], br_ref[...], preferred_element_type=F32)
         - jnp.dot(ei_ref[...], bi_ref[...], preferred_element_type=F32))
    o_ref[...] = gate_ref[...] * (y + u_ref[...] * skip_ref[...])


def _hy_out(br, bi, er, ei, u, upart, gate, gpart, skip_row):
    _, b, r, w = u.shape
    row = pl.BlockSpec((None, r, HY_LBLK), lambda bi_, j: (bi_, 0, j))
    prow = lambda part: pl.BlockSpec((None, None, r, HY_LBLK), lambda bi_, j: (part, bi_, 0, j))
    bspec = pl.BlockSpec((None, DFT_NA, HY_LBLK), lambda bi_, j: (bi_, 0, j))
    espec = pl.BlockSpec((r, DFT_NA), lambda bi_, j: (0, 0))
    return pl.pallas_call(
        _hy_out_kernel,
        out_shape=jax.ShapeDtypeStruct((b, r, w), F32),
        grid=(b, w // HY_LBLK),
        in_specs=[bspec, bspec, espec, espec, prow(upart), prow(gpart),
                  pl.BlockSpec((1, HY_LBLK), lambda bi_, j: (0, 0))],
        out_specs=row,
        compiler_params=_cparams(("parallel", "parallel")),
        name="hyena_dft_out",
    )(br, bi, er, ei, u, gate, skip_row)


def _short_conv_kernel(z_ref, w_ref, b_ref, o_ref):
    z = z_ref[...]
    l = z.shape[0]
    row = lax.broadcasted_iota(jnp.int32, (l, 1), 0)
    prev = jnp.where(row == 0, 0.0, pltpu.roll(z, 1, 0))
    nxt = jnp.where(row == l - 1, 0.0, pltpu.roll(z, l - 1, 0))
    w = w_ref[...]
    o_ref[...] = prev * w[0:1] + z * w[1:2] + nxt * w[2:3] + b_ref[...]


def _short_conv(z, w, bias, b, l):
    nc = HY_W // LANE
    return pl.pallas_call(
        _short_conv_kernel,
        out_shape=jax.ShapeDtypeStruct((3, b * l, HY_W), F32),
        grid=(b, 3, nc),
        in_specs=[pl.BlockSpec((l, LANE), lambda bi, p, ci: (bi, p * nc + ci)),
                  pl.BlockSpec((3, LANE), lambda bi, p, ci: (0, p * nc + ci)),
                  pl.BlockSpec((1, LANE), lambda bi, p, ci: (0, p * nc + ci))],
        out_specs=pl.BlockSpec((None, l, LANE), lambda bi, p, ci: (p, bi, ci)),
        compiler_params=_cparams(("parallel", "parallel", "parallel")),
        name="hyena_short_conv",
    )(z, w, bias.reshape(1, -1))


def _hyena_filter_spectrum(l, w1, b1, w2, b2, w3, b3, freq, tabs):
    t = jnp.linspace(0.0, 1.0, l, dtype=F32)[:, None]
    w = 2.0 * math.pi * jnp.arange(l, dtype=F32)[:, None] / l
    fr = jnp.linspace(1e-4, HY_BANDS - 1, HY_BANDS, dtype=F32)[None, :]
    feats = jnp.concatenate([t, jnp.cos(fr * w), -jnp.sin(fr * w)], axis=-1)
    deltas = jnp.abs(jnp.linspace(HY_MIN_DECAY, HY_MAX_DECAY, HY_W, dtype=F32))
    h, nrm = _filter_mlp(feats, w1, b1, w2, b2, w3, b3, freq, deltas)
    wcols = h.shape[1]
    ar, ai = _dft_stage1(h.reshape(1, 1, l // DFT_NB, DFT_NB * wcols), 0, tabs['f'], F32, HIGHEST)
    return _filter_spectrum(ar.reshape(DFT_NA, DFT_NB, wcols), ai.reshape(DFT_NA, DFT_NB, wcols),
                            tabs['gr'], tabs['gi'], nrm)


def _hyena(z, short_w, short_b, skip, kr, ki, tabs, b, l):
    rows = l // DFT_NB
    u = _short_conv(z, short_w, short_b, b, l).reshape(3, b, rows, DFT_NB * HY_W)
    v, vpart = u, 0
    for o in range(HY_ORDER):
        ar, ai = _dft_stage1(v, vpart, tabs['f_bf'], BF16, None)
        shp4 = (b, DFT_NA, DFT_NB, HY_W)
        br, bi = _hy_mid(ar.reshape(shp4), ai.reshape(shp4), tabs['gg_bf'], tabs['hh_bf'], kr[o], ki[o])
        shp3 = (b, DFT_NA, DFT_NB * HY_W)
        skip_row = jnp.tile(skip[o], HY_LBLK // HY_W).reshape(1, HY_LBLK)
        out = _hy_out(br.reshape(shp3), bi.reshape(shp3), tabs['er_bf'], tabs['ei_bf'], v, vpart, u, o + 1, skip_row)
        v, vpart = out[None], 0
    return out.reshape(b * l, HY_W)


HG_REC_CHUNK = 64
HG_SUB = 16
HG_TILE = 512
HG_HEADS_PER_STEP = 2
HG_UNROLL = 1


def _gate_terms(z, lb):
    a = jnp.log(lb)
    b = jnp.log1p(-lb) + (jnp.minimum(z, 0.0) - jnp.log1p(jnp.exp(-jnp.abs(z))))
    log_f = jnp.maximum(a, b) + jnp.log1p(jnp.exp(-jnp.abs(a - b)))
    return log_f, (1.0 - lb) * jax.nn.sigmoid(-z)


SUBLANES = 8


def _running_sum_rows(x, fwd):
    r = x.shape[0]
    row = jnp.bitwise_and(lax.broadcasted_iota(jnp.int32, (r, 1), 0), SUBLANES - 1)
    s = 1
    while s < SUBLANES:
        if fwd:
            x = x + jnp.where(row >= s, pltpu.roll(x, s, 0), 0.0)
        else:
            x = x + jnp.where(row < SUBLANES - s, pltpu.roll(x, r - s, 0), 0.0)
        s *= 2
    groups = [x[i:i + SUBLANES] for i in range(0, r, SUBLANES)]
    order = range(1, len(groups)) if fwd else range(len(groups) - 2, -1, -1)
    for i in order:
        prev = groups[i - 1][SUBLANES - 1:SUBLANES] if fwd else groups[i + 1][0:1]
        groups[i] = groups[i] + prev
    return jnp.concatenate(groups, axis=0)


def _gla_chunk(q, k, v, g, st, fwd):
    ch = q.shape[0]
    b = _running_sum_rows(g, fwd)
    nsb = ch // HG_SUB
    parts = [None] * nsb

    def add(i, val):
        parts[i] = val if parts[i] is None else parts[i] + val

    vb = v.astype(BF16)
    half = HG_SUB // 2
    b2 = b * math.log2(math.e)

    def halves(x, hi):
        return jnp.concatenate([x[i * HG_SUB + hi * half:i * HG_SUB + (hi + 1) * half] for i in range(nsb)], axis=0)

    qh, bh = [halves(q, 0), halves(q, 1)], [halves(b2, 0), halves(b2, 1)]
    nr = nsb * half
    li = lax.broadcasted_iota(jnp.int32, (nr, HG_SUB), 1)
    ti = jnp.bitwise_and(lax.broadcasted_iota(jnp.int32, (nr, HG_SUB), 0), half - 1)
    acc = [jnp.zeros((nr, HG_SUB), F32), jnp.zeros((nr, HG_SUB), F32)]
    for s in range(HG_SUB):
        src = [i * HG_SUB + s for i in range(nsb)]
        bs = jnp.concatenate([jnp.broadcast_to(b2[r:r + 1], (half, LANE)) for r in src], axis=0)
        ks = jnp.concatenate([jnp.broadcast_to(k[r:r + 1], (half, LANE)) for r in src], axis=0)
        for hi in range(2):
            lo_t, hi_t = hi * half, (hi + 1) * half - 1
            if (fwd and hi_t < s) or (not fwd and lo_t > s):
                continue
            e = jnp.exp2(bh[hi] - bs)
            acc[hi] = jnp.where(li == s, jnp.sum(qh[hi] * e * ks, axis=-1, keepdims=True), acc[hi])
    for hi in range(2):
        t_in = ti + hi * half
        keep = (li <= t_in) if fwd else (li >= t_in)
        acc[hi] = jnp.where(keep, acc[hi], 0.0)
    for i in range(nsb):
        sl = slice(i * HG_SUB, (i + 1) * HG_SUB)
        hs = slice(i * half, (i + 1) * half)
        a = jnp.concatenate([acc[0][hs], acc[1][hs]], axis=0).astype(BF16)
        add(i, jnp.dot(a, vb[sl], preferred_element_type=F32))
    h = HG_SUB
    while h < ch:
        for j in range(ch // (2 * h)):
            lo = slice(2 * h * j, 2 * h * j + h)
            hi = slice(2 * h * j + h, 2 * h * j + 2 * h)
            if fwd:
                rows, cols, ref = hi, lo, b[2 * h * j + h - 1:2 * h * j + h]
            else:
                rows, cols, ref = lo, hi, b[2 * h * j + h:2 * h * j + h + 1]
            qt = (q[rows] * jnp.exp(b[rows] - ref)).astype(BF16)
            kt = (k[cols] * jnp.exp(ref - b[cols])).astype(BF16)
            a = lax.dot_general(qt, kt, (((1,), (1,)), ((), ())), preferred_element_type=F32)
            ov = jnp.dot(a.astype(BF16), vb[cols], preferred_element_type=F32)
            for ii in range(h // HG_SUB):
                add(rows.start // HG_SUB + ii, ov[ii * HG_SUB:(ii + 1) * HG_SUB])
        h *= 2
    o = jnp.concatenate(parts, axis=0)
    qh = (q * jnp.exp(b)).astype(BF16)
    o = o + lax.dot_general(qh, st.astype(BF16), (((1,), (1,)), ((), ())), preferred_element_type=F32)
    btot = b[ch - 1:ch] if fwd else b[0:1]
    kh = (k * jnp.exp(btot - b)).astype(BF16)
    st_new = st * jnp.exp(btot) + lax.dot_general(vb, kh, (((0,), (0,)), ((), ())), preferred_element_type=F32)
    return o, st_new


def _hgrn2_kernel(lbf_ref, lbb_ref, ng_ref, zqf_ref, zif_ref, zff_ref, zgf_ref, zqb_ref, zib_ref, zbb_ref, zgb_ref,
                  o_ref, st_ref):
    c = pl.program_id(2)
    n = pl.num_programs(2)
    nch = HG_TILE // HG_REC_CHUNK

    @pl.when(c == 0)
    def _():
        st_ref[...] = jnp.zeros_like(st_ref)

    def body(second_pass, i, carry):
        off_f = pl.multiple_of(i * HG_REC_CHUNK, HG_REC_CHUNK)
        off_b = pl.multiple_of((nch - 1 - i) * HG_REC_CHUNK, HG_REC_CHUNK)
        for fwd, off, zq_ref, zi_ref, zd_ref, zg_ref, lb_ref, tile in (
                (True, off_f, zqf_ref, zif_ref, zff_ref, zgf_ref, lbf_ref, c),
                (False, off_b, zqb_ref, zib_ref, zbb_ref, zgb_ref, lbb_ref, n - 1 - c)):
            sl = pl.ds(off, HG_REC_CHUNK)
            rows = pl.ds(pl.multiple_of(tile * HG_TILE + off, HG_REC_CHUNK), HG_REC_CHUNK)
            d = 0 if fwd else 1
            for hh in range(HG_HEADS_PER_STEP):
                ln = slice(hh * LANE, (hh + 1) * LANE)
                g, k = _gate_terms(zd_ref[sl, ln], lb_ref[:, ln])
                o, st_new = _gla_chunk(jax.nn.silu(zq_ref[sl, ln]), k, zi_ref[sl, ln], g, st_ref[d, hh], fwd)
                st_ref[d, hh] = st_new
                if second_pass:
                    tot = o_ref[rows, ln] + o
                    tot = tot * lax.rsqrt(jnp.mean(tot * tot, axis=-1, keepdims=True) + 1e-6) * ng_ref[...]
                    o_ref[rows, ln] = tot * jax.nn.silu(zg_ref[sl, ln])
                else:
                    o_ref[rows, ln] = o
        return carry

    @pl.when(c < n // 2)
    def _():
        lax.fori_loop(0, nch, functools.partial(body, False), 0, unroll=HG_UNROLL)

    @pl.when(c >= n // 2)
    def _():
        lax.fori_loop(0, nch, functools.partial(body, True), 0, unroll=HG_UNROLL)


def _hgrn2(z, lb_fwd, lb_bwd, norm_g, b, l):
    n = l // HG_TILE
    assert n % 2 == 0
    w = HG_HEADS_PER_STEP * LANE
    base = HY_COLS // w
    nh = BRANCH_W // w

    def zspec(part, rev):
        if rev:
            return pl.BlockSpec((HG_TILE, w), lambda bi, hi, ci: (bi * n + n - 1 - ci, base + part * nh + hi))
        return pl.BlockSpec((HG_TILE, w), lambda bi, hi, ci: (bi * n + ci, base + part * nh + hi))

    lbspec = pl.BlockSpec((1, w), lambda bi, hi, ci: (0, hi))
    return pl.pallas_call(
        _hgrn2_kernel,
        out_shape=jax.ShapeDtypeStruct((b * l, BRANCH_W), F32),
        grid=(b, nh, n),
        in_specs=[lbspec, lbspec, pl.BlockSpec((1, LANE), lambda bi, hi, ci: (0, 0)),
                  zspec(0, False), zspec(1, False), zspec(2, False), zspec(4, False),
                  zspec(0, True), zspec(1, True), zspec(3, True), zspec(4, True)],
        out_specs=pl.BlockSpec((l, w), lambda bi, hi, ci: (bi, hi)),
        scratch_shapes=[pltpu.VMEM((2, HG_HEADS_PER_STEP, HG_DV, HG_DK), F32)],
        compiler_params=_cparams(("parallel", "parallel", "arbitrary")),
        name="hgrn2_scan",
    )(lb_fwd.reshape(1, BRANCH_W), lb_bwd.reshape(1, BRANCH_W), norm_g.reshape(1, HG_DV),
      z, z, z, z, z, z, z, z)


def _merge_kernel(x_ref, o0_ref, o1_ref, o2_ref, o3_ref, wg_ref, bg_ref, wbr_ref, wout_ref, g_ref, b_ref,
                  out_ref, acc_ref, xb_ref, ob_ref):
    j = pl.program_id(1)

    @pl.when(j == 0)
    def _():
        xb_ref[...] = x_ref[...].astype(BF16)
        for bi, o_ref in enumerate((o0_ref, o1_ref, o2_ref, o3_ref)):
            ob_ref[bi] = o_ref[...].astype(BF16)
        acc_ref[...] = jnp.zeros_like(acc_ref)

    xb = xb_ref[...]
    merged = None
    for bi in range(N_BRANCH):
        gate = jax.nn.sigmoid(jnp.dot(xb, wg_ref[bi], preferred_element_type=F32) + bg_ref[bi])
        term = gate * jnp.dot(ob_ref[bi], wbr_ref[bi], preferred_element_type=F32)
        merged = term if merged is None else merged + term
    acc_ref[...] += jnp.dot(merged.astype(BF16), wout_ref[...], preferred_element_type=F32)

    @pl.when(j == pl.num_programs(1) - 1)
    def _():
        out_ref[...] = _ln_rows(DN_ALPHA * x_ref[...] + acc_ref[...], g_ref[...], b_ref[...])


def _merge(x, branches, w_gate, b_gate, w_br, w_out, ln_g, ln_b, tm=512, tn=256):
    t, d = x.shape
    row = pl.BlockSpec((tm, d), lambda i, j: (i, 0))
    brow = pl.BlockSpec((tm, BRANCH_W), lambda i, j: (i, 0))
    vec = pl.BlockSpec((1, d), lambda i, j: (0, 0))
    return pl.pallas_call(
        _merge_kernel,
        out_shape=jax.ShapeDtypeStruct((t, d), F32),
        grid=(t // tm, d // tn),
        in_specs=[
            row, brow, brow, brow, brow,
            pl.BlockSpec((N_BRANCH, d, tn), lambda i, j: (0, 0, j)),
            pl.BlockSpec((N_BRANCH, 1, tn), lambda i, j: (0, 0, j)),
            pl.BlockSpec((N_BRANCH, BRANCH_W, tn), lambda i, j: (0, 0, j)),
            pl.BlockSpec((tn, d), lambda i, j: (j, 0)),
            vec, vec,
        ],
        out_specs=row,
        scratch_shapes=[pltpu.VMEM((tm, d), F32), pltpu.VMEM((tm, d), BF16),
                        pltpu.VMEM((N_BRANCH, tm, BRANCH_W), BF16)],
        compiler_params=_cparams(("parallel", "arbitrary")),
        name="gated_merge",
    )(x, *branches, w_gate, b_gate.reshape(N_BRANCH, 1, d), w_br, w_out, ln_g.reshape(1, d), ln_b.reshape(1, d))


def _expert_kernel(x_ref, gate_ref, w1_ref, w3_ref, w2_ref, o_ref):
    f = pl.program_id(2)
    x = x_ref[...]
    h1 = jnp.dot(x, w1_ref[...], preferred_element_type=F32)
    h3 = jnp.dot(x, w3_ref[...], preferred_element_type=F32)
    h = (jax.nn.silu(h1) * h3).astype(BF16)
    part = jnp.dot(h, w2_ref[...], preferred_element_type=F32)

    @pl.when(f == 0)
    def _():
        o_ref[...] = part

    @pl.when(f != 0)
    def _():
        o_ref[...] += part

    @pl.when(f == pl.num_programs(2) - 1)
    def _():
        o_ref[...] = o_ref[...] * gate_ref[...]


def _expert_ffn(xe, gate, w1, w3, w2, tm=512, tf=1024):
    e, c, d = xe.shape
    ff = w1.shape[2]
    tm = min(tm, c)
    return pl.pallas_call(
        _expert_kernel,
        out_shape=jax.ShapeDtypeStruct((e, c, d), F32),
        grid=(e, c // tm, ff // tf),
        in_specs=[
            pl.BlockSpec((None, tm, d), lambda ei, i, f: (ei, i, 0)),
            pl.BlockSpec((None, tm, 1), lambda ei, i, f: (ei, i, 0)),
            pl.BlockSpec((None, d, tf), lambda ei, i, f: (ei, 0, f)),
            pl.BlockSpec((None, d, tf), lambda ei, i, f: (ei, 0, f)),
            pl.BlockSpec((None, tf, d), lambda ei, i, f: (ei, f, 0)),
        ],
        out_specs=pl.BlockSpec((None, tm, d), lambda ei, i, f: (ei, i, 0)),
        compiler_params=_cparams(("parallel", "parallel", "arbitrary")),
        name="expert_ffn",
    )(xe, gate, w1, w3, w2)


RT_TM = 1024
RT_WIN = 32
RT_EAGER = 2
RT_CHUNKS = 4


def _router_kernel(x_ref, w_ref, o_ref):
    logits = lax.dot_general(w_ref[...], x_ref[...].astype(BF16), (((1,), (1,)), ((), ())),
                             preferred_element_type=F32)
    e = jnp.exp(logits - jnp.max(logits, axis=0, keepdims=True))
    aff = e / jnp.sum(e, axis=0, keepdims=True)
    for c in range(x_ref.shape[0] // LANE):
        o_ref[:, c, :] = aff[:, c * LANE:(c + 1) * LANE]


def _router(x, w_t):
    t, d = x.shape
    return pl.pallas_call(
        _router_kernel,
        out_shape=jax.ShapeDtypeStruct((N_EXPERTS, t // LANE, LANE), F32),
        grid=(t // RT_TM,),
        in_specs=[pl.BlockSpec((RT_TM, d), lambda i: (i, 0)), pl.BlockSpec((N_EXPERTS, d), lambda i: (0, 0))],
        out_specs=pl.BlockSpec((N_EXPERTS, RT_TM // LANE, LANE), lambda i: (0, i, 0)),
        compiler_params=_cparams(("parallel",)),
        name="moe_router",
    )(x, w_t)


def _prefix_count(mask):
    nch = mask.shape[0]
    m = jnp.where(mask, 1.0, 0.0)
    li = lax.broadcasted_iota(jnp.int32, (LANE, LANE), 0)
    lj = lax.broadcasted_iota(jnp.int32, (LANE, LANE), 1)
    incl = jnp.dot(m.astype(BF16), jnp.where(li <= lj, 1.0, 0.0).astype(BF16), preferred_element_type=F32)
    tot = jnp.broadcast_to(incl[:, LANE - 1:LANE], (nch, LANE))
    ri = lax.broadcasted_iota(jnp.int32, (nch, nch), 0)
    ci = lax.broadcasted_iota(jnp.int32, (nch, nch), 1)
    start = jnp.dot(jnp.where(ci < ri, 1.0, 0.0).astype(BF16), tot.astype(BF16), preferred_element_type=F32)
    return incl - m + start, start


def _route_select_kernel(cap, a_ref, pos_ref, start_ref):
    bits = pltpu.bitcast(a_ref[...], jnp.int32)

    def step(i, tau):
        cand = jnp.bitwise_or(tau, lax.shift_left(jnp.int32(1), 30 - i))
        cnt = jnp.sum(jnp.where(bits >= cand, 1.0, 0.0))
        return jnp.where(cnt >= cap, cand, tau)

    tau = lax.fori_loop(0, 31, step, jnp.int32(0))
    gt = bits > tau
    eq = bits == tau
    need = cap - jnp.sum(jnp.where(gt, 1.0, 0.0))
    eq_rank, _ = _prefix_count(eq)
    sel = jnp.logical_or(gt, jnp.logical_and(eq, eq_rank < need))
    pos, start = _prefix_count(sel)
    pos_ref[...] = jnp.where(sel, pos, -1.0)
    start_ref[...] = start


def _route_select(aff3, cap):
    e, nch, _ = aff3.shape
    spec = pl.BlockSpec((None, nch, LANE), lambda ei: (ei, 0, 0))
    return pl.pallas_call(
        functools.partial(_route_select_kernel, float(cap)),
        out_shape=[jax.ShapeDtypeStruct(aff3.shape, F32)] * 2,
        grid=(e,),
        in_specs=[spec],
        out_specs=[spec, spec],
        compiler_params=_cparams(("parallel",)),
        name="moe_route_select",
    )(aff3)


def _route_compact_kernel(cap, start_sm, pos_ref, a_ref, idx_ref, gate_ref, il_ref, gl_ref):
    ei = pl.program_id(0)
    nch = pos_ref.shape[0]
    il_ref[...] = jnp.zeros_like(il_ref)
    gl_ref[...] = jnp.zeros_like(gl_ref)
    s_iota = lax.broadcasted_iota(jnp.int32, (RT_WIN, LANE), 0).astype(F32)
    lane = lax.broadcasted_iota(jnp.int32, (1, LANE), 1).astype(F32)
    log_win = RT_WIN.bit_length() - 1

    def body(i, carry):
        rest = []
        for u in range(RT_CHUNKS):
            j = i * RT_CHUNKS + u
            first = start_sm[ei, j]
            nxt = jnp.where(j + 1 < nch, start_sm[ei, jnp.minimum(j + 1, nch - 1)], cap)
            base = lax.shift_left(lax.shift_right_logical(first, log_win), log_win)
            nwin = lax.shift_right_logical(nxt - base + (RT_WIN - 1), log_win)
            prow = pos_ref[pl.ds(j, 1), :]
            arow = a_ref[pl.ds(j, 1), :]
            tok = lane + jnp.asarray(j * LANE, F32)

            def window(w, c2, base=base, prow=prow, arow=arow, tok=tok):
                lo = pl.multiple_of(base + w * RT_WIN, RT_WIN)
                hit = (prow - jnp.asarray(lo, F32)) == s_iota
                il_ref[pl.ds(lo, RT_WIN), :] += jnp.sum(jnp.where(hit, tok, 0.0), axis=1, keepdims=True)
                gl_ref[pl.ds(lo, RT_WIN), :] += jnp.sum(jnp.where(hit, arow, 0.0), axis=1, keepdims=True)
                return c2

            for w in range(RT_EAGER):
                window(w, 0)
            rest.append((nwin, window))
        for nwin, window in rest:
            lax.fori_loop(RT_EAGER, nwin, window, 0)
        return carry

    lax.fori_loop(0, nch // RT_CHUNKS, body, 0)
    diag = (lax.broadcasted_iota(jnp.int32, (LANE, LANE), 0) == lax.broadcasted_iota(jnp.int32, (LANE, LANE), 1))
    for r in range(idx_ref.shape[0]):
        rows = slice(r * LANE, (r + 1) * LANE)
        idx_ref[r:r + 1, :] = jnp.sum(jnp.where(diag, il_ref[rows, :], 0.0), axis=0, keepdims=True).astype(jnp.int32)
        gate_ref[r:r + 1, :] = jnp.sum(jnp.where(diag, gl_ref[rows, :], 0.0), axis=0, keepdims=True)


def _route_compact(start, pos3, aff3, cap):
    e, nch, _ = pos3.shape
    spec = pl.BlockSpec((None, nch, LANE), lambda ei, s: (ei, 0, 0))
    ospec = pl.BlockSpec((None, cap // LANE, LANE), lambda ei, s: (ei, 0, 0))
    return pl.pallas_call(
        functools.partial(_route_compact_kernel, cap),
        out_shape=[jax.ShapeDtypeStruct((e, cap // LANE, LANE), jnp.int32),
                   jax.ShapeDtypeStruct((e, cap // LANE, LANE), F32)],
        grid_spec=pltpu.PrefetchScalarGridSpec(
            num_scalar_prefetch=1, grid=(e,), in_specs=[spec, spec], out_specs=[ospec, ospec],
            scratch_shapes=[pltpu.VMEM((cap + 2 * LANE, LANE), F32), pltpu.VMEM((cap + 2 * LANE, LANE), F32)]),
        compiler_params=_cparams(("parallel",)),
        name="moe_route_compact",
    )(start, pos3, aff3)


def _expert_choice(x, w_t, cap):
    aff3 = _router(x, w_t)
    pos3, start3 = _route_select(aff3, cap)
    idx3, gate3 = _route_compact(start3[:, :, 0].astype(jnp.int32), pos3, aff3, cap)
    return idx3.reshape(N_EXPERTS, cap), gate3.reshape(N_EXPERTS, cap)


CB_TS = 512
CB_SLOTS = 3
CB_GROUP = 8


def _combine_kernel(nper, idx_ref, idxn_ref, ye_ref, y_in, y_ref, buf, sem_r, sem_w):
    del y_in
    g = pl.program_id(0)
    ng = pl.num_programs(0)
    i = lax.rem(g, nper)
    slot = lax.rem(g, CB_SLOTS)
    nslot = lax.rem(g + 1, CB_SLOTS)
    pslot = lax.rem(g + CB_SLOTS - 1, CB_SLOTS)
    first = i == 0

    def row_read(row, s, j):
        return pltpu.make_async_copy(y_ref.at[pl.ds(row, 1)], buf.at[s, pl.ds(j, 1)], sem_r.at[s])

    def row_write(row, s, j):
        return pltpu.make_async_copy(buf.at[s, pl.ds(j, 1)], y_ref.at[pl.ds(row, 1)], sem_w.at[s])

    def for_rows(fn):
        lax.fori_loop(0, CB_TS, lambda j, c: (fn(j), c)[1], 0, unroll=8)

    def for_rows_alternating(fn):
        def group(gi, c):
            for u in range(CB_GROUP):
                fn(gi * CB_GROUP + u, u % 2)
            return c
        lax.fori_loop(0, CB_TS // CB_GROUP, group, 0)

    def start_reads(ids_ref, s):
        for_rows_alternating(lambda j, pr: row_read(ids_ref[0, j], s, j).start(priority=pr))

    def wait_reads(s):
        for_rows(lambda j: row_read(0, s, 0).wait())

    def wait_writes(s):
        for_rows(lambda j: row_write(0, s, 0).wait())

    @pl.when(jnp.logical_and(g >= 2, i != 1))
    def _():
        wait_writes(nslot)

    @pl.when(jnp.logical_and(first, g >= 1))
    def _():
        wait_writes(pslot)

    @pl.when(first)
    def _():
        start_reads(idx_ref, slot)

    @pl.when(i != nper - 1)
    def _():
        start_reads(idxn_ref, nslot)

    wait_reads(slot)
    buf[slot] = buf[slot] + ye_ref[...]
    for_rows_alternating(lambda j, pr: row_write(idx_ref[0, j], slot, j).start(priority=pr))

    @pl.when(g == ng - 1)
    def _():
        wait_writes(pslot)
        wait_writes(slot)


def _moe_combine(idx, ye, t):
    e, cap, d = ye.shape
    nper = cap // CB_TS
    assert nper >= 2
    ng = e * nper
    idx3 = idx.reshape(ng, 1, CB_TS)
    sm_spec = lambda off: pl.BlockSpec((None, 1, CB_TS), lambda g: (jnp.minimum(g + off, ng - 1), 0, 0),
                                       memory_space=pltpu.SMEM)
    return pl.pallas_call(
        functools.partial(_combine_kernel, nper),
        out_shape=jax.ShapeDtypeStruct((t, d), F32),
        grid=(ng,),
        in_specs=[sm_spec(0), sm_spec(1), pl.BlockSpec((CB_TS, d), lambda g: (g, 0)),
                  pl.BlockSpec(memory_space=pl.ANY)],
        out_specs=pl.BlockSpec(memory_space=pl.ANY),
        scratch_shapes=[pltpu.VMEM((CB_SLOTS, CB_TS, d), F32), pltpu.SemaphoreType.DMA((CB_SLOTS,)),
                        pltpu.SemaphoreType.DMA((CB_SLOTS,))],
        input_output_aliases={3: 0},
        compiler_params=_cparams(("arbitrary",)),
        name="moe_combine",
    )(idx3, idx3, ye.reshape(e * cap, d), jnp.zeros((t, d), F32))


def _gather_kernel(idx_ref, idxn_ref, x_ref, o_ref, buf, sem):
    g = pl.program_id(0)
    ng = pl.num_programs(0)
    slot = lax.rem(g, 2)

    def row_read(row, s, j):
        return pltpu.make_async_copy(x_ref.at[pl.ds(row, 1)], buf.at[s, pl.ds(j, 1)], sem.at[s])

    def start_reads(ids_ref, s):
        def group(gi, c):
            for u in range(CB_GROUP):
                j = gi * CB_GROUP + u
                row_read(ids_ref[0, j], s, j).start(priority=u % 2)
            return c
        lax.fori_loop(0, CB_TS // CB_GROUP, group, 0)

    @pl.when(g == 0)
    def _():
        start_reads(idx_ref, slot)

    @pl.when(g + 1 < ng)
    def _():
        start_reads(idxn_ref, 1 - slot)

    lax.fori_loop(0, CB_TS, lambda j, c: (row_read(0, slot, 0).wait(), c)[1], 0, unroll=8)
    o_ref[...] = buf[slot].astype(BF16)


def _moe_gather(x, idx):
    e, cap = idx.shape
    d = x.shape[1]
    ng = e * cap // CB_TS
    idx3 = idx.reshape(ng, 1, CB_TS)
    sm_spec = lambda off: pl.BlockSpec((None, 1, CB_TS), lambda g: (jnp.minimum(g + off, ng - 1), 0, 0),
                                       memory_space=pltpu.SMEM)
    out = pl.pallas_call(
        _gather_kernel,
        out_shape=jax.ShapeDtypeStruct((e * cap, d), BF16),
        grid=(ng,),
        in_specs=[sm_spec(0), sm_spec(1), pl.BlockSpec(memory_space=pl.ANY)],
        out_specs=pl.BlockSpec((CB_TS, d), lambda g: (g, 0)),
        scratch_shapes=[pltpu.VMEM((2, CB_TS, d), F32), pltpu.SemaphoreType.DMA((2,))],
        compiler_params=_cparams(("arbitrary",)),
        name="moe_gather",
    )(idx3, idx3, x)
    return out.reshape(e, cap, d)


def _attn_prep_kernel(zq_ref, zk_ref, zv_ref, cos_ref, sin_ref, q_ref, k_ref, v_ref):
    lane = lax.broadcasted_iota(jnp.int32, (1, LANE), 1)
    first = jnp.bitwise_and(lane, DA_HD - 1) < ROPE_DIM // 2
    cos, sin = cos_ref[...], sin_ref[...]

    def rope(x):
        partner = jnp.where(first, pltpu.roll(x, LANE - ROPE_DIM // 2, 1), pltpu.roll(x, ROPE_DIM // 2, 1))
        return x * cos + partner * sin

    q_ref[...] = (rope(zq_ref[...]) * (DA_HD ** -0.5)).astype(BF16)
    k_ref[...] = rope(zk_ref[...]).astype(BF16)
    v_ref[...] = zv_ref[...].astype(BF16)


def _rope_lane_tables(l):
    inv = ROPE_THETA ** (-jnp.arange(0, ROPE_DIM, 2, dtype=F32) / ROPE_DIM)
    ang = jnp.arange(l, dtype=F32)[:, None] * inv[None, :]
    c, s = jnp.cos(ang), jnp.sin(ang)
    rest = DA_HD - ROPE_DIM
    cos_h = jnp.concatenate([c, c, jnp.ones((l, rest), F32)], axis=1)
    sin_h = jnp.concatenate([-s, s, jnp.zeros((l, rest), F32)], axis=1)
    return jnp.concatenate([cos_h, cos_h], axis=1), jnp.concatenate([sin_h, sin_h], axis=1)


def _attn_prep(z, cos, sin, b, l, tm=1024):
    n = l // tm
    base = DA_OFF // LANE

    def zspec(part):
        return pl.BlockSpec((tm, LANE), lambda bi, hi, i: (bi * n + i, base + part * DA_HEADS + hi))

    tspec = pl.BlockSpec((tm, LANE), lambda bi, hi, i: (i, 0))
    ospec = pl.BlockSpec((None, None, tm, LANE), lambda bi, hi, i: (bi, hi, i, 0))
    return pl.pallas_call(
        _attn_prep_kernel,
        out_shape=[jax.ShapeDtypeStruct((b, DA_HEADS, l, DA_VD), BF16)] * 3,
        grid=(b, DA_HEADS, n),
        in_specs=[zspec(0), zspec(1), zspec(2), tspec, tspec],
        out_specs=[ospec, ospec, ospec],
        compiler_params=_cparams(("parallel", "parallel", "parallel")),
        name="attention_prep",
    )(z, z, z, cos, sin)


def _trunk(x, mem, p):
    B, L, D = x.shape
    T = B * L
    cos, sin = _rope_lane_tables(L)
    lb_all = jnp.cumsum(jax.nn.softmax(p['hg_lb_raw'], axis=1), axis=1)
    lb_all = lb_all - lb_all[:, :1]
    xf = _layer_norm(x.reshape(T, D), p['ln_in_g'], p['ln_in_b'])
    memf = mem.reshape(B * mem.shape[1], D)
    cap = EC_FACTOR * T // N_EXPERTS
    for l in range(DEPTH):
        z = _matmul(xf, p['w_in'][l], 1024, 1024)
        hy_kr, hy_ki = p['hy_spectrum'][L][l]
        o_hy = _hyena(z, p['hy_short_w'][l], p['hy_short_b'][l], p['hy_skip'][l], hy_kr, hy_ki,
                      p['dft_tables'][L], B, L)
        o_hg = _hgrn2(z, lb_all[0, l], lb_all[1, l], p['hg_norm_g'][l], B, L)
        qr, kr, vr = _attn_prep(z, cos, sin, B, L)
        lam_init = 0.8 - 0.6 * math.exp(-0.3 * l)
        o_da = _diff_attention(qr, kr, vr, p['da_lambda'][l], p['da_norm_g'][l], lam_init)
        kv = _matmul(memf, p['w_mem_kv'][l], min(512, memf.shape[0]), 512).reshape(B, -1, 2 * BRANCH_W)
        o_me = _memory_attention(z, kv, B, L)
        xf = _merge(xf, (o_hy, o_hg, o_da, o_me), p['w_gate'][l], p['b_gate'][l], p['w_br'][l], p['w_out'][l],
                    p['ln1_g'][l], p['ln1_b'][l])
        idx, gate = _expert_choice(xf, p['w_router_t'][l], cap)
        xe = _moe_gather(xf, idx)
        ye = _expert_ffn(xe, gate[..., None], p['w_e1'][l], p['w_e3'][l], p['w_e2'][l])
        y = _moe_combine(idx, ye, T)
        xf = _layer_norm(xf, p['ln2_g'][l], p['ln2_b'][l], resid=y, alpha=DN_ALPHA)
    return xf.reshape(B, L, D)


def kernel(x_prompt, x_sample, mem_prompt, mem_sample, ln_in_g, ln_in_b, w_in, hy_short_w, hy_short_b, hy_ffn_w1, hy_ffn_b1, hy_ffn_w2, hy_ffn_b2, hy_ffn_w3, hy_ffn_b3, hy_freq, hy_skip, hg_lb_raw, hg_norm_g, da_lambda, da_norm_g, w_mem_kv, w_gate, b_gate, w_br, w_out, ln1_g, ln1_b, w_router, w_e1, w_e3, w_e2, ln2_g, ln2_b):
    p = dict(ln_in_g=ln_in_g, ln_in_b=ln_in_b, hy_short_w=hy_short_w, hy_short_b=hy_short_b,
             hy_ffn_w1=hy_ffn_w1, hy_ffn_b1=hy_ffn_b1, hy_ffn_w2=hy_ffn_w2, hy_ffn_b2=hy_ffn_b2,
             hy_ffn_w3=hy_ffn_w3, hy_ffn_b3=hy_ffn_b3, hy_freq=hy_freq, hy_skip=hy_skip, hg_lb_raw=hg_lb_raw,
             hg_norm_g=hg_norm_g, da_lambda=da_lambda, da_norm_g=da_norm_g, b_gate=b_gate,
             ln1_g=ln1_g, ln1_b=ln1_b, ln2_g=ln2_g, ln2_b=ln2_b)
    for name, w in (('w_in', w_in), ('w_mem_kv', w_mem_kv), ('w_gate', w_gate), ('w_br', w_br), ('w_out', w_out),
                    ('w_e1', w_e1), ('w_e3', w_e3), ('w_e2', w_e2)):
        p[name] = w.astype(BF16)
    p['w_router_t'] = jnp.swapaxes(w_router, 1, 2).astype(BF16)
    p['dft_tables'], p['hy_spectrum'] = {}, {}
    for seq_len in sorted({x_prompt.shape[1], x_sample.shape[1]}):
        tabs = _dft_tables(seq_len)
        p['dft_tables'][seq_len] = tabs
        p['hy_spectrum'][seq_len] = [
            _hyena_filter_spectrum(seq_len, hy_ffn_w1[l], hy_ffn_b1[l], hy_ffn_w2[l], hy_ffn_b2[l], hy_ffn_w3[l],
                                   hy_ffn_b3[l], hy_freq[l], tabs) for l in range(DEPTH)]
    y_prompt = _trunk(x_prompt, mem_prompt, p)
    y_sample = _trunk(x_sample, mem_sample, p)
    return (y_prompt, y_sample)
```
